```python
import jax, jax.numpy as jnp
from jax import lax
import numpy as np

D_MODEL = 1024
BATCH = 8
SEQ = 2048
DEPTH = 4
DEC_BATCH = 32
DEC_SEQ = 8
PAST_LEN = 8192
PAGE_SIZE = 128

N_EVEN = (DEPTH + 1) // 2
N_ODD = DEPTH // 2
D_FF = 4 * D_MODEL
A_WIDTH = D_MODEL // 2
A_GROUPS = 4
A_CG = A_WIDTH // A_GROUPS
CHUNK = 128
B_WIDTH = D_MODEL // 2
B_HEAD_DIM = 64
B_HEADS = B_WIDTH // B_HEAD_DIM
B_LORA_W = 32
B_LORA_A = 32
B_LORA_G = 64
B_COLS = 3 * B_WIDTH + B_LORA_W + B_LORA_A + B_LORA_G
EVEN_IN = 2 * A_WIDTH + B_COLS
C_HEAD_DIM = 64
C_HEADS = D_MODEL // C_HEAD_DIM
ROT_DIM = C_HEAD_DIM // 4
ROPE_THETA = 500000.0
DILATED = ((128, 1), (512, 4), (2048, 16))
WIN_MAX = 2048
BAND = 128
NORM_EPS = 1e-6
LN_EPS = 1e-5
GN_EPS = 64e-5
NEG = -1e30

kernel_name = 'hybrid_gmlp_rwkv7_dilated_attn_decoder_step'


def rmsnorm(x, g):
    xf = x.astype(jnp.float32)
    return xf * lax.rsqrt(jnp.mean(xf * xf, axis=-1, keepdims=True) + NORM_EPS) * g


def ada_params(c, w, b):
    m = jax.nn.silu(c.astype(jnp.float32)) @ w + b
    return jnp.split(m[:, None, :], 6, axis=-1)


def modulate(x, g, shift, scale):
    return rmsnorm(x, g) * (1.0 + scale) + shift


def sq_relu_ffn(h, w1, w2):
    return jnp.square(jax.nn.relu(h @ w1)) @ w2


def gmlp_spatial_gate(pa, ln_g, ln_b, ws, bs):
    n, t_len, _ = pa.shape
    u = jax.nn.gelu(pa[..., :A_WIDTH], approximate=False)
    v = jax.nn.gelu(pa[..., A_WIDTH:], approximate=False).astype(jnp.float32)
    mu = jnp.mean(v, axis=-1, keepdims=True)
    var = jnp.mean(jnp.square(v - mu), axis=-1, keepdims=True)
    v = (v - mu) * lax.rsqrt(var + LN_EPS) * ln_g + ln_b
    n_chunks = -(-t_len // CHUNK)
    t_pad = n_chunks * CHUNK
    vc = jnp.pad(v, ((0, 0), (0, t_pad - t_len), (0, 0))).reshape(n, n_chunks, CHUNK, A_GROUPS, A_CG)
    w_causal = ws * jnp.tril(jnp.ones((CHUNK, CHUNK), jnp.float32))
    z = jnp.einsum('gij,bnjgc->bnigc', w_causal, vc) + bs.T[None, None, :, :, None]
    z = z.reshape(n, t_pad, A_WIDTH)[:, :t_len]
    return u * z, v


def wkv7_scan(s0, r, w, k, v, a, b):
    def step(s, xs):
        r_t, w_t, k_t, v_t, a_t, b_t = xs
        sa = jnp.einsum('bhvk,bhk->bhv', s, a_t)
        s = s * w_t[:, :, None, :] + sa[..., :, None] * b_t[..., None, :] + v_t[..., :, None] * k_t[..., None, :]
        return s, jnp.einsum('bhvk,bhk->bhv', s, r_t)
    xs = tuple(jnp.swapaxes(t, 0, 1) for t in (r, w, k, v, a, b))
    s, o = lax.scan(step, s0, xs)
    return s, jnp.swapaxes(o, 0, 1)


def rwkv7_time_mix(pb, prev, s0, mu, w0, w2, a0, a2, g2, k_k, k_a, r_k, lnx_g, lnx_b):
    f32 = jnp.float32
    n, t_len, _ = pb.shape
    pb = pb.astype(f32)
    shifted = jnp.concatenate([prev[:, None, :].astype(f32), pb[:, :-1]], axis=1)
    xm = pb + mu * (shifted - pb)
    cuts = [B_WIDTH, 2 * B_WIDTH, 3 * B_WIDTH, 3 * B_WIDTH + B_LORA_W, 3 * B_WIDTH + B_LORA_W + B_LORA_A]
    r, k, v, xw, xa, xg = jnp.split(xm, cuts, axis=-1)
    w_log = -jax.nn.softplus(-(w0 + jnp.tanh(xw) @ w2)) - 0.5
    decay = jnp.exp(-jnp.exp(w_log))
    a = jax.nn.sigmoid(a0 + xa @ a2)
    g = jax.nn.sigmoid(xg) @ g2
    heads = lambda t: t.reshape(n, t_len, B_HEADS, B_HEAD_DIM)
    kk = heads(k * k_k)
    kk = kk * lax.rsqrt(jnp.sum(kk * kk, axis=-1, keepdims=True) + 1e-12)
    k = k * (1.0 + (a - 1.0) * k_a)
    rh, kh, vh, ah = heads(r), heads(k), heads(v), heads(a)
    s, o = wkv7_scan(s0.astype(f32), rh, heads(decay), kh, vh, -kk, kk * ah)
    mo = jnp.mean(o, axis=-1, keepdims=True)
    vo = jnp.mean(jnp.square(o - mo), axis=-1, keepdims=True)
    o = ((o - mo) * lax.rsqrt(vo + GN_EPS)).reshape(n, t_len, B_WIDTH) * lnx_g + lnx_b
    bonus = jnp.sum(rh * kh * r_k, axis=-1, keepdims=True) * vh
    return (o + bonus.reshape(n, t_len, B_WIDTH)) * g, s


def even_mixer(h, s0, prev, w_in, w_out, ln_g, ln_b, ws, bs, mu, w0, w2, a0, a2, g2, k_k, k_a, r_k, lnx_g, lnx_b):
    proj = h @ w_in
    out_a, v_rows = gmlp_spatial_gate(proj[..., :2 * A_WIDTH], ln_g, ln_b, ws, bs)
    pb = proj[..., 2 * A_WIDTH:]
    out_b, s = rwkv7_time_mix(pb, prev, s0, mu, w0, w2, a0, a2, g2, k_k, k_a, r_k, lnx_g, lnx_b)
    y = jnp.concatenate([out_a, out_b], axis=-1) @ w_out
    return y, s, pb[:, -1].astype(jnp.float32), v_rows


def rope_partial(t, pos):
    half = ROT_DIM // 2
    inv = ROPE_THETA ** (-jnp.arange(half, dtype=jnp.float32) * (2.0 / ROT_DIM))
    ang = pos.astype(jnp.float32)[:, None] * inv[None, :]
    cos = jnp.cos(ang)[None, :, None, :]
    sin = jnp.sin(ang)[None, :, None, :]
    t1, t2, rest = t[..., :half], t[..., half:ROT_DIM], t[..., ROT_DIM:]
    return jnp.concatenate([t1 * cos - t2 * sin, t2 * cos + t1 * sin, rest], axis=-1)


def qkv_heads(h, w_qkv, pos):
    n, t_len, _ = h.shape
    q, k, v = jnp.split((h @ w_qkv).astype(jnp.float32), 3, axis=-1)
    q, k, v = (t.reshape(n, t_len, C_HEADS, C_HEAD_DIM) for t in (q, k, v))
    return rope_partial(q, pos), rope_partial(k, pos), v


def dilated_band_attn(q, k, v, window, dil):
    n, s_len, n_h, e = q.shape
    sub_len = s_len // dil
    sub_w = window // dil
    nb = -(-sub_len // BAND)
    l_pad = nb * BAND
    scale = C_HEAD_DIM ** -0.5

    def to_sub(t):
        t = t.reshape(n, sub_len, dil, n_h, e).transpose(0, 2, 1, 3, 4)
        t = jnp.pad(t, ((0, 0), (0, 0), (0, l_pad - sub_len), (0, 0), (0, 0)))
        return t.reshape(n, dil, nb, BAND, n_h, e)

    def band(t):
        prev = jnp.pad(t, ((0, 0), (0, 0), (1, 0), (0, 0), (0, 0), (0, 0)))[:, :, :-1]
        return jnp.concatenate([prev, t], axis=3)

    qb, kb, vb = to_sub(q), to_sub(k), to_sub(v)
    kband, vband = band(kb), band(vb)
    s = jnp.einsum('brnqhe,brnkhe->brnhqk', qb, kband) * scale
    dist = (jnp.arange(BAND)[:, None] + BAND) - jnp.arange(2 * BAND)[None, :]
    key_m = jnp.arange(nb)[:, None] * BAND - BAND + jnp.arange(2 * BAND)[None, :]
    mask = ((dist >= 0) & (dist <= sub_w))[None, :, :] & (key_m >= 0)[:, None, :]
    s = jnp.where(mask[None, None, :, None], s, NEG)
    m = jnp.max(s, axis=-1, keepdims=True)
    p = jnp.exp(s - m)
    den = jnp.sum(p, axis=-1, keepdims=True)
    o = jnp.einsum('brnhqk,brnkhe->brnqhe', p, vband) / jnp.swapaxes(den, 3, 4)
    lse = jnp.swapaxes((m + jnp.log(den))[..., 0], 3, 4)

    def from_sub(t):
        t = t.reshape((n, dil, l_pad) + t.shape[4:])[:, :, :sub_len]
        t = jnp.swapaxes(t, 1, 2)
        return t.reshape((n, s_len) + t.shape[3:])

    return from_sub(o), from_sub(lse)


def dilated_gather_attn(q, k_all, v_all, window, dil):
    n, t_len, n_h, e = q.shape
    wb = k_all.shape[1] - t_len
    j = jnp.arange(window // dil + 1)
    t = jnp.arange(t_len)
    idx = wb + t[:, None] - j[None, :] * dil
    valid = (idx >= 0) & (PAST_LEN + t[:, None] - j[None, :] * dil >= 0)
    idx = jnp.maximum(idx, 0)
    kg = k_all[:, idx]
    vg = v_all[:, idx]
    s = jnp.einsum('bthe,btjhe->bthj', q, kg) * (C_HEAD_DIM ** -0.5)
    s = jnp.where(valid[None, :, None, :], s, NEG)
    m = jnp.max(s, axis=-1, keepdims=True)
    p = jnp.exp(s - m)
    den = jnp.sum(p, axis=-1, keepdims=True)
    o = jnp.einsum('bthj,btjhe->bthe', p, vg) / den
    return o, (m + jnp.log(den))[..., 0]


def mix_dilations(outs, lses):
    wts = jax.nn.softmax(jnp.stack(lses, axis=0), axis=0)
    return jnp.sum(jnp.stack(outs, axis=0) * wts[..., None], axis=0)


def attn_prompt(h, w_qkv, w_out):
    n, s_len, _ = h.shape
    q, k, v = qkv_heads(h, w_qkv, jnp.arange(s_len))
    res = [dilated_band_attn(q, k, v, w, d) for (w, d) in DILATED]
    o = mix_dilations([r[0] for r in res], [r[1] for r in res])
    keep = min(WIN_MAX, s_len)
    return o.reshape(n, s_len, D_MODEL) @ w_out, k[:, s_len - keep:], v[:, s_len - keep:]


def attn_sample(h, ck, cv, w_qkv, w_out):
    n, t_len, _ = h.shape
    q, k, v = qkv_heads(h, w_qkv, PAST_LEN + jnp.arange(t_len))
    k_all = jnp.concatenate([ck.astype(jnp.float32), k], axis=1)
    v_all = jnp.concatenate([cv.astype(jnp.float32), v], axis=1)
    res = [dilated_gather_attn(q, k_all, v_all, w, d) for (w, d) in DILATED]
    o = mix_dilations([r[0] for r in res], [r[1] for r in res])
    return o.reshape(n, t_len, D_MODEL) @ w_out, k, v


def setup_inputs(seed: int = 0) -> dict:
    key = jax.random.key(seed)
    keys = iter(jax.random.split(key, 48))
    f32 = jnp.float32
    D = D_MODEL
    wbuf = min(WIN_MAX, PAST_LEN)
    nrm = lambda shape, std: jax.random.normal(next(keys), shape, f32) * std
    uni = lambda shape, lo, hi: jax.random.uniform(next(keys), shape, f32, lo, hi)
    return {
        'x_prompt': nrm((BATCH, SEQ, D), 1.0),
        'x_sample': nrm((DEC_BATCH, DEC_SEQ, D), 1.0),
        'state_wkv': nrm((N_EVEN, DEC_BATCH, B_HEADS, B_HEAD_DIM, B_HEAD_DIM), 0.1),
        'state_shift': nrm((N_EVEN, DEC_BATCH, B_COLS), 1.0),
        'cache_k': nrm((N_ODD, DEC_BATCH, wbuf, C_HEADS, C_HEAD_DIM), 1.0),
        'cache_v': nrm((N_ODD, DEC_BATCH, wbuf, C_HEADS, C_HEAD_DIM), 1.0),
        'c_prompt': nrm((BATCH, D), 1.0),
        'c_sample': nrm((DEC_BATCH, D), 1.0),
        'ada_w': nrm((DEPTH, D, 6 * D), 0.5 * D ** -0.5),
        'ada_b': nrm((DEPTH, 6 * D), 0.02),
        'norm_mix_pre': 1.0 + nrm((DEPTH, D), 0.05),
        'norm_mix_post': 1.0 + nrm((DEPTH, D), 0.05),
        'norm_ffn_pre': 1.0 + nrm((DEPTH, D), 0.05),
        'norm_ffn_post': 1.0 + nrm((DEPTH, D), 0.05),
        'ffn_w1': nrm((DEPTH, D, D_FF), D ** -0.5),
        'ffn_w2': nrm((DEPTH, D_FF, D), D_FF ** -0.5),
        'ev_w_in': nrm((N_EVEN, D, EVEN_IN), D ** -0.5),
        'ev_w_out': nrm((N_EVEN, A_WIDTH + B_WIDTH, D), (A_WIDTH + B_WIDTH) ** -0.5),
        'gm_ln_g': 1.0 + nrm((N_EVEN, A_WIDTH), 0.05),
        'gm_ln_b': nrm((N_EVEN, A_WIDTH), 0.02),
        'gm_ws': nrm((N_EVEN, A_GROUPS, CHUNK, CHUNK), CHUNK ** -0.5),
        'gm_bs': 1.0 + nrm((N_EVEN, A_GROUPS, CHUNK), 0.05),
        'rw_mu': uni((N_EVEN, B_COLS), 0.0, 1.0),
        'rw_w0': uni((N_EVEN, B_WIDTH), -3.0, 1.0),
        'rw_w2': nrm((N_EVEN, B_LORA_W, B_WIDTH), B_LORA_W ** -0.5),
        'rw_a0': nrm((N_EVEN, B_WIDTH), 0.5),
        'rw_a2': nrm((N_EVEN, B_LORA_A, B_WIDTH), B_LORA_A ** -0.5),
        'rw_g2': nrm((N_EVEN, B_LORA_G, B_WIDTH), B_LORA_G ** -0.5),
        'rw_kk': 0.85 + nrm((N_EVEN, B_WIDTH), 0.05),
        'rw_ka': 1.0 + nrm((N_EVEN, B_WIDTH), 0.05),
        'rw_rk': nrm((N_EVEN, B_HEADS, B_HEAD_DIM), 0.1),
        'rw_lnx_g': 1.0 + nrm((N_EVEN, B_WIDTH), 0.05),
        'rw_lnx_b': nrm((N_EVEN, B_WIDTH), 0.02),
        'od_w_qkv': nrm((N_ODD, D, 3 * D), D ** -0.5),
        'od_w_out': nrm((N_ODD, D, D), D ** -0.5),
    }


def reference(x_prompt, x_sample, state_wkv, state_shift, cache_k, cache_v, c_prompt, c_sample,
              ada_w, ada_b, norm_mix_pre, norm_mix_post, norm_ffn_pre, norm_ffn_post, ffn_w1, ffn_w2,
              ev_w_in, ev_w_out, gm_ln_g, gm_ln_b, gm_ws, gm_bs, rw_mu, rw_w0, rw_w2, rw_a0, rw_a2,
              rw_g2, rw_kk, rw_ka, rw_rk, rw_lnx_g, rw_lnx_b, od_w_qkv, od_w_out):
    f32 = jnp.float32
    xp = x_prompt.astype(f32)
    xs = x_sample.astype(f32)
    n_p = x_prompt.shape[0]
    wkv_p, shift_p, k_p, v_p = [], [], [], []
    wkv_s, shift_s, k_s, v_s, gv_s = [], [], [], [], []
    for l in range(DEPTH):
        shm_p, scm_p, gtm_p, shf_p, scf_p, gtf_p = ada_params(c_prompt, ada_w[l], ada_b[l])
        shm_s, scm_s, gtm_s, shf_s, scf_s, gtf_s = ada_params(c_sample, ada_w[l], ada_b[l])
        hp = modulate(xp, norm_mix_pre[l], shm_p, scm_p)
        hs = modulate(xs, norm_mix_pre[l], shm_s, scm_s)
        if l % 2 == 0:
            e = l // 2
            ew = (ev_w_in[e], ev_w_out[e], gm_ln_g[e], gm_ln_b[e], gm_ws[e], gm_bs[e], rw_mu[e],
                  rw_w0[e], rw_w2[e], rw_a0[e], rw_a2[e], rw_g2[e], rw_kk[e], rw_ka[e], rw_rk[e],
                  rw_lnx_g[e], rw_lnx_b[e])
            s0 = jnp.zeros((n_p, B_HEADS, B_HEAD_DIM, B_HEAD_DIM), f32)
            prev0 = jnp.zeros((n_p, B_COLS), f32)
            yp, sp, lastp, _ = even_mixer(hp, s0, prev0, *ew)
            ys, ss, lasts, gvs = even_mixer(hs, state_wkv[e], state_shift[e], *ew)
            wkv_p.append(sp)
            shift_p.append(lastp)
            wkv_s.append(ss)
            shift_s.append(lasts)
            gv_s.append(gvs)
        else:
            o = l // 2
            yp, kp, vp = attn_prompt(hp, od_w_qkv[o], od_w_out[o])
            ys, kss, vss = attn_sample(hs, cache_k[o], cache_v[o], od_w_qkv[o], od_w_out[o])
            k_p.append(kp)
            v_p.append(vp)
            k_s.append(kss)
            v_s.append(vss)
        xp = xp + gtm_p * rmsnorm(yp, norm_mix_post[l])
        xs = xs + gtm_s * rmsnorm(ys, norm_mix_post[l])
        fp = modulate(xp, norm_ffn_pre[l], shf_p, scf_p)
        fs = modulate(xs, norm_ffn_pre[l], shf_s, scf_s)
        xp = xp + gtf_p * rmsnorm(sq_relu_ffn(fp, ffn_w1[l], ffn_w2[l]), norm_ffn_post[l])
        xs = xs + gtf_s * rmsnorm(sq_relu_ffn(fs, ffn_w1[l], ffn_w2[l]), norm_ffn_post[l])
    return (xp, xs, jnp.stack(wkv_p), jnp.stack(shift_p), jnp.stack(k_p), jnp.stack(v_p),
            jnp.stack(wkv_s), jnp.stack(shift_s), jnp.stack(k_s), jnp.stack(v_s), jnp.stack(gv_s))
```

```python
import functools
import math

import jax
import jax.numpy as jnp
from jax import lax
from jax.experimental import pallas as pl
from jax.experimental.pallas import tpu as pltpu

F32 = jnp.float32
BF16 = jnp.bfloat16

D = 1024
DEPTH = 4
D_FF = 4 * D
A_WIDTH = 512
A_GROUPS = 4
CHUNK = 128
B_WIDTH = 512
B_HEADS = 8
HEAD = 64
LORA = 128
B_COLS = 3 * B_WIDTH + LORA
C_HEADS = 16
ROT_DIM = 16
ROPE_THETA = 500000.0
DILATIONS = (1, 4, 16)
BAND = 128
PAST_LEN = 8192
NORM_EPS = 1e-6
LN_EPS = 1e-5
GN_EPS = 64e-5
NEG = -1e30

LANES = 128
VMEM_LIMIT = 52 * 1024 * 1024

RW_PASSES_SOLVE = 3
RW_PASSES_MAIN = 3


def _cparams(*sem):
    return pltpu.CompilerParams(dimension_semantics=sem, vmem_limit_bytes=VMEM_LIMIT)


def _split(x):
    hi = x.astype(BF16)
    return hi, (x - hi.astype(F32)).astype(BF16)


def _mm(a, b, dims=((1,), (0,)), passes=1):
    dn = (dims, ((), ()))
    dot = lambda p, q: lax.dot_general(p, q, dn, preferred_element_type=F32)
    if passes == 1:
        return dot(a.astype(BF16), b.astype(BF16))
    ah, al = _split(a)
    if passes == 2:
        bh = b.astype(BF16)
        return dot(ah, bh) + dot(al, bh)
    bh, bl = _split(b)
    return dot(ah, bh) + (dot(ah, bl) + dot(al, bh))


NT = ((1,), (1,))
TN = ((0,), (0,))


def _rms(x):
    return x * lax.rsqrt(jnp.mean(x * x, axis=-1, keepdims=True) + NORM_EPS)


def _mod_spec(mods, layer, chunk, tiles_per_group):
    return pl.BlockSpec((None, None, mods.shape[2], D), lambda i, *_: (layer, i // tiles_per_group, 0, chunk))


def _row_spec(tm, n):
    return pl.BlockSpec((tm, n), lambda i, *_: (i, 0))


def _const_spec(shape):
    nd = len(shape)
    return pl.BlockSpec(shape, lambda *_: (0,) * nd)


def _ada_body(c_ref, w_ref, b_ref, o_ref):
    c = c_ref[...]
    o_ref[...] = _mm(c * jax.nn.sigmoid(c), w_ref[...]) + b_ref[...]


def _ada(c_all, ada_w, ada_b):
    n = c_all.shape[0]
    tn = 1536
    return pl.pallas_call(
        _ada_body,
        grid=(DEPTH, 6 * D // tn),
        in_specs=[pl.BlockSpec((n, D), lambda l, j: (0, 0)),
                  pl.BlockSpec((None, D, tn), lambda l, j: (l, 0, j)),
                  pl.BlockSpec((None, 1, tn), lambda l, j: (l, 0, j))],
        out_specs=pl.BlockSpec((None, n, tn), lambda l, j: (l, 0, j)),
        out_shape=jax.ShapeDtypeStruct((DEPTH, n, 6 * D), F32),
        compiler_params=_cparams("parallel", "parallel"),
        name="ada",
    )(c_all, ada_w, ada_b.reshape(DEPTH, 1, 6 * D))


def _modnorm(x, g, sh, sc):
    return _rms(x) * g * (1.0 + sc) + sh


def _inproj_body(x_ref, g_ref, sh_ref, sc_ref, w_ref, pa_ref, pb_ref):
    h = _modnorm(x_ref[...], g_ref[...], sh_ref[...], sc_ref[...]).astype(BF16)
    pa_ref[...] = _mm(h, w_ref[:, :2 * A_WIDTH])
    pb_ref[...] = _mm(h, w_ref[:, 2 * A_WIDTH:])


def _inproj(x, g, mods, layer, tm, w):
    r = x.shape[0]
    tpg = (r // mods.shape[1]) // tm
    n = w.shape[1]
    return pl.pallas_call(
        _inproj_body,
        grid=(r // tm,),
        in_specs=[_row_spec(tm, D), _const_spec((1, D)), _mod_spec(mods, layer, 0, tpg),
                  _mod_spec(mods, layer, 1, tpg), _const_spec((D, n))],
        out_specs=[_row_spec(tm, 2 * A_WIDTH), _row_spec(tm, B_COLS)],
        out_shape=[jax.ShapeDtypeStruct((r, 2 * A_WIDTH), F32), jax.ShapeDtypeStruct((r, B_COLS), F32)],
        compiler_params=_cparams("parallel"),
        name="even_inproj",
    )(x, g, mods, mods, w)


def _rope(seg, cos, s_lo, s_hi):
    return seg * cos + pltpu.roll(seg, LANES - ROT_DIM // 2, 1) * s_lo + pltpu.roll(seg, ROT_DIM // 2, 1) * s_hi


def _qkv_body(x_ref, g_ref, sh_ref, sc_ref, w_ref, cos_ref, slo_ref, shi_ref,
              q_ref, kb_ref, vb_ref, kf_ref, vf_ref):
    h = _modnorm(x_ref[...], g_ref[...], sh_ref[...], sc_ref[...]).astype(BF16)
    cos, s_lo, s_hi = cos_ref[...], slo_ref[...], shi_ref[...]
    scale = HEAD ** -0.5
    for grp in range(D // LANES):
        lanes = slice(grp * LANES, (grp + 1) * LANES)
        q = _rope(_mm(h, w_ref[:, grp * LANES:(grp + 1) * LANES]), cos, s_lo, s_hi)
        q_ref[:, lanes] = (q * scale).astype(q_ref.dtype)
        k = _rope(_mm(h, w_ref[:, D + grp * LANES:D + (grp + 1) * LANES]), cos, s_lo, s_hi)
        kf_ref[:, lanes] = k
        kb_ref[:, lanes] = k.astype(kb_ref.dtype)
    v = _mm(h, w_ref[:, 2 * D:])
    vf_ref[...] = v
    vb_ref[...] = v.astype(vb_ref.dtype)


def _rope_tables(pos):
    half = ROT_DIM // 2
    inv = ROPE_THETA ** (-jnp.arange(half, dtype=F32) * (2.0 / ROT_DIM))
    ang = pos.astype(F32)[:, None] * inv[None, :]
    cos, sin = jnp.cos(ang), jnp.sin(ang)
    t = pos.shape[0]
    one, zero, z8 = jnp.ones((t, HEAD - ROT_DIM), F32), jnp.zeros((t, HEAD - ROT_DIM), F32), jnp.zeros((t, half), F32)
    tab = lambda *parts: jnp.tile(jnp.concatenate(parts, axis=1), (1, LANES // HEAD))
    return tab(cos, cos, one), tab(-sin, z8, zero), tab(z8, sin, zero)


def _qkv(x, g, mods, layer, tm, w, tables, tables_per_seq, act_dtype):
    r = x.shape[0]
    tpg = (r // mods.shape[1]) // tm
    tab_spec = pl.BlockSpec((tm, LANES), lambda i: (i % tables_per_seq, 0))
    act = jax.ShapeDtypeStruct((r, D), act_dtype)
    f32 = jax.ShapeDtypeStruct((r, D), F32)
    return pl.pallas_call(
        _qkv_body,
        grid=(r // tm,),
        in_specs=[_row_spec(tm, D), _const_spec((1, D)), _mod_spec(mods, layer, 0, tpg),
                  _mod_spec(mods, layer, 1, tpg), _const_spec((D, 3 * D)), tab_spec, tab_spec, tab_spec],
        out_specs=[_row_spec(tm, D)] * 5,
        out_shape=[act, act, act, f32, f32],
        compiler_params=_cparams("parallel"),
        name="odd_qkv",
    )(x, g, mods, mods, w, *tables)


def _outproj_body(*refs, n_in):
    ins, ws = refs[:n_in], refs[n_in:2 * n_in]
    x_ref, gp_ref, gate_ref, o_ref = refs[2 * n_in:]
    y = _mm(ins[0][...], ws[0][...])
    for a, w in zip(ins[1:], ws[1:]):
        y = y + _mm(a[...], w[...])
    o_ref[...] = x_ref[...] + gate_ref[...] * (_rms(y) * gp_ref[...])


def _outproj(ins, ws, x, gp, mods, layer, tm):
    r = x.shape[0]
    tpg = (r // mods.shape[1]) // tm
    return pl.pallas_call(
        functools.partial(_outproj_body, n_in=len(ins)),
        grid=(r // tm,),
        in_specs=[_row_spec(tm, a.shape[1]) for a in ins] + [_const_spec(w.shape) for w in ws]
        + [_row_spec(tm, D), _const_spec((1, D)), _mod_spec(mods, layer, 2, tpg)],
        out_specs=_row_spec(tm, D),
        out_shape=jax.ShapeDtypeStruct((r, D), F32),
        compiler_params=_cparams("parallel"),
        name="outproj",
    )(*ins, *ws, x, gp, mods)


def _ffn_body(x_ref, g_ref, sh_ref, sc_ref, gate_ref, gp_ref, w1_ref, w2_ref, o_ref, h_scr, acc_scr):
    j = pl.program_id(1)

    @pl.when(j == 0)
    def _():
        h_scr[...] = _modnorm(x_ref[...], g_ref[...], sh_ref[...], sc_ref[...]).astype(BF16)
        acc_scr[...] = jnp.zeros_like(acc_scr)

    t = jnp.maximum(_mm(h_scr[...], w1_ref[...]), 0.0)
    acc_scr[...] += _mm(t * t, w2_ref[...])

    @pl.when(j == pl.num_programs(1) - 1)
    def _():
        o_ref[...] = x_ref[...] + gate_ref[...] * (_rms(acc_scr[...]) * gp_ref[...])


def _ffn(x, g, gp, mods, layer, tm, tf, w1, w2):
    r = x.shape[0]
    tpg = (r // mods.shape[1]) // tm
    return pl.pallas_call(
        _ffn_body,
        grid=(r // tm, D_FF // tf),
        in_specs=[_row_spec(tm, D), _const_spec((1, D)), _mod_spec(mods, layer, 3, tpg),
                  _mod_spec(mods, layer, 4, tpg), _mod_spec(mods, layer, 5, tpg), _const_spec((1, D)),
                  pl.BlockSpec((D, tf), lambda i, j: (0, j)), pl.BlockSpec((tf, D), lambda i, j: (j, 0))],
        out_specs=_row_spec(tm, D),
        out_shape=jax.ShapeDtypeStruct((r, D), F32),
        scratch_shapes=[pltpu.VMEM((tm, D), BF16), pltpu.VMEM((tm, D), F32)],
        compiler_params=_cparams("parallel", "arbitrary"),
        name="ffn",
    )(x, g, mods, mods, mods, gp, w1, w2)


def _gelu(x):
    return 0.5 * x * (1.0 + lax.erf(x * (2.0 ** -0.5)))


def _gmlp_body(pa_ref, lng_ref, lnb_ref, ws_ref, bias_ref, out_ref, *v_refs):
    pa = pa_ref[...]
    u = _gelu(pa[:, :A_WIDTH])
    v = _gelu(pa[:, A_WIDTH:])
    mu = jnp.mean(v, axis=-1, keepdims=True)
    vc = v - mu
    var = jnp.mean(vc * vc, axis=-1, keepdims=True)
    vn = vc * lax.rsqrt(var + LN_EPS) * lng_ref[...] + lnb_ref[...]
    if v_refs:
        v_refs[0][...] = vn
    ii = lax.broadcasted_iota(jnp.int32, (CHUNK, CHUNK), 0)
    jj = lax.broadcasted_iota(jnp.int32, (CHUNK, CHUNK), 1)
    cg = A_WIDTH // A_GROUPS
    for grp in range(A_GROUPS):
        lanes = slice(grp * cg, (grp + 1) * cg)
        wc = jnp.where(jj <= ii, ws_ref[grp], 0.0).astype(BF16)
        for ci in range(pa.shape[0] // CHUNK):
            rows = slice(ci * CHUNK, (ci + 1) * CHUNK)
            z = _mm(wc, vn[rows, lanes]) + bias_ref[:, lanes]
            out_ref[rows, lanes] = u[rows, lanes] * z


def _gmlp(pa, ln_g, ln_b, ws, bias, tm, emit_v):
    r = pa.shape[0]
    out = jax.ShapeDtypeStruct((r, A_WIDTH), F32)
    spec = _row_spec(tm, A_WIDTH)
    return pl.pallas_call(
        _gmlp_body,
        grid=(r // tm,),
        in_specs=[_row_spec(tm, 2 * A_WIDTH), _const_spec((1, A_WIDTH)), _const_spec((1, A_WIDTH)),
                  _const_spec(ws.shape), _const_spec(bias.shape)],
        out_specs=[spec, spec] if emit_v else spec,
        out_shape=[out, out] if emit_v else out,
        compiler_params=_cparams("parallel"),
        name="gmlp",
    )(pa, ln_g, ln_b, ws, bias)


def _seg_cumsum(x, seg):
    row = lax.broadcasted_iota(jnp.int32, (x.shape[0], 1), 0) % seg
    s = 1
    while s < seg:
        x = x + jnp.where(row >= s, pltpu.roll(x, s, 0), 0.0)
        s *= 2
    return x


def _wkv_chunk(s0, a_t, b_t, k_t, r_t, v_h, p_end, strict, incl, eye):
    pm, ps = RW_PASSES_MAIN, RW_PASSES_SOLVE
    c = a_t.shape[0]
    g_ab = jnp.where(strict, _mm(a_t, b_t, NT, pm), 0.0)
    g_ak = jnp.where(strict, _mm(a_t, k_t, NT, pm), 0.0)
    g_rb = jnp.where(incl, _mm(r_t, b_t, NT, pm), 0.0)
    g_rk = jnp.where(incl, _mm(r_t, k_t, NT, pm), 0.0)
    inv = eye + g_ab
    pw = g_ab
    for _ in range(int(math.log2(c)) - 1):
        pw = _mm(pw, pw, passes=ps)
        inv = inv + _mm(inv, pw, passes=ps)
    rhs = _mm(a_t, s0, NT, pm) + _mm(g_ak, v_h, passes=pm)
    u = _mm(inv, rhs, passes=ps)
    o = _mm(r_t, s0, NT, pm) + _mm(g_rb, u, passes=pm) + _mm(g_rk, v_h, passes=pm)
    s1 = (s0 + _mm(u, b_t, TN, pm) + _mm(v_h, k_t, TN, pm)) * p_end
    return o, s1


def _rwkv_body(pb_ref, prev_ref, s0_ref, mu_ref, prm_ref, lora_ref, hsum_ref, out_ref, s_ref, last_scr,
               *, nb, tt, c):
    @pl.when(pl.program_id(1) == 0)
    def _():
        s_ref[...] = s0_ref[...]
        last_scr[...] = prev_ref[...]

    mu = mu_ref[...]
    w0, a0, k_k, k_a, r_k, ln_g, ln_b = (prm_ref[i:i + 1, :] for i in range(7))
    hsum = hsum_ref[...]
    headsum = lambda t: _mm(t, hsum, passes=2)
    row = lax.broadcasted_iota(jnp.int32, (tt, 1), 0)
    ii = lax.broadcasted_iota(jnp.int32, (c, c), 0)
    jj = lax.broadcasted_iota(jnp.int32, (c, c), 1)
    strict, incl = jj < ii, jj <= ii
    eye = jnp.where(ii == jj, 1.0, 0.0).astype(F32)

    for s in range(nb):
        x = pb_ref[s]
        shifted = jnp.where(row == 0, last_scr[s], pltpu.roll(x, 1, 0))
        last_scr[s] = x[tt - 1:tt, :]
        xm = x + mu * (shifted - x)
        r, k, v = xm[:, :B_WIDTH], xm[:, B_WIDTH:2 * B_WIDTH], xm[:, 2 * B_WIDTH:3 * B_WIDTH]
        lo = xm[:, 3 * B_WIDTH:]
        z = -(w0 + _mm(jnp.tanh(lo), lora_ref[0]))
        softplus = jnp.maximum(z, 0.0) + jnp.log1p(jnp.exp(-jnp.abs(z)))
        lw = -jnp.exp(-softplus - 0.5)
        a = jax.nn.sigmoid(a0 + _mm(lo, lora_ref[1]))
        g = _mm(jax.nn.sigmoid(lo), lora_ref[2])
        kk = k * k_k
        kk = kk * lax.rsqrt(headsum(kk * kk) + 1e-12)
        k2 = k * (1.0 + (a - 1.0) * k_a)
        lp = _seg_cumsum(lw, c)
        e_pos, e_neg = jnp.exp(lp), jnp.exp(-lp)
        a_t = -kk * jnp.exp(lp - lw)
        b_t = kk * a * e_neg
        k_t = k2 * e_neg
        r_t = r * e_pos
        o_rows = []
        for ci in range(tt // c):
            rows = slice(ci * c, (ci + 1) * c)
            p_end = e_pos[(ci + 1) * c - 1:(ci + 1) * c, :]
            o_heads = []
            for h in range(B_HEADS):
                hs = slice(h * HEAD, (h + 1) * HEAD)
                o, s1 = _wkv_chunk(s_ref[s, h], a_t[rows, hs], b_t[rows, hs], k_t[rows, hs], r_t[rows, hs],
                                   v[rows, hs], p_end[:, hs], strict, incl, eye)
                s_ref[s, h] = s1
                o_heads.append(o)
            o_rows.append(jnp.concatenate(o_heads, axis=1))
        o = o_rows[0] if len(o_rows) == 1 else jnp.concatenate(o_rows, axis=0)
        inv_n = 1.0 / HEAD
        oc = o - headsum(o) * inv_n
        on = oc * lax.rsqrt(headsum(oc * oc) * inv_n + GN_EPS) * ln_g + ln_b
        bonus = headsum(r * k2 * r_k) * v
        out_ref[s] = (on + bonus) * g


def _rwkv(pb, prev, s0, mu, prm, lora, hsum, nb, tt, c):
    n, t, _ = pb.shape
    return pl.pallas_call(
        functools.partial(_rwkv_body, nb=nb, tt=tt, c=c),
        grid=(n // nb, t // tt),
        in_specs=[pl.BlockSpec((nb, tt, B_COLS), lambda i, j: (i, j, 0)),
                  pl.BlockSpec((nb, 1, B_COLS), lambda i, j: (i, 0, 0)),
                  pl.BlockSpec((nb, B_HEADS, HEAD, HEAD), lambda i, j: (i, 0, 0, 0)),
                  _const_spec(mu.shape), _const_spec(prm.shape), _const_spec(lora.shape), _const_spec(hsum.shape)],
        out_specs=[pl.BlockSpec((nb, tt, B_WIDTH), lambda i, j: (i, j, 0)),
                   pl.BlockSpec((nb, B_HEADS, HEAD, HEAD), lambda i, j: (i, 0, 0, 0))],
        out_shape=[jax.ShapeDtypeStruct((n, t, B_WIDTH), F32), jax.ShapeDtypeStruct(s0.shape, F32)],
        scratch_shapes=[pltpu.VMEM((nb, 1, B_COLS), F32)],
        compiler_params=_cparams("parallel", "arbitrary"),
        name="rwkv7",
    )(pb, prev, s0, mu, prm, lora, hsum)


def _band_attn_body(q_ref, k_ref, v_ref, acc_ref, st_ref, *, nblk):
    n = pl.program_id(2)
    nk = BAND if nblk == 1 else 2 * BAND
    cur = pl.multiple_of(n * BAND, BAND)
    prev = pl.multiple_of(jnp.maximum(n - 1, 0) * BAND, BAND)

    def keys(ref, lanes):
        if nblk == 1:
            return ref[:, lanes]
        return jnp.concatenate([ref[pl.ds(prev, BAND), lanes], ref[pl.ds(cur, BAND), lanes]], axis=0)

    qi = lax.broadcasted_iota(jnp.int32, (BAND, nk), 0)
    kc = lax.broadcasted_iota(jnp.int32, (BAND, nk), 1)
    if nblk == 1:
        mask = kc <= qi
    else:
        first_key = jnp.where(n > 0, 0, BAND)
        mask = (kc >= qi) & (kc <= qi + BAND) & (kc >= first_key)
    lane = lax.broadcasted_iota(jnp.int32, (BAND, LANES), 1)
    stats = jnp.zeros((BAND, LANES), F32)
    for pair in range(C_HEADS // 2):
        lanes = slice(pair * LANES, (pair + 1) * LANES)
        q2 = q_ref[:, lanes]
        k2 = keys(k_ref, lanes)
        v2 = keys(v_ref, lanes)
        accs = []
        for half in range(2):
            head = 2 * pair + half
            in_head = (lane >= HEAD) if half else (lane < HEAD)
            s = _mm(jnp.where(in_head, q2, jnp.zeros_like(q2)), k2, NT)
            s = jnp.where(mask, s, NEG)
            m = jnp.max(s, axis=-1, keepdims=True)
            p = jnp.exp(s - m)
            den = jnp.sum(p, axis=-1, keepdims=True)
            accs.append(_mm(p, v2))
            stats = jnp.where(lane == head, m, stats)
            stats = jnp.where(lane == C_HEADS + head, den, stats)
        acc_ref[:, lanes] = jnp.where(lane < HEAD, accs[0], accs[1]).astype(acc_ref.dtype)
    st_ref[...] = stats


def _band_attn(q, k, v, nseq, dil):
    s_len = q.shape[0] // nseq
    sub = s_len // dil
    nblk = sub // BAND
    view = lambda t: t.reshape(nseq, sub, dil * D)
    blk = pl.BlockSpec((None, BAND, D), lambda b, r, n: (b, n, r))
    cls = pl.BlockSpec((None, sub, D), lambda b, r, n: (b, 0, r))
    acc, st = pl.pallas_call(
        functools.partial(_band_attn_body, nblk=nblk),
        grid=(nseq, dil, nblk),
        in_specs=[blk, cls, cls],
        out_specs=[blk, pl.BlockSpec((None, BAND, LANES), lambda b, r, n: (b, n, r))],
        out_shape=[jax.ShapeDtypeStruct((nseq, sub, dil * D), q.dtype),
                   jax.ShapeDtypeStruct((nseq, sub, dil * LANES), F32)],
        compiler_params=_cparams("parallel", "parallel", "arbitrary"),
        name=f"band_attn_d{dil}",
    )(view(q), view(k), view(v))
    return acc.reshape(nseq * s_len, D), st.reshape(nseq * s_len, LANES)


def _mix_body(a1_ref, a4_ref, a16_ref, s1_ref, s4_ref, s16_ref, e_ref, w_ref, x_ref, gp_ref, gate_ref, o_ref):
    sts = [s1_ref[...], s4_ref[...], s16_ref[...]]
    lane = lax.broadcasted_iota(jnp.int32, sts[0].shape, 1)
    top = jnp.maximum(jnp.maximum(sts[0], sts[1]), sts[2])
    es = [jnp.exp(st - top) for st in sts]
    dens = [pltpu.roll(st, LANES - C_HEADS, 1) for st in sts]
    total = es[0] * dens[0] + es[1] * dens[1] + es[2] * dens[2]
    o = None
    for e, acc_ref in zip(es, (a1_ref, a4_ref, a16_ref)):
        coef = jnp.where(lane < C_HEADS, e / total, 0.0)
        term = _mm(coef, e_ref[...], passes=2) * acc_ref[...].astype(F32)
        o = term if o is None else o + term
    y = _mm(o, w_ref[...])
    o_ref[...] = x_ref[...] + gate_ref[...] * (_rms(y) * gp_ref[...])


def _mix_outproj(accs, sts, expand, w, x, gp, mods, layer, tm):
    r = x.shape[0]
    tpg = (r // mods.shape[1]) // tm
    return pl.pallas_call(
        _mix_body,
        grid=(r // tm,),
        in_specs=[_row_spec(tm, D)] * 3 + [_row_spec(tm, LANES)] * 3
        + [_const_spec(expand.shape), _const_spec(w.shape), _row_spec(tm, D), _const_spec((1, D)),
           _mod_spec(mods, layer, 2, tpg)],
        out_specs=_row_spec(tm, D),
        out_shape=jax.ShapeDtypeStruct((r, D), F32),
        compiler_params=_cparams("parallel"),
        name="attn_mix_outproj",
    )(*accs, *sts, expand, w, x, gp, mods)


def _attn_sample_body(q_ref, kn_ref, vn_ref, k1_ref, v1_ref, k4_ref, v4_ref, k16_ref, v16_ref, o_ref, *, t_len):
    ti = lax.broadcasted_iota(jnp.int32, (t_len, BAND), 0)
    mi = lax.broadcasted_iota(jnp.int32, (t_len, BAND), 1)
    tq = lax.broadcasted_iota(jnp.int32, (t_len, t_len), 0)
    tk = lax.broadcasted_iota(jnp.int32, (t_len, t_len), 1)
    mult = (jnp.where(tk == tq, 3.0, 0.0) + jnp.where(tk < tq, 1.0, 0.0) + jnp.where(tk == tq - 4, 1.0, 0.0))
    lane = lax.broadcasted_iota(jnp.int32, (t_len, LANES), 1)
    for pair in range(C_HEADS // 2):
        lanes = slice(pair * LANES, (pair + 1) * LANES)
        q2 = q_ref[:, lanes]
        kn, vn = kn_ref[:, lanes], vn_ref[:, lanes]
        outs = []
        for half in range(2):
            in_head = (lane >= HEAD) if half else (lane < HEAD)
            qh = jnp.where(in_head, q2, 0.0)
            blocks = []
            s = _mm(qh, kn, NT)
            blocks.append((jnp.where(mult > 0.0, s, NEG), vn, mult))
            s = _mm(qh, k1_ref[:, lanes], NT)
            blocks.append((jnp.where(mi >= ti, s, NEG), v1_ref[:, lanes], None))
            for rho in range(4):
                cols = slice(rho * D + pair * LANES, rho * D + (pair + 1) * LANES)
                s = _mm(qh, k4_ref[:, cols], NT)
                ok = ((ti % 4) == rho) & ((ti < 4) | (mi >= 1))
                blocks.append((jnp.where(ok, s, NEG), v4_ref[:, cols], None))
            for rho in range(t_len):
                cols = slice(rho * D + pair * LANES, rho * D + (pair + 1) * LANES)
                s = _mm(qh, k16_ref[:, cols], NT)
                blocks.append((jnp.where(ti == rho, s, NEG), v16_ref[:, cols], None))
            top = None
            for s, _, _ in blocks:
                m = jnp.max(s, axis=-1, keepdims=True)
                top = m if top is None else jnp.maximum(top, m)
            num, den = None, None
            for s, vals, w in blocks:
                p = jnp.exp(s - top)
                if w is not None:
                    p = p * w
                d = jnp.sum(p, axis=-1, keepdims=True)
                nm = _mm(p, vals)
                num = nm if num is None else num + nm
                den = d if den is None else den + d
            outs.append(num / den)
        o_ref[:, lanes] = jnp.where(lane < HEAD, outs[0], outs[1])


def _attn_sample(q, kn, vn, ck, cv, nseq, t_len):
    w = ck.shape[1]
    assert w == 2048 and t_len == 8
    new = pl.BlockSpec((t_len, D), lambda b: (b, 0))
    v1 = lambda t: t
    v4 = lambda t: t.reshape(nseq, w // 4, 4 * D)
    v16 = lambda t: t.reshape(nseq, w // 16, 16 * D)
    s1 = pl.BlockSpec((None, BAND, D), lambda b: (b, w // BAND - 1, 0))
    s4 = pl.BlockSpec((None, BAND, 4 * D), lambda b: (b, w // 4 // BAND - 1, 0))
    s16 = pl.BlockSpec((None, BAND, t_len * D), lambda b: (b, 0, 0))
    return pl.pallas_call(
        functools.partial(_attn_sample_body, t_len=t_len),
        grid=(nseq,),
        in_specs=[new, new, new, s1, s1, s4, s4, s16, s16],
        out_specs=new,
        out_shape=jax.ShapeDtypeStruct((nseq * t_len, D), F32),
        compiler_params=_cparams("parallel"),
        name="attn_sample",
    )(q, kn, vn, v1(ck), v1(cv), v4(ck), v4(cv), v16(ck), v16(cv))


def kernel(x_prompt, x_sample, state_wkv, state_shift, cache_k, cache_v, c_prompt, c_sample, ada_w, ada_b, norm_mix_pre, norm_mix_post, norm_ffn_pre, norm_ffn_post, ffn_w1, ffn_w2, ev_w_in, ev_w_out, gm_ln_g, gm_ln_b, gm_ws, gm_bs, rw_mu, rw_w0, rw_w2, rw_a0, rw_a2, rw_g2, rw_kk, rw_ka, rw_rk, rw_lnx_g, rw_lnx_b, od_w_qkv, od_w_out):
    n_p, s_len, _ = x_prompt.shape
    n_s, t_len, _ = x_sample.shape
    r_p, r_s = n_p * s_len, n_s * t_len
    row = lambda t: t.reshape(1, -1).astype(F32)

    mods = _ada(jnp.concatenate([c_prompt, c_sample], axis=0).astype(F32), ada_w, ada_b)
    mods_p = mods[:, :n_p].reshape(DEPTH, n_p, 1, 6 * D)
    mods_s = jnp.repeat(mods[:, n_p:], t_len, axis=1).reshape(DEPTH, 1, r_s, 6 * D)

    xp = x_prompt.astype(F32).reshape(r_p, D)
    xs = x_sample.astype(F32).reshape(r_s, D)

    hsum = jnp.kron(jnp.eye(B_HEADS, dtype=F32), jnp.ones((HEAD, HEAD), F32)).astype(BF16)
    expand = jnp.concatenate([jnp.kron(jnp.eye(C_HEADS, dtype=F32), jnp.ones((1, HEAD), F32)),
                              jnp.zeros((LANES - C_HEADS, D), F32)], axis=0).astype(BF16)
    tabs_p = _rope_tables(jnp.arange(s_len))
    tabs_s = _rope_tables(jnp.tile(PAST_LEN + jnp.arange(t_len), n_s))
    zeros_prev = jnp.zeros((n_p, 1, B_COLS), F32)
    zeros_state = jnp.zeros((n_p, B_HEADS, HEAD, HEAD), F32)
    reps = CHUNK // t_len

    tm_p = 512
    outs = {k: [] for k in ("wkv_p", "shift_p", "k_p", "v_p", "wkv_s", "shift_s", "k_s", "v_s", "gv_s")}
    for l in range(DEPTH):
        g_pre, g_post = row(norm_mix_pre[l]), row(norm_mix_post[l])
        if l % 2 == 0:
            e = l // 2
            w_in = ev_w_in[e].astype(BF16)
            w_out = ev_w_out[e].astype(BF16)
            w_oa, w_ob = w_out[:A_WIDTH], w_out[A_WIDTH:]
            ln_g, ln_b = row(gm_ln_g[e]), row(gm_ln_b[e])
            ws = gm_ws[e].astype(F32)
            bias_p = jnp.repeat(gm_bs[e].T, A_WIDTH // A_GROUPS, axis=1)
            ws_s = jax.vmap(lambda m: jnp.kron(jnp.eye(reps, dtype=F32), m))(ws[:, :t_len, :t_len])
            bias_s = jnp.tile(bias_p[:t_len], (reps, 1))
            mu = row(rw_mu[e])
            zpad = jnp.zeros((7 * B_WIDTH,), F32)
            prm = jnp.concatenate([rw_w0[e], rw_a0[e], rw_kk[e], rw_ka[e], rw_rk[e].reshape(-1), rw_lnx_g[e],
                                   rw_lnx_b[e], zpad[:B_WIDTH]]).reshape(8, B_WIDTH).astype(F32)
            zl = lambda a, b: jnp.zeros((a, b), F32)
            lora = jnp.stack([
                jnp.concatenate([rw_w2[e], zl(96, B_WIDTH)], axis=0),
                jnp.concatenate([zl(32, B_WIDTH), rw_a2[e], zl(64, B_WIDTH)], axis=0),
                jnp.concatenate([zl(64, B_WIDTH), rw_g2[e]], axis=0)]).astype(BF16)

            pa_p, pb_p = _inproj(xp, g_pre, mods_p, l, tm_p, w_in)
            oa_p = _gmlp(pa_p, ln_g, ln_b, ws, bias_p, 256, False)
            pb_p3 = pb_p.reshape(n_p, s_len, B_COLS)
            ob_p, st_p = _rwkv(pb_p3, zeros_prev, zeros_state, mu, prm, lora, hsum, 2, 64, 64)
            xp = _outproj([oa_p, ob_p.reshape(r_p, B_WIDTH)], [w_oa, w_ob], xp, g_post, mods_p, l, tm_p)
            outs["wkv_p"].append(st_p)
            outs["shift_p"].append(pb_p3[:, -1])

            pa_s, pb_s = _inproj(xs, g_pre, mods_s, l, r_s, w_in)
            oa_s, gv = _gmlp(pa_s, ln_g, ln_b, ws_s, bias_s, r_s, True)
            pb_s3 = pb_s.reshape(n_s, t_len, B_COLS)
            ob_s, st_s = _rwkv(pb_s3, state_shift[e].astype(F32).reshape(n_s, 1, B_COLS), state_wkv[e].astype(F32),
                               mu, prm, lora, hsum, 4, t_len, t_len)
            xs = _outproj([oa_s, ob_s.reshape(r_s, B_WIDTH)], [w_oa, w_ob], xs, g_post, mods_s, l, r_s)
            outs["wkv_s"].append(st_s)
            outs["shift_s"].append(pb_s3[:, -1])
            outs["gv_s"].append(gv.reshape(n_s, t_len, A_WIDTH))
        else:
            o = l // 2
            w_qkv = od_w_qkv[o].astype(BF16)
            w_out = od_w_out[o].astype(BF16)
            q, kb, vb, kf, vf = _qkv(xp, g_pre, mods_p, l, tm_p, w_qkv, tabs_p, s_len // tm_p, BF16)
            res = [_band_attn(q, kb, vb, n_p, dil) for dil in DILATIONS]
            xp = _mix_outproj([a for a, _ in res], [s for _, s in res], expand, w_out, xp, g_post, mods_p, l, tm_p)
            outs["k_p"].append(kf.reshape(n_p, s_len, C_HEADS, HEAD))
            outs["v_p"].append(vf.reshape(n_p, s_len, C_HEADS, HEAD))

            qs, _, _, kfs, vfs = _qkv(xs, g_pre, mods_s, l, r_s, w_qkv, tabs_s, 1, F32)
            ck = cache_k[o].astype(F32).reshape(n_s, -1, D)
            cv = cache_v[o].astype(F32).reshape(n_s, -1, D)
            att = _attn_sample(qs, kfs, vfs, ck, cv, n_s, t_len)
            xs = _outproj([att], [w_out], xs, g_post, mods_s, l, r_s)
            outs["k_s"].append(kfs.reshape(n_s, t_len, C_HEADS, HEAD))
            outs["v_s"].append(vfs.reshape(n_s, t_len, C_HEADS, HEAD))
        w1, w2 = ffn_w1[l].astype(BF16), ffn_w2[l].astype(BF16)
        g_fpre, g_fpost = row(norm_ffn_pre[l]), row(norm_ffn_post[l])
        xp = _ffn(xp, g_fpre, g_fpost, mods_p, l, 1024, 512, w1, w2)
        xs = _ffn(xs, g_fpre, g_fpost, mods_s, l, r_s, 512, w1, w2)

    st = lambda name: jnp.stack(outs[name])
    return (xp.reshape(n_p, s_len, D), xs.reshape(n_s, t_len, D), st("wkv_p"), st("shift_p"), st("k_p"),
            st("v_p"), st("wkv_s"), st("shift_s"), st("k_s"), st("v_s"), st("gv_s"))
```

```python
import functools
import math

import jax
import jax.numpy as jnp
from jax import lax
from jax.experimental import pallas as pl
from jax.experimental.pallas import tpu as pltpu

F32 = jnp.float32
BF16 = jnp.bfloat16

D = 1024
DEPTH = 4
D_FF = 4 * D
A_WIDTH = 512
A_GROUPS = 4
CHUNK = 128
B_WIDTH = 512
B_HEADS = 8
HEAD = 64
LORA = 128
B_COLS = 3 * B_WIDTH + LORA
C_HEADS = 16
ROT_DIM = 16
ROPE_THETA = 500000.0
DILATIONS = (1, 4, 16)
BAND = 128
PAST_LEN = 8192
NORM_EPS = 1e-6
LN_EPS = 1e-5
GN_EPS = 64e-5
NEG = -1e30

LANES = 128
VMEM_LIMIT = 52 * 1024 * 1024


def _cparams(*sem):
    return pltpu.CompilerParams(dimension_semantics=sem, vmem_limit_bytes=VMEM_LIMIT)


def _split(x):
    hi = x.astype(BF16)
    return hi, (x - hi.astype(F32)).astype(BF16)


def _mm(a, b, dims=((1,), (0,)), passes=1):
    dn = (dims, ((), ()))
    dot = lambda p, q: lax.dot_general(p, q, dn, preferred_element_type=F32)
    if passes == 1:
        return dot(a.astype(BF16), b.astype(BF16))
    ah, al = _split(a)
    return dot(ah, b) + dot(al, b)


NT = ((1,), (1,))
TN = ((0,), (0,))


def _rms(x):
    return x * lax.rsqrt(jnp.mean(x * x, axis=-1, keepdims=True) + NORM_EPS)


def _mod_spec(mods, layer, chunk, tiles_per_group):
    return pl.BlockSpec((None, None, mods.shape[2], D), lambda i, *_: (layer, i // tiles_per_group, 0, chunk))


def _row_spec(tm, n):
    return pl.BlockSpec((tm, n), lambda i, *_: (i, 0))


def _const_spec(shape):
    nd = len(shape)
    return pl.BlockSpec(shape, lambda *_: (0,) * nd)


def _ada_body(c_ref, w_ref, b_ref, o_ref):
    c = c_ref[...]
    o_ref[...] = _mm(c * jax.nn.sigmoid(c), w_ref[...]) + b_ref[...]


def _ada(c_all, ada_w, ada_b):
    n = c_all.shape[0]
    tn = 1536
    return pl.pallas_call(
        _ada_body,
        grid=(DEPTH, 6 * D // tn),
        in_specs=[pl.BlockSpec((n, D), lambda l, j: (0, 0)),
                  pl.BlockSpec((None, D, tn), lambda l, j: (l, 0, j)),
                  pl.BlockSpec((None, 1, tn), lambda l, j: (l, 0, j))],
        out_specs=pl.BlockSpec((None, n, tn), lambda l, j: (l, 0, j)),
        out_shape=jax.ShapeDtypeStruct((DEPTH, n, 6 * D), F32),
        compiler_params=_cparams("parallel", "parallel"),
        name="ada",
    )(c_all, ada_w, ada_b.reshape(DEPTH, 1, 6 * D))


def _modnorm(x, g, sh, sc):
    return _rms(x) * g * (1.0 + sc) + sh


def _inproj_body(x_ref, g_ref, sh_ref, sc_ref, w_ref, pa_ref, pb_ref):
    h = _modnorm(x_ref[...], g_ref[...], sh_ref[...], sc_ref[...]).astype(BF16)
    pa_ref[...] = _mm(h, w_ref[:, :2 * A_WIDTH])
    pb_ref[...] = _mm(h, w_ref[:, 2 * A_WIDTH:])


def _inproj(x, g, mods, layer, tm, w):
    r = x.shape[0]
    tpg = (r // mods.shape[1]) // tm
    n = w.shape[1]
    return pl.pallas_call(
        _inproj_body,
        grid=(r // tm,),
        in_specs=[_row_spec(tm, D), _const_spec((1, D)), _mod_spec(mods, layer, 0, tpg),
                  _mod_spec(mods, layer, 1, tpg), _const_spec((D, n))],
        out_specs=[_row_spec(tm, 2 * A_WIDTH), _row_spec(tm, B_COLS)],
        out_shape=[jax.ShapeDtypeStruct((r, 2 * A_WIDTH), F32), jax.ShapeDtypeStruct((r, B_COLS), F32)],
        compiler_params=_cparams("parallel"),
        name="even_inproj",
    )(x, g, mods, mods, w)


def _rope(seg, cos, s_lo, s_hi):
    return seg * cos + pltpu.roll(seg, LANES - ROT_DIM // 2, 1) * s_lo + pltpu.roll(seg, ROT_DIM // 2, 1) * s_hi


def _qkv_groups(x_ref, g_ref, sh_ref, sc_ref, w_ref, cos_ref, slo_ref, shi_ref):
    h = _modnorm(x_ref[...], g_ref[...], sh_ref[...], sc_ref[...]).astype(BF16)
    cos, s_lo, s_hi = cos_ref[...], slo_ref[...], shi_ref[...]
    for grp in range(D // LANES):
        cols = lambda part: slice(part * D + grp * LANES, part * D + (grp + 1) * LANES)
        q = _rope(_mm(h, w_ref[:, cols(0)]), cos, s_lo, s_hi) * HEAD ** -0.5
        k = _rope(_mm(h, w_ref[:, cols(1)]), cos, s_lo, s_hi)
        yield slice(grp * LANES, (grp + 1) * LANES), q, k, _mm(h, w_ref[:, cols(2)])


def _qkv_body(*refs):
    q_ref, kf_ref, vf_ref = refs[8:]
    for lanes, q, k, v in _qkv_groups(*refs[:8]):
        q_ref[:, lanes] = q
        kf_ref[:, lanes] = k
        vf_ref[:, lanes] = v


def _qkv_classes_body(*refs):
    q1, k1, v1, q4, k4, v4, q16, k16, v16, kf_ref, vf_ref, slab = refs[8:]
    tm = kf_ref.shape[0]
    ngrp = D // LANES
    for grp, (lanes, q, k, v) in enumerate(_qkv_groups(*refs[:8])):
        kf_ref[:, lanes] = k
        vf_ref[:, lanes] = v
        for which, (val, nat) in enumerate(((q, q1), (k, k1), (v, v1))):
            nat[:, lanes] = val.astype(BF16)
            slab[which * ngrp + grp] = val
    for which, (o4, o16) in enumerate(((q4, q16), (k4, k16), (v4, v16))):
        for grp in range(ngrp):
            lanes = slice(grp * LANES, (grp + 1) * LANES)
            for dil, out in ((4, o4), (16, o16)):
                for r in range(dil):
                    out[r, :, lanes] = slab[which * ngrp + grp, pl.ds(r, tm // dil, stride=dil), :].astype(BF16)


def _rope_tables(pos):
    half = ROT_DIM // 2
    inv = ROPE_THETA ** (-jnp.arange(half, dtype=F32) * (2.0 / ROT_DIM))
    ang = pos.astype(F32)[:, None] * inv[None, :]
    cos, sin = jnp.cos(ang), jnp.sin(ang)
    t = pos.shape[0]
    one, zero, z8 = jnp.ones((t, HEAD - ROT_DIM), F32), jnp.zeros((t, HEAD - ROT_DIM), F32), jnp.zeros((t, half), F32)
    tab = lambda *parts: jnp.tile(jnp.concatenate(parts, axis=1), (1, LANES // HEAD))
    return tab(cos, cos, one), tab(-sin, z8, zero), tab(z8, sin, zero)


def _qkv_in_specs(mods, layer, tm, tpg, tables_per_seq):
    tab_spec = pl.BlockSpec((tm, LANES), lambda i: (i % tables_per_seq, 0))
    return [_row_spec(tm, D), _const_spec((1, D)), _mod_spec(mods, layer, 0, tpg), _mod_spec(mods, layer, 1, tpg),
            _const_spec((D, 3 * D)), tab_spec, tab_spec, tab_spec]


def _qkv(x, g, mods, layer, tm, w, tables):
    r = x.shape[0]
    tpg = (r // mods.shape[1]) // tm
    f32 = jax.ShapeDtypeStruct((r, D), F32)
    return pl.pallas_call(
        _qkv_body,
        grid=(r // tm,),
        in_specs=_qkv_in_specs(mods, layer, tm, tpg, 1),
        out_specs=[_row_spec(tm, D)] * 3,
        out_shape=[f32, f32, f32],
        compiler_params=_cparams("parallel"),
        name="odd_qkv_decode",
    )(x, g, mods, mods, w, *tables)


def _qkv_classes(x, g, mods, layer, tm, w, tables, nseq):
    r = x.shape[0]
    s_len = r // nseq
    tps = s_len // tm
    cls_spec = lambda dil: pl.BlockSpec((None, dil, tm // dil, D), lambda i: (i // tps, 0, i % tps, 0))
    cls_shape = lambda dil: jax.ShapeDtypeStruct((nseq, dil, s_len // dil, D), BF16)
    nat = jax.ShapeDtypeStruct((r, D), BF16)
    f32 = jax.ShapeDtypeStruct((r, D), F32)
    return pl.pallas_call(
        _qkv_classes_body,
        grid=(r // tm,),
        in_specs=_qkv_in_specs(mods, layer, tm, tps, tps),
        out_specs=[_row_spec(tm, D)] * 3 + [cls_spec(4)] * 3 + [cls_spec(16)] * 3 + [_row_spec(tm, D)] * 2,
        out_shape=[nat] * 3 + [cls_shape(4)] * 3 + [cls_shape(16)] * 3 + [f32, f32],
        scratch_shapes=[pltpu.VMEM((3 * D // LANES, tm, LANES), F32)],
        compiler_params=_cparams("parallel"),
        name="odd_qkv",
    )(x, g, mods, mods, w, *tables)


def _outproj_body(*refs, n_in):
    ins, ws = refs[:n_in], refs[n_in:2 * n_in]
    x_ref, gp_ref, gate_ref, o_ref = refs[2 * n_in:]
    y = _mm(ins[0][...], ws[0][...])
    for a, w in zip(ins[1:], ws[1:]):
        y = y + _mm(a[...], w[...])
    o_ref[...] = x_ref[...] + gate_ref[...] * (_rms(y) * gp_ref[...])


def _outproj(ins, ws, x, gp, mods, layer, tm):
    r = x.shape[0]
    tpg = (r // mods.shape[1]) // tm
    return pl.pallas_call(
        functools.partial(_outproj_body, n_in=len(ins)),
        grid=(r // tm,),
        in_specs=[_row_spec(tm, a.shape[1]) for a in ins] + [_const_spec(w.shape) for w in ws]
        + [_row_spec(tm, D), _const_spec((1, D)), _mod_spec(mods, layer, 2, tpg)],
        out_specs=_row_spec(tm, D),
        out_shape=jax.ShapeDtypeStruct((r, D), F32),
        compiler_params=_cparams("parallel"),
        name="outproj",
    )(*ins, *ws, x, gp, mods)


def _ffn_body(x_ref, g_ref, sh_ref, sc_ref, gate_ref, gp_ref, w1_ref, w2_ref, o_ref, h_scr, acc_scr):
    j = pl.program_id(1)

    @pl.when(j == 0)
    def _():
        h_scr[...] = _modnorm(x_ref[...], g_ref[...], sh_ref[...], sc_ref[...]).astype(BF16)
        acc_scr[...] = jnp.zeros_like(acc_scr)

    t = jnp.maximum(_mm(h_scr[...], w1_ref[...]), 0.0)
    acc_scr[...] += _mm(t * t, w2_ref[...])

    @pl.when(j == pl.num_programs(1) - 1)
    def _():
        o_ref[...] = x_ref[...] + gate_ref[...] * (_rms(acc_scr[...]) * gp_ref[...])


def _ffn(x, g, gp, mods, layer, tm, tf, w1, w2):
    r = x.shape[0]
    tpg = (r // mods.shape[1]) // tm
    return pl.pallas_call(
        _ffn_body,
        grid=(r // tm, D_FF // tf),
        in_specs=[_row_spec(tm, D), _const_spec((1, D)), _mod_spec(mods, layer, 3, tpg),
                  _mod_spec(mods, layer, 4, tpg), _mod_spec(mods, layer, 5, tpg), _const_spec((1, D)),
                  pl.BlockSpec((D, tf), lambda i, j: (0, j)), pl.BlockSpec((tf, D), lambda i, j: (j, 0))],
        out_specs=_row_spec(tm, D),
        out_shape=jax.ShapeDtypeStruct((r, D), F32),
        scratch_shapes=[pltpu.VMEM((tm, D), BF16), pltpu.VMEM((tm, D), F32)],
        compiler_params=_cparams("parallel", "arbitrary"),
        name="ffn",
    )(x, g, mods, mods, mods, gp, w1, w2)


def _gelu(x):
    return 0.5 * x * (1.0 + lax.erf(x * (2.0 ** -0.5)))


def _gmlp_body(pa_ref, lng_ref, lnb_ref, ws_ref, bias_ref, out_ref, *v_refs):
    pa = pa_ref[...]
    u = _gelu(pa[:, :A_WIDTH])
    v = _gelu(pa[:, A_WIDTH:])
    mu = jnp.mean(v, axis=-1, keepdims=True)
    vc = v - mu
    var = jnp.mean(vc * vc, axis=-1, keepdims=True)
    vn = vc * lax.rsqrt(var + LN_EPS) * lng_ref[...] + lnb_ref[...]
    if v_refs:
        v_refs[0][...] = vn
    ii = lax.broadcasted_iota(jnp.int32, (CHUNK, CHUNK), 0)
    jj = lax.broadcasted_iota(jnp.int32, (CHUNK, CHUNK), 1)
    cg = A_WIDTH // A_GROUPS
    for grp in range(A_GROUPS):
        lanes = slice(grp * cg, (grp + 1) * cg)
        wc = jnp.where(jj <= ii, ws_ref[grp], 0.0).astype(BF16)
        for ci in range(pa.shape[0] // CHUNK):
            rows = slice(ci * CHUNK, (ci + 1) * CHUNK)
            z = _mm(wc, vn[rows, lanes]) + bias_ref[:, lanes]
            out_ref[rows, lanes] = u[rows, lanes] * z


def _gmlp(pa, ln_g, ln_b, ws, bias, tm, emit_v):
    r = pa.shape[0]
    out = jax.ShapeDtypeStruct((r, A_WIDTH), F32)
    spec = _row_spec(tm, A_WIDTH)
    return pl.pallas_call(
        _gmlp_body,
        grid=(r // tm,),
        in_specs=[_row_spec(tm, 2 * A_WIDTH), _const_spec((1, A_WIDTH)), _const_spec((1, A_WIDTH)),
                  _const_spec(ws.shape), _const_spec(bias.shape)],
        out_specs=[spec, spec] if emit_v else spec,
        out_shape=[out, out] if emit_v else out,
        compiler_params=_cparams("parallel"),
        name="gmlp",
    )(pa, ln_g, ln_b, ws, bias)


def _seg_cumsum(x, seg):
    row = lax.broadcasted_iota(jnp.int32, (x.shape[0], 1), 0) % seg
    s = 1
    while s < seg:
        x = x + jnp.where(row >= s, pltpu.roll(x, s, 0), 0.0)
        s *= 2
    return x


def _rwkv_body(pb_ref, prev_ref, s0_ref, mu_ref, prm_ref, lora_ref, hsum_ref, out_ref, s_ref, last_scr,
               *, nb, tt, c):
    @pl.when(pl.program_id(1) == 0)
    def _():
        s_ref[...] = s0_ref[...]
        last_scr[...] = prev_ref[...]

    mu = mu_ref[...]
    w0, a0, k_k, k_a, r_k, ln_g, ln_b = (prm_ref[i:i + 1, :] for i in range(7))
    hsum = hsum_ref[...]
    headsum = lambda t: _mm(t, hsum, passes=2)
    row = lax.broadcasted_iota(jnp.int32, (tt, 1), 0)
    ii = lax.broadcasted_iota(jnp.int32, (c, 2 * c), 0)
    jj = lax.broadcasted_iota(jnp.int32, (c, 2 * c), 1) % c
    strict, incl = jj < ii, jj <= ii
    eye = jnp.where(lax.broadcasted_iota(jnp.int32, (c, c), 0) == lax.broadcasted_iota(jnp.int32, (c, c), 1), 1.0, 0.0).astype(F32)
    zeros_c = jnp.zeros((c, HEAD), BF16)
    n_chunk = tt // c
    levels = int(math.log2(c)) - 1

    seqs = []
    for s in range(nb):
        x = pb_ref[s]
        shifted = jnp.where(row == 0, last_scr[s], pltpu.roll(x, 1, 0))
        last_scr[s] = x[tt - 1:tt, :]
        xm = x + mu * (shifted - x)
        r, k, v = xm[:, :B_WIDTH], xm[:, B_WIDTH:2 * B_WIDTH], xm[:, 2 * B_WIDTH:3 * B_WIDTH]
        lo = xm[:, 3 * B_WIDTH:]
        z = -(w0 + _mm(jnp.tanh(lo), lora_ref[0]))
        softplus = jnp.maximum(z, 0.0) + jnp.log1p(jnp.exp(-jnp.abs(z)))
        lw = -jnp.exp(-softplus - 0.5)
        a = jax.nn.sigmoid(a0 + _mm(lo, lora_ref[1]))
        g = _mm(jax.nn.sigmoid(lo), lora_ref[2])
        kk = k * k_k
        kk = kk * lax.rsqrt(headsum(kk * kk) + 1e-12)
        k2 = k * (1.0 + (a - 1.0) * k_a)
        lp = _seg_cumsum(lw, c)
        e_pos, e_neg = jnp.exp(lp), jnp.exp(-lp)
        seqs.append(dict(
            r=r, k2=k2, v=v, g=g, e_pos=e_pos,
            a_t=-kk * jnp.exp(lp - lw), b_t=kk * a * e_neg, k_t=k2 * e_neg, r_t=r * e_pos, v_b=v))

    def piece(s, name, ci, h):
        return seqs[s][name][ci * c:(ci + 1) * c, h * HEAD:(h + 1) * HEAD].astype(BF16)

    chains = [(s, h) for s in range(nb) for h in range(B_HEADS)]
    units = [(ci, s, h) for ci in range(n_chunk) for (s, h) in chains]

    a_u = {u: piece(u[1], "a_t", u[0], u[2]) for u in units}
    r_u = {u: piece(u[1], "r_t", u[0], u[2]) for u in units}
    v_u = {u: piece(u[1], "v_b", u[0], u[2]) for u in units}
    bk_u = {u: jnp.concatenate([piece(u[1], "b_t", u[0], u[2]), piece(u[1], "k_t", u[0], u[2])], axis=0) for u in units}
    zv_u = {u: jnp.concatenate([zeros_c, v_u[u]], axis=0) for u in units}
    g1 = {u: jnp.where(strict, _mm(a_u[u], bk_u[u], NT), 0.0) for u in units}
    g2 = {u: jnp.where(incl, _mm(r_u[u], bk_u[u], NT), 0.0).astype(BF16) for u in units}
    pw = {u: g1[u][:, :c] for u in units}
    inv = {u: eye + pw[u] for u in units}
    for _ in range(levels):
        pw = {u: _mm(pw[u], pw[u]) for u in units}
        step = {u: _mm(inv[u], pw[u]) for u in units}
        inv = {u: inv[u] + step[u] for u in units}
    g1 = {u: g1[u].astype(BF16) for u in units}

    state = {ch: s_ref[ch[0], ch[1]] for ch in chains}
    o_u = {}
    for ci in range(n_chunk):
        us = [(ci, s, h) for (s, h) in chains]
        sb = {u: state[u[1:]].astype(BF16) for u in us}
        rhs = {u: _mm(a_u[u], sb[u], NT) + _mm(g1[u], zv_u[u]) for u in us}
        uu = {u: _mm(inv[u], rhs[u]) for u in us}
        uv = {u: jnp.concatenate([uu[u].astype(BF16), v_u[u]], axis=0) for u in us}
        for u in us:
            o_u[u] = _mm(r_u[u], sb[u], NT) + _mm(g2[u], uv[u])
        for u in us:
            s, h = u[1:]
            p_end = seqs[s]["e_pos"][(ci + 1) * c - 1:(ci + 1) * c, h * HEAD:(h + 1) * HEAD]
            state[(s, h)] = (state[(s, h)] + _mm(uv[u], bk_u[u], TN)) * p_end
    for (s, h) in chains:
        s_ref[s, h] = state[(s, h)]

    for s in range(nb):
        q = seqs[s]
        o_rows = [jnp.concatenate([o_u[(ci, s, h)] for h in range(B_HEADS)], axis=1) for ci in range(n_chunk)]
        o = o_rows[0] if n_chunk == 1 else jnp.concatenate(o_rows, axis=0)
        inv_n = 1.0 / HEAD
        oc = o - headsum(o) * inv_n
        on = oc * lax.rsqrt(headsum(oc * oc) * inv_n + GN_EPS) * ln_g + ln_b
        bonus = headsum(q["r"] * q["k2"] * r_k) * q["v"]
        out_ref[s] = (on + bonus) * q["g"]


def _rwkv(pb, prev, s0, mu, prm, lora, hsum, nb, tt, c):
    n, t, _ = pb.shape
    return pl.pallas_call(
        functools.partial(_rwkv_body, nb=nb, tt=tt, c=c),
        grid=(n // nb, t // tt),
        in_specs=[pl.BlockSpec((nb, tt, B_COLS), lambda i, j: (i, j, 0)),
                  pl.BlockSpec((nb, 1, B_COLS), lambda i, j: (i, 0, 0)),
                  pl.BlockSpec((nb, B_HEADS, HEAD, HEAD), lambda i, j: (i, 0, 0, 0)),
                  _const_spec(mu.shape), _const_spec(prm.shape), _const_spec(lora.shape), _const_spec(hsum.shape)],
        out_specs=[pl.BlockSpec((nb, tt, B_WIDTH), lambda i, j: (i, j, 0)),
                   pl.BlockSpec((nb, B_HEADS, HEAD, HEAD), lambda i, j: (i, 0, 0, 0))],
        out_shape=[jax.ShapeDtypeStruct((n, t, B_WIDTH), F32), jax.ShapeDtypeStruct(s0.shape, F32)],
        scratch_shapes=[pltpu.VMEM((nb, 1, B_COLS), F32)],
        compiler_params=_cparams("parallel", "arbitrary"),
        name="rwkv7",
    )(pb, prev, s0, mu, prm, lora, hsum)


def _band_attn_body(q_ref, k_ref, v_ref, acc_ref, st_ref, *, nblk):
    n = pl.program_id(2)
    nk = BAND if nblk == 1 else 2 * BAND
    cur = pl.multiple_of(n * BAND, BAND)
    prev = pl.multiple_of(jnp.maximum(n - 1, 0) * BAND, BAND)

    def keys(ref, lanes):
        if nblk == 1:
            return ref[:, lanes]
        return jnp.concatenate([ref[pl.ds(prev, BAND), lanes], ref[pl.ds(cur, BAND), lanes]], axis=0)

    qi = lax.broadcasted_iota(jnp.int32, (BAND, nk), 0)
    kc = lax.broadcasted_iota(jnp.int32, (BAND, nk), 1)
    if nblk == 1:
        mask = kc <= qi
    else:
        first_key = jnp.where(n > 0, 0, BAND)
        mask = (kc >= qi) & (kc <= qi + BAND) & (kc >= first_key)
    lane = lax.broadcasted_iota(jnp.int32, (BAND, LANES), 1)
    stats = jnp.zeros((BAND, LANES), F32)
    group = 2
    for g0 in range(0, C_HEADS // 2, group):
        pairs = range(g0, g0 + group)
        scores = []
        for pair in pairs:
            lanes = slice(pair * LANES, (pair + 1) * LANES)
            q2, k2 = q_ref[:, lanes], keys(k_ref, lanes)
            for half in range(2):
                in_head = (lane >= HEAD) if half else (lane < HEAD)
                scores.append(_mm(jnp.where(in_head, q2, jnp.zeros_like(q2)), k2, NT))
        probs = []
        for i, s in enumerate(scores):
            head = 2 * g0 + i
            s = jnp.where(mask, s, NEG)
            m = jnp.max(s, axis=-1, keepdims=True)
            p = jnp.exp(s - m)
            probs.append(p.astype(BF16))
            stats = jnp.where(lane == head, m, stats)
            stats = jnp.where(lane == C_HEADS + head, jnp.sum(p, axis=-1, keepdims=True), stats)
        for i, pair in enumerate(pairs):
            lanes = slice(pair * LANES, (pair + 1) * LANES)
            v2 = keys(v_ref, lanes)
            acc_ref[:, lanes] = jnp.where(lane < HEAD, _mm(probs[2 * i], v2), _mm(probs[2 * i + 1], v2)).astype(acc_ref.dtype)
    st_ref[...] = stats


def _band_attn(q, k, v):
    nseq, dil, sub, _ = q.shape
    nblk = sub // BAND
    blk = lambda width: pl.BlockSpec((None, None, BAND, width), lambda b, r, n: (b, r, n, 0))
    cls = pl.BlockSpec((None, None, sub, D), lambda b, r, n: (b, r, 0, 0))
    return pl.pallas_call(
        functools.partial(_band_attn_body, nblk=nblk),
        grid=(nseq, dil, nblk),
        in_specs=[blk(D), cls, cls],
        out_specs=[blk(D), blk(LANES)],
        out_shape=[jax.ShapeDtypeStruct(q.shape, q.dtype), jax.ShapeDtypeStruct((nseq, dil, sub, LANES), F32)],
        compiler_params=_cparams("parallel", "parallel", "arbitrary"),
        name=f"band_attn_d{dil}",
    )(q, k, v)


def _mix_body(a1_ref, a4_ref, a16_ref, s1_ref, s4_ref, s16_ref, e_ref, w_ref, x_ref, gp_ref, gate_ref, o_ref,
              slab4, slab16, st4, st16, o_scr):
    tm = x_ref.shape[0]
    ngrp = D // LANES
    for dil, a_ref, s_ref, slab, st in ((4, a4_ref, s4_ref, slab4, st4), (16, a16_ref, s16_ref, slab16, st16)):
        for r in range(dil):
            rows = pl.ds(r, tm // dil, stride=dil)
            st[rows, :] = s_ref[r]
            for grp in range(ngrp):
                slab[grp, rows, :] = a_ref[r, :, grp * LANES:(grp + 1) * LANES].astype(F32)
    sts = [s1_ref[...], st4[...], st16[...]]
    lane = lax.broadcasted_iota(jnp.int32, sts[0].shape, 1)
    top = jnp.maximum(jnp.maximum(sts[0], sts[1]), sts[2])
    es = [jnp.exp(st - top) for st in sts]
    dens = [pltpu.roll(st, LANES - C_HEADS, 1) for st in sts]
    total = es[0] * dens[0] + es[1] * dens[1] + es[2] * dens[2]
    coefs = [_mm(jnp.where(lane < C_HEADS, e / total, 0.0), e_ref[...], passes=2) for e in es]
    for grp in range(ngrp):
        lanes = slice(grp * LANES, (grp + 1) * LANES)
        o = (coefs[0][:, lanes] * a1_ref[:, lanes].astype(F32) + coefs[1][:, lanes] * slab4[grp]
             + coefs[2][:, lanes] * slab16[grp])
        o_scr[:, lanes] = o.astype(BF16)
    y = _mm(o_scr[...], w_ref[...])
    o_ref[...] = x_ref[...] + gate_ref[...] * (_rms(y) * gp_ref[...])


def _mix_outproj(accs, sts, expand, w, x, gp, mods, layer, tm):
    r = x.shape[0]
    nseq = accs[1].shape[0]
    tps = (r // nseq) // tm
    cls_spec = lambda dil, width: pl.BlockSpec((None, dil, tm // dil, width), lambda i: (i // tps, 0, i % tps, 0))
    return pl.pallas_call(
        _mix_body,
        grid=(r // tm,),
        in_specs=[_row_spec(tm, D), cls_spec(4, D), cls_spec(16, D), _row_spec(tm, LANES), cls_spec(4, LANES),
                  cls_spec(16, LANES), _const_spec(expand.shape), _const_spec(w.shape), _row_spec(tm, D),
                  _const_spec((1, D)), _mod_spec(mods, layer, 2, tps)],
        out_specs=_row_spec(tm, D),
        out_shape=jax.ShapeDtypeStruct((r, D), F32),
        scratch_shapes=[pltpu.VMEM((D // LANES, tm, LANES), F32), pltpu.VMEM((D // LANES, tm, LANES), F32),
                        pltpu.VMEM((tm, LANES), F32), pltpu.VMEM((tm, LANES), F32), pltpu.VMEM((tm, D), BF16)],
        compiler_params=_cparams("parallel"),
        name="attn_mix_outproj",
    )(*accs, *sts, expand, w, x, gp, mods)


def _attn_sample_body(q_ref, kn_ref, vn_ref, k1_ref, v1_ref, k4_ref, v4_ref, k16_ref, v16_ref, o_ref, *, t_len):
    t = pl.program_id(1)
    q = q_ref[...][None]
    score = lambda ref: jnp.sum(ref[...] * q, axis=-1, keepdims=True)
    mi = lax.broadcasted_iota(jnp.int32, (BAND, C_HEADS, 1), 0)
    ti = lax.broadcasted_iota(jnp.int32, (t_len, C_HEADS, 1), 0)
    mult = jnp.where(ti == t, 3.0, 0.0) + jnp.where(ti < t, 1.0, 0.0) + jnp.where(ti == t - 4, 1.0, 0.0)
    first4 = jnp.where(t < 4, 0, 1)
    parts = [(jnp.where(mult > 0.0, score(kn_ref), NEG), vn_ref, mult),
             (jnp.where(mi >= t, score(k1_ref), NEG), v1_ref, None),
             (jnp.where(mi >= first4, score(k4_ref), NEG), v4_ref, None),
             (score(k16_ref), v16_ref, None)]
    top = None
    for s, _, _ in parts:
        m = jnp.max(s, axis=0, keepdims=True)
        top = m if top is None else jnp.maximum(top, m)
    num, den = None, None
    for s, v_ref, w in parts:
        p = jnp.exp(s - top)
        if w is not None:
            p = p * w
        d = jnp.sum(p, axis=0)
        nm = jnp.sum(p * v_ref[...], axis=0)
        num = nm if num is None else num + nm
        den = d if den is None else den + d
    o_ref[...] = num / den


def _attn_sample(q, kn, vn, cache_k, cache_v, layer, t_len):
    nl, n, w = cache_k.shape[:3]
    assert w == 16 * BAND and t_len == 8
    tok = pl.BlockSpec((None, None, C_HEADS, HEAD), lambda b, t: (b, t, 0, 0))
    new = pl.BlockSpec((None, t_len, C_HEADS, HEAD), lambda b, t: (b, 0, 0, 0))
    c1 = pl.BlockSpec((None, None, BAND, C_HEADS, HEAD), lambda b, t: (layer, b, w // BAND - 1, 0, 0))
    c4 = pl.BlockSpec((None, None, BAND, None, C_HEADS, HEAD), lambda b, t: (layer, b, w // 4 // BAND - 1, t % 4, 0, 0))
    c16 = pl.BlockSpec((None, None, BAND, None, C_HEADS, HEAD), lambda b, t: (layer, b, 0, t, 0, 0))
    v4 = lambda c: c.reshape(nl, n, w // 4, 4, C_HEADS, HEAD)
    v16 = lambda c: c.reshape(nl, n, w // 16, 16, C_HEADS, HEAD)
    return pl.pallas_call(
        functools.partial(_attn_sample_body, t_len=t_len),
        grid=(n, t_len),
        in_specs=[tok, new, new, c1, c1, c4, c4, c16, c16],
        out_specs=tok,
        out_shape=jax.ShapeDtypeStruct((n, t_len, C_HEADS, HEAD), F32),
        compiler_params=_cparams("parallel", "arbitrary"),
        name="attn_sample",
    )(q, kn, vn, cache_k, cache_v, v4(cache_k), v4(cache_v), v16(cache_k), v16(cache_v))


def kernel(x_prompt, x_sample, state_wkv, state_shift, cache_k, cache_v, c_prompt, c_sample, ada_w, ada_b, norm_mix_pre, norm_mix_post, norm_ffn_pre, norm_ffn_post, ffn_w1, ffn_w2, ev_w_in, ev_w_out, gm_ln_g, gm_ln_b, gm_ws, gm_bs, rw_mu, rw_w0, rw_w2, rw_a0, rw_a2, rw_g2, rw_kk, rw_ka, rw_rk, rw_lnx_g, rw_lnx_b, od_w_qkv, od_w_out):
    n_p, s_len, _ = x_prompt.shape
    n_s, t_len, _ = x_sample.shape
    r_p, r_s = n_p * s_len, n_s * t_len
    row = lambda t: t.reshape(1, -1).astype(F32)

    mods = _ada(jnp.concatenate([c_prompt, c_sample], axis=0).astype(F32), ada_w, ada_b)
    mods_p = mods[:, :n_p].reshape(DEPTH, n_p, 1, 6 * D)
    mods_s = jnp.repeat(mods[:, n_p:], t_len, axis=1).reshape(DEPTH, 1, r_s, 6 * D)

    xp = x_prompt.astype(F32).reshape(r_p, D)
    xs = x_sample.astype(F32).reshape(r_s, D)

    hsum = jnp.kron(jnp.eye(B_HEADS, dtype=F32), jnp.ones((HEAD, HEAD), F32)).astype(BF16)
    expand = jnp.concatenate([jnp.kron(jnp.eye(C_HEADS, dtype=F32), jnp.ones((1, HEAD), F32)),
                              jnp.zeros((LANES - C_HEADS, D), F32)], axis=0).astype(BF16)
    tabs_p = _rope_tables(jnp.arange(s_len))
    tabs_s = _rope_tables(jnp.tile(PAST_LEN + jnp.arange(t_len), n_s))
    zeros_prev = jnp.zeros((n_p, 1, B_COLS), F32)
    zeros_state = jnp.zeros((n_p, B_HEADS, HEAD, HEAD), F32)
    reps = CHUNK // t_len

    tm_p = 512
    outs = {k: [] for k in ("wkv_p", "shift_p", "k_p", "v_p", "wkv_s", "shift_s", "k_s", "v_s", "gv_s")}
    for l in range(DEPTH):
        g_pre, g_post = row(norm_mix_pre[l]), row(norm_mix_post[l])
        if l % 2 == 0:
            e = l // 2
            w_in = ev_w_in[e].astype(BF16)
            w_out = ev_w_out[e].astype(BF16)
            w_oa, w_ob = w_out[:A_WIDTH], w_out[A_WIDTH:]
            ln_g, ln_b = row(gm_ln_g[e]), row(gm_ln_b[e])
            ws = gm_ws[e].astype(F32)
            bias_p = jnp.repeat(gm_bs[e].T, A_WIDTH // A_GROUPS, axis=1)
            ws_s = jax.vmap(lambda m: jnp.kron(jnp.eye(reps, dtype=F32), m))(ws[:, :t_len, :t_len])
            bias_s = jnp.tile(bias_p[:t_len], (reps, 1))
            mu = row(rw_mu[e])
            zpad = jnp.zeros((7 * B_WIDTH,), F32)
            prm = jnp.concatenate([rw_w0[e], rw_a0[e], rw_kk[e], rw_ka[e], rw_rk[e].reshape(-1), rw_lnx_g[e],
                                   rw_lnx_b[e], zpad[:B_WIDTH]]).reshape(8, B_WIDTH).astype(F32)
            zl = lambda a, b: jnp.zeros((a, b), F32)
            lora = jnp.stack([
                jnp.concatenate([rw_w2[e], zl(96, B_WIDTH)], axis=0),
                jnp.concatenate([zl(32, B_WIDTH), rw_a2[e], zl(64, B_WIDTH)], axis=0),
                jnp.concatenate([zl(64, B_WIDTH), rw_g2[e]], axis=0)]).astype(BF16)

            pa_p, pb_p = _inproj(xp, g_pre, mods_p, l, tm_p, w_in)
            oa_p = _gmlp(pa_p, ln_g, ln_b, ws, bias_p, 256, False)
            pb_p3 = pb_p.reshape(n_p, s_len, B_COLS)
            ob_p, st_p = _rwkv(pb_p3, zeros_prev, zeros_state, mu, prm, lora, hsum, 2, 128, 64)
            xp = _outproj([oa_p, ob_p.reshape(r_p, B_WIDTH)], [w_oa, w_ob], xp, g_post, mods_p, l, tm_p)
            outs["wkv_p"].append(st_p)
            outs["shift_p"].append(pb_p3[:, -1])

            pa_s, pb_s = _inproj(xs, g_pre, mods_s, l, r_s, w_in)
            oa_s, gv = _gmlp(pa_s, ln_g, ln_b, ws_s, bias_s, r_s, True)
            pb_s3 = pb_s.reshape(n_s, t_len, B_COLS)
            ob_s, st_s = _rwkv(pb_s3, state_shift[e].astype(F32).reshape(n_s, 1, B_COLS), state_wkv[e].astype(F32),
                               mu, prm, lora, hsum, 8, t_len, t_len)
            xs = _outproj([oa_s, ob_s.reshape(r_s, B_WIDTH)], [w_oa, w_ob], xs, g_post, mods_s, l, r_s)
            outs["wkv_s"].append(st_s)
            outs["shift_s"].append(pb_s3[:, -1])
            outs["gv_s"].append(gv.reshape(n_s, t_len, A_WIDTH))
        else:
            o = l // 2
            w_qkv = od_w_qkv[o].astype(BF16)
            w_out = od_w_out[o].astype(BF16)
            q1, k1, v1, q4, k4, v4, q16, k16, v16, kf, vf = _qkv_classes(xp, g_pre, mods_p, l, 256, w_qkv, tabs_p, n_p)
            nat = lambda t: t.reshape(n_p, 1, s_len, D)
            res = [_band_attn(*qkv) for qkv in ((nat(q1), nat(k1), nat(v1)), (q4, k4, v4), (q16, k16, v16))]
            accs = [res[0][0].reshape(r_p, D), res[1][0], res[2][0]]
            sts = [res[0][1].reshape(r_p, LANES), res[1][1], res[2][1]]
            xp = _mix_outproj(accs, sts, expand, w_out, xp, g_post, mods_p, l, 256)
            outs["k_p"].append(kf.reshape(n_p, s_len, C_HEADS, HEAD))
            outs["v_p"].append(vf.reshape(n_p, s_len, C_HEADS, HEAD))

            qs, kfs, vfs = _qkv(xs, g_pre, mods_s, l, r_s, w_qkv, tabs_s)
            heads = lambda t: t.reshape(n_s, t_len, C_HEADS, HEAD)
            att = _attn_sample(heads(qs), heads(kfs), heads(vfs), cache_k.astype(F32), cache_v.astype(F32), o, t_len)
            xs = _outproj([att.reshape(r_s, D)], [w_out], xs, g_post, mods_s, l, r_s)
            outs["k_s"].append(heads(kfs))
            outs["v_s"].append(heads(vfs))
        w1, w2 = ffn_w1[l].astype(BF16), ffn_w2[l].astype(BF16)
        g_fpre, g_fpost = row(norm_ffn_pre[l]), row(norm_ffn_post[l])
        xp = _ffn(xp, g_fpre, g_fpost, mods_p, l, 1024, 512, w1, w2)
        xs = _ffn(xs, g_fpre, g_fpost, mods_s, l, r_s, 512, w1, w2)

    st = lambda name: jnp.stack(outs[name])
    return (xp.reshape(n_p, s_len, D), xs.reshape(n_s, t_len, D), st("wkv_p"), st("shift_p"), st("k_p"),
            st("v_p"), st("wkv_s"), st("shift_s"), st("k_s"), st("v_s"), st("gv_s"))
```

```python
import functools
import math

import jax
import jax.numpy as jnp
from jax import lax
from jax.experimental import pallas as pl
from jax.experimental.pallas import tpu as pltpu

F32 = jnp.float32
BF16 = jnp.bfloat16

D = 1024
DEPTH = 4
D_FF = 4 * D
A_WIDTH = 512
A_GROUPS = 4
CHUNK = 128
B_WIDTH = 512
B_HEADS = 8
HEAD = 64
LORA = 128
B_COLS = 3 * B_WIDTH + LORA
C_HEADS = 16
ROT_DIM = 16
ROPE_THETA = 500000.0
DILATIONS = (1, 4, 16)
BAND = 128
PAST_LEN = 8192
NORM_EPS = 1e-6
LN_EPS = 1e-5
GN_EPS = 64e-5
NEG = -1e30

LANES = 128
VMEM_LIMIT = 52 * 1024 * 1024


def _cparams(*sem):
    return pltpu.CompilerParams(dimension_semantics=sem, vmem_limit_bytes=VMEM_LIMIT)


def _split(x):
    hi = x.astype(BF16)
    return hi, (x - hi.astype(F32)).astype(BF16)


def _mm(a, b, dims=((1,), (0,)), passes=1):
    dn = (dims, ((), ()))
    dot = lambda p, q: lax.dot_general(p, q, dn, preferred_element_type=F32)
    if passes == 1:
        return dot(a.astype(BF16), b.astype(BF16))
    ah, al = _split(a)
    return dot(ah, b) + dot(al, b)


def _mm3(a, b):
    dot = lambda p, q: jnp.dot(p, q, preferred_element_type=F32)
    ah, al = _split(a)
    bh, bl = _split(b)
    return dot(ah, bh) + (dot(ah, bl) + dot(al, bh))


NT = ((1,), (1,))
TN = ((0,), (0,))


def _rms(x):
    return x * lax.rsqrt(jnp.mean(x * x, axis=-1, keepdims=True) + NORM_EPS)


def _mod_spec(mods, layer, chunk, tiles_per_group):
    return pl.BlockSpec((None, None, mods.shape[2], D), lambda i, *_: (layer, i // tiles_per_group, 0, chunk))


def _row_spec(tm, n):
    return pl.BlockSpec((tm, n), lambda i, *_: (i, 0))


def _const_spec(shape):
    nd = len(shape)
    return pl.BlockSpec(shape, lambda *_: (0,) * nd)


def _ada_body(c_ref, w_ref, b_ref, o_ref):
    c = c_ref[...]
    o_ref[...] = _mm(c * jax.nn.sigmoid(c), w_ref[...]) + b_ref[...]


def _ada(c_all, ada_w, ada_b):
    n = c_all.shape[0]
    tn = 1536
    return pl.pallas_call(
        _ada_body,
        grid=(DEPTH, 6 * D // tn),
        in_specs=[pl.BlockSpec((n, D), lambda l, j: (0, 0)),
                  pl.BlockSpec((None, D, tn), lambda l, j: (l, 0, j)),
                  pl.BlockSpec((None, 1, tn), lambda l, j: (l, 0, j))],
        out_specs=pl.BlockSpec((None, n, tn), lambda l, j: (l, 0, j)),
        out_shape=jax.ShapeDtypeStruct((DEPTH, n, 6 * D), F32),
        compiler_params=_cparams("parallel", "parallel"),
        name="ada",
    )(c_all, ada_w, ada_b.reshape(DEPTH, 1, 6 * D))


def _modnorm(x, g, sh, sc):
    return _rms(x) * g * (1.0 + sc) + sh


def _inproj_body(x_ref, g_ref, sh_ref, sc_ref, w_ref, pa_ref, pb_ref):
    h = _modnorm(x_ref[...], g_ref[...], sh_ref[...], sc_ref[...]).astype(BF16)
    pa_ref[...] = _mm(h, w_ref[:, :2 * A_WIDTH])
    pb_ref[...] = _mm(h, w_ref[:, 2 * A_WIDTH:])


def _inproj(x, g, mods, layer, tm, w):
    r = x.shape[0]
    tpg = (r // mods.shape[1]) // tm
    n = w.shape[1]
    return pl.pallas_call(
        _inproj_body,
        grid=(r // tm,),
        in_specs=[_row_spec(tm, D), _const_spec((1, D)), _mod_spec(mods, layer, 0, tpg),
                  _mod_spec(mods, layer, 1, tpg), _const_spec((D, n))],
        out_specs=[_row_spec(tm, 2 * A_WIDTH), _row_spec(tm, B_COLS)],
        out_shape=[jax.ShapeDtypeStruct((r, 2 * A_WIDTH), F32), jax.ShapeDtypeStruct((r, B_COLS), F32)],
        compiler_params=_cparams("parallel"),
        name="even_inproj",
    )(x, g, mods, mods, w)


def _rope(seg, cos, s_lo, s_hi):
    return seg * cos + pltpu.roll(seg, LANES - ROT_DIM // 2, 1) * s_lo + pltpu.roll(seg, ROT_DIM // 2, 1) * s_hi


def _qkv_groups(x_ref, g_ref, sh_ref, sc_ref, w_ref, cos_ref, slo_ref, shi_ref):
    h = _modnorm(x_ref[...], g_ref[...], sh_ref[...], sc_ref[...]).astype(BF16)
    cos, s_lo, s_hi = cos_ref[...], slo_ref[...], shi_ref[...]
    for grp in range(D // LANES):
        cols = lambda part: slice(part * D + grp * LANES, part * D + (grp + 1) * LANES)
        q = _rope(_mm(h, w_ref[:, cols(0)]), cos, s_lo, s_hi) * HEAD ** -0.5
        k = _rope(_mm(h, w_ref[:, cols(1)]), cos, s_lo, s_hi)
        yield slice(grp * LANES, (grp + 1) * LANES), q, k, _mm(h, w_ref[:, cols(2)])


def _qkv_body(*refs):
    q_ref, kf_ref, vf_ref = refs[8:]
    for lanes, q, k, v in _qkv_groups(*refs[:8]):
        q_ref[:, lanes] = q
        kf_ref[:, lanes] = k
        vf_ref[:, lanes] = v


def _qkv_classes_body(*refs):
    q1, k1, v1, q4, k4, v4, q16, k16, v16, kf_ref, vf_ref, slab = refs[8:]
    tm = q1.shape[0]
    ngrp = D // LANES
    for grp, (lanes, q, k, v) in enumerate(_qkv_groups(*refs[:8])):
        kf_ref[lanes, :] = k.T
        vf_ref[lanes, :] = v.T
        for which, (val, nat) in enumerate(((q, q1), (k, k1), (v, v1))):
            nat[:, lanes] = val.astype(BF16)
            slab[which * ngrp + grp] = val
    for which, (o4, o16) in enumerate(((q4, q16), (k4, k16), (v4, v16))):
        for grp in range(ngrp):
            lanes = slice(grp * LANES, (grp + 1) * LANES)
            for dil, out in ((4, o4), (16, o16)):
                for r in range(dil):
                    out[r, :, lanes] = slab[which * ngrp + grp, pl.ds(r, tm // dil, stride=dil), :].astype(BF16)


def _rope_tables(pos):
    half = ROT_DIM // 2
    inv = ROPE_THETA ** (-jnp.arange(half, dtype=F32) * (2.0 / ROT_DIM))
    ang = pos.astype(F32)[:, None] * inv[None, :]
    cos, sin = jnp.cos(ang), jnp.sin(ang)
    t = pos.shape[0]
    one, zero, z8 = jnp.ones((t, HEAD - ROT_DIM), F32), jnp.zeros((t, HEAD - ROT_DIM), F32), jnp.zeros((t, half), F32)
    tab = lambda *parts: jnp.tile(jnp.concatenate(parts, axis=1), (1, LANES // HEAD))
    return tab(cos, cos, one), tab(-sin, z8, zero), tab(z8, sin, zero)


def _qkv_in_specs(mods, layer, tm, tpg, tables_per_seq):
    tab_spec = pl.BlockSpec((tm, LANES), lambda i: (i % tables_per_seq, 0))
    return [_row_spec(tm, D), _const_spec((1, D)), _mod_spec(mods, layer, 0, tpg), _mod_spec(mods, layer, 1, tpg),
            _const_spec((D, 3 * D)), tab_spec, tab_spec, tab_spec]


def _qkv(x, g, mods, layer, tm, w, tables):
    r = x.shape[0]
    tpg = (r // mods.shape[1]) // tm
    f32 = jax.ShapeDtypeStruct((r, D), F32)
    return pl.pallas_call(
        _qkv_body,
        grid=(r // tm,),
        in_specs=_qkv_in_specs(mods, layer, tm, tpg, 1),
        out_specs=[_row_spec(tm, D)] * 3,
        out_shape=[f32, f32, f32],
        compiler_params=_cparams("parallel"),
        name="odd_qkv_decode",
    )(x, g, mods, mods, w, *tables)


def _qkv_classes(x, g, mods, layer, tm, w, tables, nseq):
    r = x.shape[0]
    s_len = r // nseq
    tps = s_len // tm
    cls_spec = lambda dil: pl.BlockSpec((None, dil, tm // dil, D), lambda i: (i // tps, 0, i % tps, 0))
    cls_shape = lambda dil: jax.ShapeDtypeStruct((nseq, dil, s_len // dil, D), BF16)
    nat = jax.ShapeDtypeStruct((r, D), BF16)
    f32 = jax.ShapeDtypeStruct((nseq, D, s_len), F32)
    f32_spec = pl.BlockSpec((None, D, tm), lambda i: (i // tps, 0, i % tps))
    return pl.pallas_call(
        _qkv_classes_body,
        grid=(r // tm,),
        in_specs=_qkv_in_specs(mods, layer, tm, tps, tps),
        out_specs=[_row_spec(tm, D)] * 3 + [cls_spec(4)] * 3 + [cls_spec(16)] * 3 + [f32_spec] * 2,
        out_shape=[nat] * 3 + [cls_shape(4)] * 3 + [cls_shape(16)] * 3 + [f32, f32],
        scratch_shapes=[pltpu.VMEM((3 * D // LANES, tm, LANES), F32)],
        compiler_params=_cparams("parallel"),
        name="odd_qkv",
    )(x, g, mods, mods, w, *tables)


def _outproj_body(*refs, n_in):
    ins, ws = refs[:n_in], refs[n_in:2 * n_in]
    x_ref, gp_ref, gate_ref, o_ref = refs[2 * n_in:]
    y = _mm(ins[0][...], ws[0][...])
    for a, w in zip(ins[1:], ws[1:]):
        y = y + _mm(a[...], w[...])
    o_ref[...] = x_ref[...] + gate_ref[...] * (_rms(y) * gp_ref[...])


def _outproj(ins, ws, x, gp, mods, layer, tm):
    r = x.shape[0]
    tpg = (r // mods.shape[1]) // tm
    return pl.pallas_call(
        functools.partial(_outproj_body, n_in=len(ins)),
        grid=(r // tm,),
        in_specs=[_row_spec(tm, a.shape[1]) for a in ins] + [_const_spec(w.shape) for w in ws]
        + [_row_spec(tm, D), _const_spec((1, D)), _mod_spec(mods, layer, 2, tpg)],
        out_specs=_row_spec(tm, D),
        out_shape=jax.ShapeDtypeStruct((r, D), F32),
        compiler_params=_cparams("parallel"),
        name="outproj",
    )(*ins, *ws, x, gp, mods)


def _ffn_body(x_ref, g_ref, sh_ref, sc_ref, gate_ref, gp_ref, w1_ref, w2_ref, o_ref, h_scr, acc_scr):
    j = pl.program_id(1)

    @pl.when(j == 0)
    def _():
        h_scr[...] = _modnorm(x_ref[...], g_ref[...], sh_ref[...], sc_ref[...]).astype(BF16)
        acc_scr[...] = jnp.zeros_like(acc_scr)

    t = jnp.maximum(_mm(h_scr[...], w1_ref[...]), 0.0)
    acc_scr[...] += _mm(t * t, w2_ref[...])

    @pl.when(j == pl.num_programs(1) - 1)
    def _():
        o_ref[...] = x_ref[...] + gate_ref[...] * (_rms(acc_scr[...]) * gp_ref[...])


def _ffn(x, g, gp, mods, layer, tm, tf, w1, w2):
    r = x.shape[0]
    tpg = (r // mods.shape[1]) // tm
    return pl.pallas_call(
        _ffn_body,
        grid=(r // tm, D_FF // tf),
        in_specs=[_row_spec(tm, D), _const_spec((1, D)), _mod_spec(mods, layer, 3, tpg),
                  _mod_spec(mods, layer, 4, tpg), _mod_spec(mods, layer, 5, tpg), _const_spec((1, D)),
                  pl.BlockSpec((D, tf), lambda i, j: (0, j)), pl.BlockSpec((tf, D), lambda i, j: (j, 0))],
        out_specs=_row_spec(tm, D),
        out_shape=jax.ShapeDtypeStruct((r, D), F32),
        scratch_shapes=[pltpu.VMEM((tm, D), BF16), pltpu.VMEM((tm, D), F32)],
        compiler_params=_cparams("parallel", "arbitrary"),
        name="ffn",
    )(x, g, mods, mods, mods, gp, w1, w2)


def _gelu(x):
    return 0.5 * x * (1.0 + lax.erf(x * (2.0 ** -0.5)))


def _gmlp_body(pa_ref, lng_ref, lnb_ref, ws_ref, bias_ref, out_ref, *v_refs):
    pa = pa_ref[...]
    u = _gelu(pa[:, :A_WIDTH])
    v = _gelu(pa[:, A_WIDTH:])
    mu = jnp.mean(v, axis=-1, keepdims=True)
    vc = v - mu
    var = jnp.mean(vc * vc, axis=-1, keepdims=True)
    vn = vc * lax.rsqrt(var + LN_EPS) * lng_ref[...] + lnb_ref[...]
    if v_refs:
        v_refs[0][...] = vn
    ii = lax.broadcasted_iota(jnp.int32, (CHUNK, CHUNK), 0)
    jj = lax.broadcasted_iota(jnp.int32, (CHUNK, CHUNK), 1)
    cg = A_WIDTH // A_GROUPS
    for grp in range(A_GROUPS):
        lanes = slice(grp * cg, (grp + 1) * cg)
        wc = jnp.where(jj <= ii, ws_ref[grp], 0.0).astype(BF16)
        for ci in range(pa.shape[0] // CHUNK):
            rows = slice(ci * CHUNK, (ci + 1) * CHUNK)
            z = _mm(wc, vn[rows, lanes]) + bias_ref[:, lanes]
            out_ref[rows, lanes] = u[rows, lanes] * z


def _gmlp(pa, ln_g, ln_b, ws, bias, tm, emit_v):
    r = pa.shape[0]
    out = jax.ShapeDtypeStruct((r, A_WIDTH), F32)
    spec = _row_spec(tm, A_WIDTH)
    return pl.pallas_call(
        _gmlp_body,
        grid=(r // tm,),
        in_specs=[_row_spec(tm, 2 * A_WIDTH), _const_spec((1, A_WIDTH)), _const_spec((1, A_WIDTH)),
                  _const_spec(ws.shape), _const_spec(bias.shape)],
        out_specs=[spec, spec] if emit_v else spec,
        out_shape=[out, out] if emit_v else out,
        compiler_params=_cparams("parallel"),
        name="gmlp",
    )(pa, ln_g, ln_b, ws, bias)


def _seg_cumsum(x, seg):
    row = lax.broadcasted_iota(jnp.int32, (x.shape[0], 1), 0) % seg
    s = 1
    while s < seg:
        x = x + jnp.where(row >= s, pltpu.roll(x, s, 0), 0.0)
        s *= 2
    return x


def _rwkv_body(pb_ref, prev_ref, s0_ref, mu_ref, prm_ref, lora_ref, hsum_ref, out_ref, s_ref, last_scr,
               *, nb, tt, c):
    @pl.when(pl.program_id(1) == 0)
    def _():
        s_ref[...] = s0_ref[...]
        last_scr[...] = prev_ref[...]

    mu = mu_ref[...]
    w0, a0, k_k, k_a, r_k, ln_g, ln_b = (prm_ref[i:i + 1, :] for i in range(7))
    hsum = hsum_ref[...]
    headsum = lambda t: _mm(t, hsum, passes=2)
    row = lax.broadcasted_iota(jnp.int32, (tt, 1), 0)
    ii = lax.broadcasted_iota(jnp.int32, (c, 2 * c), 0)
    jj = lax.broadcasted_iota(jnp.int32, (c, 2 * c), 1) % c
    strict, incl = jj < ii, jj <= ii
    eye = jnp.where(lax.broadcasted_iota(jnp.int32, (c, c), 0) == lax.broadcasted_iota(jnp.int32, (c, c), 1), 1.0, 0.0).astype(F32)
    zeros_c = jnp.zeros((c, HEAD), BF16)
    n_chunk = tt // c
    levels = int(math.log2(c)) - 1

    seqs = []
    for s in range(nb):
        x = pb_ref[s]
        shifted = jnp.where(row == 0, last_scr[s], pltpu.roll(x, 1, 0))
        last_scr[s] = x[tt - 1:tt, :]
        xm = x + mu * (shifted - x)
        r, k, v = xm[:, :B_WIDTH], xm[:, B_WIDTH:2 * B_WIDTH], xm[:, 2 * B_WIDTH:3 * B_WIDTH]
        lo = xm[:, 3 * B_WIDTH:]
        z = -(w0 + _mm(jnp.tanh(lo), lora_ref[0]))
        softplus = jnp.maximum(z, 0.0) + jnp.log1p(jnp.exp(-jnp.abs(z)))
        lw = -jnp.exp(-softplus - 0.5)
        a = jax.nn.sigmoid(a0 + _mm(lo, lora_ref[1]))
        g = _mm(jax.nn.sigmoid(lo), lora_ref[2])
        kk = k * k_k
        kk = kk * lax.rsqrt(headsum(kk * kk) + 1e-12)
        k2 = k * (1.0 + (a - 1.0) * k_a)
        lp = _seg_cumsum(lw, c)
        e_pos, e_neg = jnp.exp(lp), jnp.exp(-lp)
        seqs.append(dict(
            r=r, k2=k2, v=v, g=g, e_pos=e_pos,
            a_t=-kk * jnp.exp(lp - lw), b_t=kk * a * e_neg, k_t=k2 * e_neg, r_t=r * e_pos, v_b=v))

    def piece(s, name, ci, h):
        return seqs[s][name][ci * c:(ci + 1) * c, h * HEAD:(h + 1) * HEAD].astype(BF16)

    chains = [(s, h) for s in range(nb) for h in range(B_HEADS)]
    units = [(ci, s, h) for ci in range(n_chunk) for (s, h) in chains]

    a_u = {u: piece(u[1], "a_t", u[0], u[2]) for u in units}
    r_u = {u: piece(u[1], "r_t", u[0], u[2]) for u in units}
    v_u = {u: piece(u[1], "v_b", u[0], u[2]) for u in units}
    bk_u = {u: jnp.concatenate([piece(u[1], "b_t", u[0], u[2]), piece(u[1], "k_t", u[0], u[2])], axis=0) for u in units}
    zv_u = {u: jnp.concatenate([zeros_c, v_u[u]], axis=0) for u in units}
    g1 = {u: jnp.where(strict, _mm(a_u[u], bk_u[u], NT), 0.0) for u in units}
    g2 = {u: jnp.where(incl, _mm(r_u[u], bk_u[u], NT), 0.0).astype(BF16) for u in units}
    pw = {u: g1[u][:, :c] for u in units}
    inv = {u: eye + pw[u] for u in units}
    for _ in range(levels):
        pw = {u: _mm(pw[u], pw[u]) for u in units}
        step = {u: _mm(inv[u], pw[u]) for u in units}
        inv = {u: inv[u] + step[u] for u in units}
    res = {u: eye - inv[u] + _mm3(g1[u][:, :c], inv[u]) for u in units}
    corr = {u: _mm(inv[u], res[u]) for u in units}
    inv = {u: inv[u] + corr[u] for u in units}
    g1 = {u: g1[u].astype(BF16) for u in units}

    state = {ch: s_ref[ch[0], ch[1]] for ch in chains}
    o_u = {}
    for ci in range(n_chunk):
        us = [(ci, s, h) for (s, h) in chains]
        sb = {u: state[u[1:]].astype(BF16) for u in us}
        rhs = {u: _mm(a_u[u], sb[u], NT) + _mm(g1[u], zv_u[u]) for u in us}
        uu = {u: _mm(inv[u], rhs[u]) for u in us}
        uv = {u: jnp.concatenate([uu[u].astype(BF16), v_u[u]], axis=0) for u in us}
        for u in us:
            o_u[u] = _mm(r_u[u], sb[u], NT) + _mm(g2[u], uv[u])
        for u in us:
            s, h = u[1:]
            p_end = seqs[s]["e_pos"][(ci + 1) * c - 1:(ci + 1) * c, h * HEAD:(h + 1) * HEAD]
            state[(s, h)] = (state[(s, h)] + _mm(uv[u], bk_u[u], TN)) * p_end
    for (s, h) in chains:
        s_ref[s, h] = state[(s, h)]

    for s in range(nb):
        q = seqs[s]
        o_rows = [jnp.concatenate([o_u[(ci, s, h)] for h in range(B_HEADS)], axis=1) for ci in range(n_chunk)]
        o = o_rows[0] if n_chunk == 1 else jnp.concatenate(o_rows, axis=0)
        inv_n = 1.0 / HEAD
        oc = o - headsum(o) * inv_n
        on = oc * lax.rsqrt(headsum(oc * oc) * inv_n + GN_EPS) * ln_g + ln_b
        bonus = headsum(q["r"] * q["k2"] * r_k) * q["v"]
        out_ref[s] = (on + bonus) * q["g"]


def _rwkv(pb, prev, s0, mu, prm, lora, hsum, nb, tt, c):
    n, t, _ = pb.shape
    return pl.pallas_call(
        functools.partial(_rwkv_body, nb=nb, tt=tt, c=c),
        grid=(n // nb, t // tt),
        in_specs=[pl.BlockSpec((nb, tt, B_COLS), lambda i, j: (i, j, 0)),
                  pl.BlockSpec((nb, 1, B_COLS), lambda i, j: (i, 0, 0)),
                  pl.BlockSpec((nb, B_HEADS, HEAD, HEAD), lambda i, j: (i, 0, 0, 0)),
                  _const_spec(mu.shape), _const_spec(prm.shape), _const_spec(lora.shape), _const_spec(hsum.shape)],
        out_specs=[pl.BlockSpec((nb, tt, B_WIDTH), lambda i, j: (i, j, 0)),
                   pl.BlockSpec((nb, B_HEADS, HEAD, HEAD), lambda i, j: (i, 0, 0, 0))],
        out_shape=[jax.ShapeDtypeStruct((n, t, B_WIDTH), F32), jax.ShapeDtypeStruct(s0.shape, F32)],
        scratch_shapes=[pltpu.VMEM((nb, 1, B_COLS), F32)],
        compiler_params=_cparams("parallel", "arbitrary"),
        name="rwkv7",
    )(pb, prev, s0, mu, prm, lora, hsum)


def _band_attn_body(q_ref, k_ref, v_ref, acc_ref, st_ref, *, nblk):
    n = pl.program_id(2)
    nk = BAND if nblk == 1 else 2 * BAND
    cur = pl.multiple_of(n * BAND, BAND)
    prev = pl.multiple_of(jnp.maximum(n - 1, 0) * BAND, BAND)

    def keys(ref, lanes):
        if nblk == 1:
            return ref[:, lanes]
        return jnp.concatenate([ref[pl.ds(prev, BAND), lanes], ref[pl.ds(cur, BAND), lanes]], axis=0)

    qi = lax.broadcasted_iota(jnp.int32, (BAND, nk), 0)
    kc = lax.broadcasted_iota(jnp.int32, (BAND, nk), 1)
    if nblk == 1:
        mask = kc <= qi
    else:
        first_key = jnp.where(n > 0, 0, BAND)
        mask = (kc >= qi) & (kc <= qi + BAND) & (kc >= first_key)
    lane = lax.broadcasted_iota(jnp.int32, (BAND, LANES), 1)
    stats = jnp.zeros((BAND, LANES), F32)
    group = 2
    for g0 in range(0, C_HEADS // 2, group):
        pairs = range(g0, g0 + group)
        scores = []
        for pair in pairs:
            lanes = slice(pair * LANES, (pair + 1) * LANES)
            q2, k2 = q_ref[:, lanes], keys(k_ref, lanes)
            for half in range(2):
                in_head = (lane >= HEAD) if half else (lane < HEAD)
                scores.append(_mm(jnp.where(in_head, q2, jnp.zeros_like(q2)), k2, NT))
        probs = []
        for i, s in enumerate(scores):
            head = 2 * g0 + i
            s = jnp.where(mask, s, NEG)
            m = jnp.max(s, axis=-1, keepdims=True)
            p = jnp.exp(s - m)
            probs.append(p.astype(BF16))
            stats = jnp.where(lane == head, m, stats)
            stats = jnp.where(lane == C_HEADS + head, jnp.sum(p, axis=-1, keepdims=True), stats)
        for i, pair in enumerate(pairs):
            lanes = slice(pair * LANES, (pair + 1) * LANES)
            v2 = keys(v_ref, lanes)
            acc_ref[:, lanes] = jnp.where(lane < HEAD, _mm(probs[2 * i], v2), _mm(probs[2 * i + 1], v2)).astype(acc_ref.dtype)
    st_ref[...] = stats


def _band_attn(q, k, v):
    nseq, dil, sub, _ = q.shape
    nblk = sub // BAND
    blk = lambda width: pl.BlockSpec((None, None, BAND, width), lambda b, r, n: (b, r, n, 0))
    cls = pl.BlockSpec((None, None, sub, D), lambda b, r, n: (b, r, 0, 0))
    return pl.pallas_call(
        functools.partial(_band_attn_body, nblk=nblk),
        grid=(nseq, dil, nblk),
        in_specs=[blk(D), cls, cls],
        out_specs=[blk(D), blk(LANES)],
        out_shape=[jax.ShapeDtypeStruct(q.shape, q.dtype), jax.ShapeDtypeStruct((nseq, dil, sub, LANES), F32)],
        compiler_params=_cparams("parallel", "parallel", "arbitrary"),
        name=f"band_attn_d{dil}",
    )(q, k, v)


def _mix_body(a1_ref, a4_ref, a16_ref, s1_ref, s4_ref, s16_ref, e_ref, w_ref, x_ref, gp_ref, gate_ref, o_ref,
              slab4, slab16, st4, st16, o_scr):
    tm = x_ref.shape[0]
    ngrp = D // LANES
    for dil, a_ref, s_ref, slab, st in ((4, a4_ref, s4_ref, slab4, st4), (16, a16_ref, s16_ref, slab16, st16)):
        for r in range(dil):
            rows = pl.ds(r, tm // dil, stride=dil)
            st[rows, :] = s_ref[r]
            for grp in range(ngrp):
                slab[grp, rows, :] = a_ref[r, :, grp * LANES:(grp + 1) * LANES].astype(F32)
    sts = [s1_ref[...], st4[...], st16[...]]
    lane = lax.broadcasted_iota(jnp.int32, sts[0].shape, 1)
    top = jnp.maximum(jnp.maximum(sts[0], sts[1]), sts[2])
    es = [jnp.exp(st - top) for st in sts]
    dens = [pltpu.roll(st, LANES - C_HEADS, 1) for st in sts]
    total = es[0] * dens[0] + es[1] * dens[1] + es[2] * dens[2]
    coefs = [_mm(jnp.where(lane < C_HEADS, e / total, 0.0), e_ref[...], passes=2) for e in es]
    for grp in range(ngrp):
        lanes = slice(grp * LANES, (grp + 1) * LANES)
        o = (coefs[0][:, lanes] * a1_ref[:, lanes].astype(F32) + coefs[1][:, lanes] * slab4[grp]
             + coefs[2][:, lanes] * slab16[grp])
        o_scr[:, lanes] = o.astype(BF16)
    y = _mm(o_scr[...], w_ref[...])
    o_ref[...] = x_ref[...] + gate_ref[...] * (_rms(y) * gp_ref[...])


def _mix_outproj(accs, sts, expand, w, x, gp, mods, layer, tm):
    r = x.shape[0]
    nseq = accs[1].shape[0]
    tps = (r // nseq) // tm
    cls_spec = lambda dil, width: pl.BlockSpec((None, dil, tm // dil, width), lambda i: (i // tps, 0, i % tps, 0))
    return pl.pallas_call(
        _mix_body,
        grid=(r // tm,),
        in_specs=[_row_spec(tm, D), cls_spec(4, D), cls_spec(16, D), _row_spec(tm, LANES), cls_spec(4, LANES),
                  cls_spec(16, LANES), _const_spec(expand.shape), _const_spec(w.shape), _row_spec(tm, D),
                  _const_spec((1, D)), _mod_spec(mods, layer, 2, tps)],
        out_specs=_row_spec(tm, D),
        out_shape=jax.ShapeDtypeStruct((r, D), F32),
        scratch_shapes=[pltpu.VMEM((D // LANES, tm, LANES), F32), pltpu.VMEM((D // LANES, tm, LANES), F32),
                        pltpu.VMEM((tm, LANES), F32), pltpu.VMEM((tm, LANES), F32), pltpu.VMEM((tm, D), BF16)],
        compiler_params=_cparams("parallel"),
        name="attn_mix_outproj",
    )(*accs, *sts, expand, w, x, gp, mods)


def _attn_sample_body(q_ref, kn_ref, vn_ref, kc_ref, vc_ref, o_ref, *, t_len, window):
    tq = lax.broadcasted_iota(jnp.int32, (t_len, window), 0)
    back = window + tq - lax.broadcasted_iota(jnp.int32, (t_len, window), 1)
    cnt = (jnp.where(back <= BAND, 1.0, 0.0) + jnp.where((back <= 4 * BAND) & ((back & 3) == 0), 1.0, 0.0)
           + jnp.where((back & 15) == 0, 1.0, 0.0))
    tn = lax.broadcasted_iota(jnp.int32, (t_len, t_len), 0)
    dn = tn - lax.broadcasted_iota(jnp.int32, (t_len, t_len), 1)
    mult = jnp.where(dn == 0, 3.0, 0.0) + jnp.where(dn > 0, 1.0, 0.0) + jnp.where(dn == 4, 1.0, 0.0)
    lane = lax.broadcasted_iota(jnp.int32, (t_len, LANES), 1)
    for pair in range(q_ref.shape[1] // LANES):
        lanes = slice(pair * LANES, (pair + 1) * LANES)
        q2, kn, vn = q_ref[:, lanes], kn_ref[:, lanes].astype(BF16), vn_ref[:, lanes].astype(BF16)
        kc, vc = kc_ref[lanes, :].astype(BF16), vc_ref[lanes, :].astype(BF16)
        outs = []
        for half in range(2):
            qh = jnp.where((lane >= HEAD) if half else (lane < HEAD), q2, 0.0).astype(BF16)
            s_c = jnp.where(cnt > 0.0, _mm(qh, kc), NEG)
            s_n = jnp.where(mult > 0.0, _mm(qh, kn, NT), NEG)
            top = jnp.maximum(jnp.max(s_c, axis=-1, keepdims=True), jnp.max(s_n, axis=-1, keepdims=True))
            p_c = jnp.exp(s_c - top) * cnt
            p_n = jnp.exp(s_n - top) * mult
            den = jnp.sum(p_c, axis=-1, keepdims=True) + jnp.sum(p_n, axis=-1, keepdims=True)
            outs.append((_mm(p_c, vc, NT) + _mm(p_n, vn)) / den)
        o_ref[:, lanes] = jnp.where(lane < HEAD, outs[0], outs[1])


def _attn_sample(q, kn, vn, cache_k, cache_v, layer, nseq):
    window = cache_k.shape[-1]
    t_len = q.shape[0] // nseq
    assert window == 16 * BAND and t_len == 8
    width = 2 * LANES
    new = pl.BlockSpec((t_len, width), lambda b, g: (b, g))
    cached = pl.BlockSpec((None, None, width, window), lambda b, g: (layer, b, g, 0))
    return pl.pallas_call(
        functools.partial(_attn_sample_body, t_len=t_len, window=window),
        grid=(nseq, D // width),
        in_specs=[new, new, new, cached, cached],
        out_specs=new,
        out_shape=jax.ShapeDtypeStruct(q.shape, F32),
        compiler_params=_cparams("parallel", "parallel"),
        name="attn_sample",
    )(q, kn, vn, cache_k, cache_v)


def kernel(x_prompt, x_sample, state_wkv, state_shift, cache_k, cache_v, c_prompt, c_sample, ada_w, ada_b, norm_mix_pre, norm_mix_post, norm_ffn_pre, norm_ffn_post, ffn_w1, ffn_w2, ev_w_in, ev_w_out, gm_ln_g, gm_ln_b, gm_ws, gm_bs, rw_mu, rw_w0, rw_w2, rw_a0, rw_a2, rw_g2, rw_kk, rw_ka, rw_rk, rw_lnx_g, rw_lnx_b, od_w_qkv, od_w_out):
    n_p, s_len, _ = x_prompt.shape
    n_s, t_len, _ = x_sample.shape
    r_p, r_s = n_p * s_len, n_s * t_len
    row = lambda t: t.reshape(1, -1).astype(F32)

    mods = _ada(jnp.concatenate([c_prompt, c_sample], axis=0).astype(F32), ada_w, ada_b)
    mods_p = mods[:, :n_p].reshape(DEPTH, n_p, 1, 6 * D)
    mods_s = jnp.repeat(mods[:, n_p:], t_len, axis=1).reshape(DEPTH, 1, r_s, 6 * D)

    xp = x_prompt.astype(F32).reshape(r_p, D)
    xs = x_sample.astype(F32).reshape(r_s, D)

    hsum = jnp.kron(jnp.eye(B_HEADS, dtype=F32), jnp.ones((HEAD, HEAD), F32)).astype(BF16)
    expand = jnp.concatenate([jnp.kron(jnp.eye(C_HEADS, dtype=F32), jnp.ones((1, HEAD), F32)),
                              jnp.zeros((LANES - C_HEADS, D), F32)], axis=0).astype(BF16)
    tabs_p = _rope_tables(jnp.arange(s_len))
    tabs_s = _rope_tables(jnp.tile(PAST_LEN + jnp.arange(t_len), n_s))
    zeros_prev = jnp.zeros((n_p, 1, B_COLS), F32)
    zeros_state = jnp.zeros((n_p, B_HEADS, HEAD, HEAD), F32)
    reps = CHUNK // t_len
    feat_major = lambda c: jnp.transpose(c.astype(F32), (0, 1, 3, 4, 2)).reshape(c.shape[0], c.shape[1], D, c.shape[2])
    cache_kt, cache_vt = feat_major(cache_k), feat_major(cache_v)
    tok_major = lambda t: jnp.transpose(t.reshape(n_p, C_HEADS, HEAD, s_len), (0, 3, 1, 2))

    tm_p = 512
    outs = {k: [] for k in ("wkv_p", "shift_p", "k_p", "v_p", "wkv_s", "shift_s", "k_s", "v_s", "gv_s")}
    for l in range(DEPTH):
        g_pre, g_post = row(norm_mix_pre[l]), row(norm_mix_post[l])
        if l % 2 == 0:
            e = l // 2
            w_in = ev_w_in[e].astype(BF16)
            w_out = ev_w_out[e].astype(BF16)
            w_oa, w_ob = w_out[:A_WIDTH], w_out[A_WIDTH:]
            ln_g, ln_b = row(gm_ln_g[e]), row(gm_ln_b[e])
            ws = gm_ws[e].astype(F32)
            bias_p = jnp.repeat(gm_bs[e].T, A_WIDTH // A_GROUPS, axis=1)
            ws_s = jax.vmap(lambda m: jnp.kron(jnp.eye(reps, dtype=F32), m))(ws[:, :t_len, :t_len])
            bias_s = jnp.tile(bias_p[:t_len], (reps, 1))
            mu = row(rw_mu[e])
            zpad = jnp.zeros((7 * B_WIDTH,), F32)
            prm = jnp.concatenate([rw_w0[e], rw_a0[e], rw_kk[e], rw_ka[e], rw_rk[e].reshape(-1), rw_lnx_g[e],
                                   rw_lnx_b[e], zpad[:B_WIDTH]]).reshape(8, B_WIDTH).astype(F32)
            zl = lambda a, b: jnp.zeros((a, b), F32)
            lora = jnp.stack([
                jnp.concatenate([rw_w2[e], zl(96, B_WIDTH)], axis=0),
                jnp.concatenate([zl(32, B_WIDTH), rw_a2[e], zl(64, B_WIDTH)], axis=0),
                jnp.concatenate([zl(64, B_WIDTH), rw_g2[e]], axis=0)]).astype(BF16)

            pa_p, pb_p = _inproj(xp, g_pre, mods_p, l, tm_p, w_in)
            oa_p = _gmlp(pa_p, ln_g, ln_b, ws, bias_p, 256, False)
            pb_p3 = pb_p.reshape(n_p, s_len, B_COLS)
            ob_p, st_p = _rwkv(pb_p3, zeros_prev, zeros_state, mu, prm, lora, hsum, 2, 128, 64)
            xp = _outproj([oa_p, ob_p.reshape(r_p, B_WIDTH)], [w_oa, w_ob], xp, g_post, mods_p, l, tm_p)
            outs["wkv_p"].append(st_p)
            outs["shift_p"].append(pb_p3[:, -1])

            pa_s, pb_s = _inproj(xs, g_pre, mods_s, l, r_s, w_in)
            oa_s, gv = _gmlp(pa_s, ln_g, ln_b, ws_s, bias_s, r_s, True)
            pb_s3 = pb_s.reshape(n_s, t_len, B_COLS)
            ob_s, st_s = _rwkv(pb_s3, state_shift[e].astype(F32).reshape(n_s, 1, B_COLS), state_wkv[e].astype(F32),
                               mu, prm, lora, hsum, 8, t_len, t_len)
            xs = _outproj([oa_s, ob_s.reshape(r_s, B_WIDTH)], [w_oa, w_ob], xs, g_post, mods_s, l, r_s)
            outs["wkv_s"].append(st_s)
            outs["shift_s"].append(pb_s3[:, -1])
            outs["gv_s"].append(gv.reshape(n_s, t_len, A_WIDTH))
        else:
            o = l // 2
            w_qkv = od_w_qkv[o].astype(BF16)
            w_out = od_w_out[o].astype(BF16)
            q1, k1, v1, q4, k4, v4, q16, k16, v16, kf, vf = _qkv_classes(xp, g_pre, mods_p, l, 256, w_qkv, tabs_p, n_p)
            nat = lambda t: t.reshape(n_p, 1, s_len, D)
            res = [_band_attn(*qkv) for qkv in ((nat(q1), nat(k1), nat(v1)), (q4, k4, v4), (q16, k16, v16))]
            accs = [res[0][0].reshape(r_p, D), res[1][0], res[2][0]]
            sts = [res[0][1].reshape(r_p, LANES), res[1][1], res[2][1]]
            xp = _mix_outproj(accs, sts, expand, w_out, xp, g_post, mods_p, l, 256)
            outs["k_p"].append(tok_major(kf))
            outs["v_p"].append(tok_major(vf))

            qs, kfs, vfs = _qkv(xs, g_pre, mods_s, l, r_s, w_qkv, tabs_s)
            heads = lambda t: t.reshape(n_s, t_len, C_HEADS, HEAD)
            att = _attn_sample(qs, kfs, vfs, cache_kt, cache_vt, o, n_s)
            xs = _outproj([att], [w_out], xs, g_post, mods_s, l, r_s)
            outs["k_s"].append(heads(kfs))
            outs["v_s"].append(heads(vfs))
        w1, w2 = ffn_w1[l].astype(BF16), ffn_w2[l].astype(BF16)
        g_fpre, g_fpost = row(norm_ffn_pre[l]), row(norm_ffn_post[l])
        xp = _ffn(xp, g_fpre, g_fpost, mods_p, l, 1024, 512, w1, w2)
        xs = _ffn(xs, g_fpre, g_fpost, mods_s, l, r_s, 512, w1, w2)

    st = lambda name: jnp.stack(outs[name])
    return (xp.reshape(n_p, s_len, D), xs.reshape(n_s, t_len, D), st("wkv_p"), st("shift_p"), st("k_p"),
            st("v_p"), st("wkv_s"), st("shift_s"), st("k_s"), st("v_s"), st("gv_s"))
```

```python
import functools
import math

import jax
import jax.numpy as jnp
from jax import lax
from jax.experimental import pallas as pl
from jax.experimental.pallas import tpu as pltpu

F32 = jnp.float32
BF16 = jnp.bfloat16

D = 1024
DEPTH = 4
D_FF = 4 * D
A_WIDTH = 512
A_GROUPS = 4
CHUNK = 128
B_WIDTH = 512
B_HEADS = 8
HEAD = 64
LORA = 128
B_COLS = 3 * B_WIDTH + LORA
C_HEADS = 16
ROT_DIM = 16
ROPE_THETA = 500000.0
DILATIONS = (1, 4, 16)
BAND = 128
PAST_LEN = 8192
NORM_EPS = 1e-6
LN_EPS = 1e-5
GN_EPS = 64e-5
NEG = -1e30

LANES = 128
DEN_EVEN, DEN_ODD = 64, 16
VMEM_LIMIT = 52 * 1024 * 1024


def _cparams(*sem):
    return pltpu.CompilerParams(dimension_semantics=sem, vmem_limit_bytes=VMEM_LIMIT)


def _split(x):
    hi = x.astype(BF16)
    return hi, (x - hi.astype(F32)).astype(BF16)


def _mm(a, b, dims=((1,), (0,)), passes=1):
    dn = (dims, ((), ()))
    dot = lambda p, q: lax.dot_general(p, q, dn, preferred_element_type=F32)
    if passes == 1:
        return dot(a.astype(BF16), b.astype(BF16))
    ah, al = _split(a)
    return dot(ah, b) + dot(al, b)


def _mm3(a, b):
    dot = lambda p, q: jnp.dot(p, q, preferred_element_type=F32)
    ah, al = _split(a)
    bh, bl = _split(b)
    return dot(ah, bh) + (dot(ah, bl) + dot(al, bh))


NT = ((1,), (1,))
TN = ((0,), (0,))


def _rms(x):
    return x * lax.rsqrt(jnp.mean(x * x, axis=-1, keepdims=True) + NORM_EPS)


def _mod_spec(mods, layer, chunk, tiles_per_group):
    return pl.BlockSpec((None, None, mods.shape[2], D), lambda i, *_: (layer, i // tiles_per_group, 0, chunk))


def _row_spec(tm, n):
    return pl.BlockSpec((tm, n), lambda i, *_: (i, 0))


def _const_spec(shape):
    nd = len(shape)
    return pl.BlockSpec(shape, lambda *_: (0,) * nd)


def _ada_body(c_ref, w_ref, b_ref, o_ref):
    c = c_ref[...]
    o_ref[...] = _mm(c * jax.nn.sigmoid(c), w_ref[...]) + b_ref[...]


def _ada(c_all, ada_w, ada_b):
    n = c_all.shape[0]
    tn = 1536
    return pl.pallas_call(
        _ada_body,
        grid=(DEPTH, 6 * D // tn),
        in_specs=[pl.BlockSpec((n, D), lambda l, j: (0, 0)),
                  pl.BlockSpec((None, D, tn), lambda l, j: (l, 0, j)),
                  pl.BlockSpec((None, 1, tn), lambda l, j: (l, 0, j))],
        out_specs=pl.BlockSpec((None, n, tn), lambda l, j: (l, 0, j)),
        out_shape=jax.ShapeDtypeStruct((DEPTH, n, 6 * D), F32),
        compiler_params=_cparams("parallel", "parallel"),
        name="ada",
    )(c_all, ada_w, ada_b.reshape(DEPTH, 1, 6 * D))


def _modnorm(x, g, sh, sc):
    return _rms(x) * g * (1.0 + sc) + sh


def _gelu(x):
    return 0.5 * x * (1.0 + lax.erf(x * (2.0 ** -0.5)))


def _inproj_body(x_ref, g_ref, sh_ref, sc_ref, w_ref, lng_ref, lnb_ref, ws_ref, bias_ref, oa_ref, pb_ref, *v_refs):
    h = _modnorm(x_ref[...], g_ref[...], sh_ref[...], sc_ref[...]).astype(BF16)
    pb_ref[...] = _mm(h, w_ref[:, 2 * A_WIDTH:])
    u = _gelu(_mm(h, w_ref[:, :A_WIDTH]))
    v = _gelu(_mm(h, w_ref[:, A_WIDTH:2 * A_WIDTH]))
    vc = v - jnp.mean(v, axis=-1, keepdims=True)
    var = jnp.mean(vc * vc, axis=-1, keepdims=True)
    vn = vc * lax.rsqrt(var + LN_EPS) * lng_ref[...] + lnb_ref[...]
    if v_refs:
        v_refs[0][...] = vn
    ii = lax.broadcasted_iota(jnp.int32, (CHUNK, CHUNK), 0)
    jj = lax.broadcasted_iota(jnp.int32, (CHUNK, CHUNK), 1)
    cg = A_WIDTH // A_GROUPS
    for grp in range(A_GROUPS):
        lanes = slice(grp * cg, (grp + 1) * cg)
        wc = jnp.where(jj <= ii, ws_ref[grp], 0.0).astype(BF16)
        for ci in range(x_ref.shape[0] // CHUNK):
            rows = slice(ci * CHUNK, (ci + 1) * CHUNK)
            z = _mm(wc, vn[rows, lanes]) + bias_ref[:, lanes]
            oa_ref[rows, lanes] = u[rows, lanes] * z


def _inproj(x, g, mods, layer, tm, w, ln_g, ln_b, ws, bias, emit_v):
    r = x.shape[0]
    tpg = (r // mods.shape[1]) // tm
    n = w.shape[1]
    half = jax.ShapeDtypeStruct((r, A_WIDTH), F32)
    return pl.pallas_call(
        _inproj_body,
        grid=(r // tm,),
        in_specs=[_row_spec(tm, D), _const_spec((1, D)), _mod_spec(mods, layer, 0, tpg),
                  _mod_spec(mods, layer, 1, tpg), _const_spec((D, n)), _const_spec((1, A_WIDTH)),
                  _const_spec((1, A_WIDTH)), _const_spec(ws.shape), _const_spec(bias.shape)],
        out_specs=[_row_spec(tm, A_WIDTH), _row_spec(tm, B_COLS)] + ([_row_spec(tm, A_WIDTH)] if emit_v else []),
        out_shape=[half, jax.ShapeDtypeStruct((r, B_COLS), F32)] + ([half] if emit_v else []),
        compiler_params=_cparams("parallel"),
        name="even_inproj_gmlp",
    )(x, g, mods, mods, w, ln_g, ln_b, ws, bias)


def _rope(seg, cos, s_lo, s_hi):
    return seg * cos + pltpu.roll(seg, LANES - ROT_DIM // 2, 1) * s_lo + pltpu.roll(seg, ROT_DIM // 2, 1) * s_hi


def _qkv_groups(x_ref, g_ref, sh_ref, sc_ref, w_ref, cos_ref, slo_ref, shi_ref):
    h = _modnorm(x_ref[...], g_ref[...], sh_ref[...], sc_ref[...]).astype(BF16)
    cos, s_lo, s_hi = cos_ref[...], slo_ref[...], shi_ref[...]
    for grp in range(D // LANES):
        cols = lambda part: slice(part * D + grp * LANES, part * D + (grp + 1) * LANES)
        q = _rope(_mm(h, w_ref[:, cols(0)]), cos, s_lo, s_hi) * HEAD ** -0.5
        k = _rope(_mm(h, w_ref[:, cols(1)]), cos, s_lo, s_hi)
        yield slice(grp * LANES, (grp + 1) * LANES), q, k, _mm(h, w_ref[:, cols(2)])


def _qkv_body(*refs):
    q_ref, kf_ref, vf_ref = refs[8:]
    for lanes, q, k, v in _qkv_groups(*refs[:8]):
        q_ref[:, lanes] = q
        kf_ref[:, lanes] = k
        vf_ref[:, lanes] = v


def _qkv_classes_body(*refs):
    q1, k1, v1, q4, k4, v4, q16, k16, v16, kf_ref, vf_ref, slab, slab4 = refs[8:]
    tm = q1.shape[0]
    for lanes, q, k, v in _qkv_groups(*refs[:8]):
        kf_ref[lanes, :] = k.T
        vf_ref[lanes, :] = v.T
        for i, (val, nat, o4, o16) in enumerate(((q, q1, q4, q16), (k, k1, k4, k16), (v, v1, v4, v16))):
            nat[:, lanes] = val.astype(BF16)
            slab[i] = val
            for r4 in range(4):
                cls = slab[i, pl.ds(r4, tm // 4, stride=4), :]
                o4[r4, :, lanes] = cls.astype(BF16)
                slab4[i, r4 * (tm // 4):(r4 + 1) * (tm // 4), :] = cls
            for r16 in range(16):
                start = (r16 % 4) * (tm // 4) + r16 // 4
                o16[r16, :, lanes] = slab4[i, pl.ds(start, tm // 16, stride=4), :].astype(BF16)


def _rope_tables(pos):
    half = ROT_DIM // 2
    inv = ROPE_THETA ** (-jnp.arange(half, dtype=F32) * (2.0 / ROT_DIM))
    ang = pos.astype(F32)[:, None] * inv[None, :]
    cos, sin = jnp.cos(ang), jnp.sin(ang)
    t = pos.shape[0]
    one, zero, z8 = jnp.ones((t, HEAD - ROT_DIM), F32), jnp.zeros((t, HEAD - ROT_DIM), F32), jnp.zeros((t, half), F32)
    tab = lambda *parts: jnp.tile(jnp.concatenate(parts, axis=1), (1, LANES // HEAD))
    return tab(cos, cos, one), tab(-sin, z8, zero), tab(z8, sin, zero)


def _qkv_in_specs(mods, layer, tm, tpg, tables_per_seq):
    tab_spec = pl.BlockSpec((tm, LANES), lambda i: (i % tables_per_seq, 0))
    return [_row_spec(tm, D), _const_spec((1, D)), _mod_spec(mods, layer, 0, tpg), _mod_spec(mods, layer, 1, tpg),
            _const_spec((D, 3 * D)), tab_spec, tab_spec, tab_spec]


def _qkv(x, g, mods, layer, tm, w, tables):
    r = x.shape[0]
    tpg = (r // mods.shape[1]) // tm
    f32 = jax.ShapeDtypeStruct((r, D), F32)
    return pl.pallas_call(
        _qkv_body,
        grid=(r // tm,),
        in_specs=_qkv_in_specs(mods, layer, tm, tpg, 1),
        out_specs=[_row_spec(tm, D)] * 3,
        out_shape=[f32, f32, f32],
        compiler_params=_cparams("parallel"),
        name="odd_qkv_decode",
    )(x, g, mods, mods, w, *tables)


def _qkv_classes(x, g, mods, layer, tm, w, tables, nseq):
    r = x.shape[0]
    s_len = r // nseq
    tps = s_len // tm
    cls_spec = lambda dil: pl.BlockSpec((None, dil, tm // dil, D), lambda i: (i // tps, 0, i % tps, 0))
    cls_shape = lambda dil: jax.ShapeDtypeStruct((nseq, dil, s_len // dil, D), BF16)
    nat = jax.ShapeDtypeStruct((r, D), BF16)
    f32 = jax.ShapeDtypeStruct((nseq, D, s_len), F32)
    f32_spec = pl.BlockSpec((None, D, tm), lambda i: (i // tps, 0, i % tps))
    return pl.pallas_call(
        _qkv_classes_body,
        grid=(r // tm,),
        in_specs=_qkv_in_specs(mods, layer, tm, tps, tps),
        out_specs=[_row_spec(tm, D)] * 3 + [cls_spec(4)] * 3 + [cls_spec(16)] * 3 + [f32_spec] * 2,
        out_shape=[nat] * 3 + [cls_shape(4)] * 3 + [cls_shape(16)] * 3 + [f32, f32],
        scratch_shapes=[pltpu.VMEM((3, tm, LANES), F32), pltpu.VMEM((3, tm, LANES), F32)],
        compiler_params=_cparams("parallel"),
        name="odd_qkv",
    )(x, g, mods, mods, w, *tables)


def _outproj_body(*refs, n_in):
    ins, ws = refs[:n_in], refs[n_in:2 * n_in]
    x_ref, gp_ref, gate_ref, o_ref = refs[2 * n_in:]
    y = _mm(ins[0][...], ws[0][...])
    for a, w in zip(ins[1:], ws[1:]):
        y = y + _mm(a[...], w[...])
    o_ref[...] = x_ref[...] + gate_ref[...] * (_rms(y) * gp_ref[...])


def _outproj(ins, ws, x, gp, mods, layer, tm):
    r = x.shape[0]
    tpg = (r // mods.shape[1]) // tm
    return pl.pallas_call(
        functools.partial(_outproj_body, n_in=len(ins)),
        grid=(r // tm,),
        in_specs=[_row_spec(tm, a.shape[1]) for a in ins] + [_const_spec(w.shape) for w in ws]
        + [_row_spec(tm, D), _const_spec((1, D)), _mod_spec(mods, layer, 2, tpg)],
        out_specs=_row_spec(tm, D),
        out_shape=jax.ShapeDtypeStruct((r, D), F32),
        compiler_params=_cparams("parallel"),
        name="outproj",
    )(*ins, *ws, x, gp, mods)


def _ffn_body(x_ref, g_ref, sh_ref, sc_ref, gate_ref, gp_ref, w1_ref, w2_ref, o_ref, h_scr, acc_scr):
    j = pl.program_id(1)

    @pl.when(j == 0)
    def _():
        h_scr[...] = _modnorm(x_ref[...], g_ref[...], sh_ref[...], sc_ref[...]).astype(BF16)
        acc_scr[...] = jnp.zeros_like(acc_scr)

    t = jnp.maximum(_mm(h_scr[...], w1_ref[...]), 0.0)
    acc_scr[...] += _mm(t * t, w2_ref[...])

    @pl.when(j == pl.num_programs(1) - 1)
    def _():
        o_ref[...] = x_ref[...] + gate_ref[...] * (_rms(acc_scr[...]) * gp_ref[...])


def _ffn(x, g, gp, mods, layer, tm, tf, w1, w2):
    r = x.shape[0]
    tpg = (r // mods.shape[1]) // tm
    return pl.pallas_call(
        _ffn_body,
        grid=(r // tm, D_FF // tf),
        in_specs=[_row_spec(tm, D), _const_spec((1, D)), _mod_spec(mods, layer, 3, tpg),
                  _mod_spec(mods, layer, 4, tpg), _mod_spec(mods, layer, 5, tpg), _const_spec((1, D)),
                  pl.BlockSpec((D, tf), lambda i, j: (0, j)), pl.BlockSpec((tf, D), lambda i, j: (j, 0))],
        out_specs=_row_spec(tm, D),
        out_shape=jax.ShapeDtypeStruct((r, D), F32),
        scratch_shapes=[pltpu.VMEM((tm, D), BF16), pltpu.VMEM((tm, D), F32)],
        compiler_params=_cparams("parallel", "arbitrary"),
        name="ffn",
    )(x, g, mods, mods, mods, gp, w1, w2)


def _seg_cumsum(x, seg):
    row = lax.broadcasted_iota(jnp.int32, (x.shape[0], 1), 0) % seg
    s = 1
    while s < seg:
        x = x + jnp.where(row >= s, pltpu.roll(x, s, 0), 0.0)
        s *= 2
    return x


def _rwkv_body(pb_ref, prev_ref, s0_ref, mu_ref, prm_ref, lora_ref, hsum_ref, out_ref, s_ref, last_scr,
               *, nb, tt, c):
    @pl.when(pl.program_id(1) == 0)
    def _():
        s_ref[...] = s0_ref[...]
        last_scr[...] = prev_ref[...]

    mu = mu_ref[...]
    w0, a0, k_k, k_a, r_k, ln_g, ln_b = (prm_ref[i:i + 1, :] for i in range(7))
    hsum = hsum_ref[...]
    headsum = lambda t: _mm(t, hsum, passes=2)
    row = lax.broadcasted_iota(jnp.int32, (tt, 1), 0)
    ii = lax.broadcasted_iota(jnp.int32, (c, 2 * c), 0)
    jj = lax.broadcasted_iota(jnp.int32, (c, 2 * c), 1) % c
    strict, incl = jj < ii, jj <= ii
    eye = jnp.where(lax.broadcasted_iota(jnp.int32, (c, c), 0) == lax.broadcasted_iota(jnp.int32, (c, c), 1), 1.0, 0.0).astype(F32)
    zeros_c = jnp.zeros((c, HEAD), BF16)
    n_chunk = tt // c
    levels = int(math.log2(c)) - 1

    seqs = []
    for s in range(nb):
        x = pb_ref[s]
        shifted = jnp.where(row == 0, last_scr[s], pltpu.roll(x, 1, 0))
        last_scr[s] = x[tt - 1:tt, :]
        xm = x + mu * (shifted - x)
        r, k, v = xm[:, :B_WIDTH], xm[:, B_WIDTH:2 * B_WIDTH], xm[:, 2 * B_WIDTH:3 * B_WIDTH]
        lo = xm[:, 3 * B_WIDTH:]
        z = -(w0 + _mm(jnp.tanh(lo), lora_ref[0]))
        softplus = jnp.maximum(z, 0.0) + jnp.log1p(jnp.exp(-jnp.abs(z)))
        lw = -jnp.exp(-softplus - 0.5)
        a = jax.nn.sigmoid(a0 + _mm(lo, lora_ref[1]))
        g = _mm(jax.nn.sigmoid(lo), lora_ref[2])
        kk = k * k_k
        kk = kk * lax.rsqrt(headsum(kk * kk) + 1e-12)
        k2 = k * (1.0 + (a - 1.0) * k_a)
        lp = _seg_cumsum(lw, c)
        e_pos, e_neg = jnp.exp(lp), jnp.exp(-lp)
        seqs.append(dict(
            r=r, k2=k2, v=v, g=g, e_pos=e_pos,
            a_t=-kk * jnp.exp(lp - lw), b_t=kk * a * e_neg, k_t=k2 * e_neg, r_t=r * e_pos, v_b=v))

    def piece(s, name, ci, h):
        return seqs[s][name][ci * c:(ci + 1) * c, h * HEAD:(h + 1) * HEAD].astype(BF16)

    chains = [(s, h) for s in range(nb) for h in range(B_HEADS)]
    units = [(ci, s, h) for ci in range(n_chunk) for (s, h) in chains]

    a_u = {u: piece(u[1], "a_t", u[0], u[2]) for u in units}
    r_u = {u: piece(u[1], "r_t", u[0], u[2]) for u in units}
    v_u = {u: piece(u[1], "v_b", u[0], u[2]) for u in units}
    bk_u = {u: jnp.concatenate([piece(u[1], "b_t", u[0], u[2]), piece(u[1], "k_t", u[0], u[2])], axis=0) for u in units}
    zv_u = {u: jnp.concatenate([zeros_c, v_u[u]], axis=0) for u in units}
    g1 = {u: jnp.where(strict, _mm(a_u[u], bk_u[u], NT), 0.0) for u in units}
    g2 = {u: jnp.where(incl, _mm(r_u[u], bk_u[u], NT), 0.0).astype(BF16) for u in units}
    pw = {u: g1[u][:, :c] for u in units}
    inv = {u: eye + pw[u] for u in units}
    for _ in range(levels):
        pw = {u: _mm(pw[u], pw[u]) for u in units}
        step = {u: _mm(inv[u], pw[u]) for u in units}
        inv = {u: inv[u] + step[u] for u in units}
    res = {u: eye - inv[u] + _mm3(g1[u][:, :c], inv[u]) for u in units}
    corr = {u: _mm(inv[u], res[u]) for u in units}
    inv = {u: inv[u] + corr[u] for u in units}
    g1 = {u: g1[u].astype(BF16) for u in units}

    state = {ch: s_ref[ch[0], ch[1]] for ch in chains}
    o_u = {}
    for ci in range(n_chunk):
        us = [(ci, s, h) for (s, h) in chains]
        sb = {u: state[u[1:]].astype(BF16) for u in us}
        rhs = {u: _mm(a_u[u], sb[u], NT) + _mm(g1[u], zv_u[u]) for u in us}
        uu = {u: _mm(inv[u], rhs[u]) for u in us}
        uv = {u: jnp.concatenate([uu[u].astype(BF16), v_u[u]], axis=0) for u in us}
        for u in us:
            o_u[u] = _mm(r_u[u], sb[u], NT) + _mm(g2[u], uv[u])
        for u in us:
            s, h = u[1:]
            p_end = seqs[s]["e_pos"][(ci + 1) * c - 1:(ci + 1) * c, h * HEAD:(h + 1) * HEAD]
            state[(s, h)] = (state[(s, h)] + _mm(uv[u], bk_u[u], TN)) * p_end
    for (s, h) in chains:
        s_ref[s, h] = state[(s, h)]

    for s in range(nb):
        q = seqs[s]
        o_rows = [jnp.concatenate([o_u[(ci, s, h)] for h in range(B_HEADS)], axis=1) for ci in range(n_chunk)]
        o = o_rows[0] if n_chunk == 1 else jnp.concatenate(o_rows, axis=0)
        inv_n = 1.0 / HEAD
        oc = o - headsum(o) * inv_n
        on = oc * lax.rsqrt(headsum(oc * oc) * inv_n + GN_EPS) * ln_g + ln_b
        bonus = headsum(q["r"] * q["k2"] * r_k) * q["v"]
        out_ref[s] = (on + bonus) * q["g"]


def _rwkv(pb, prev, s0, mu, prm, lora, hsum, nb, tt, c):
    n, t, _ = pb.shape
    return pl.pallas_call(
        functools.partial(_rwkv_body, nb=nb, tt=tt, c=c),
        grid=(n // nb, t // tt),
        in_specs=[pl.BlockSpec((nb, tt, B_COLS), lambda i, j: (i, j, 0)),
                  pl.BlockSpec((nb, 1, B_COLS), lambda i, j: (i, 0, 0)),
                  pl.BlockSpec((nb, B_HEADS, HEAD, HEAD), lambda i, j: (i, 0, 0, 0)),
                  _const_spec(mu.shape), _const_spec(prm.shape), _const_spec(lora.shape), _const_spec(hsum.shape)],
        out_specs=[pl.BlockSpec((nb, tt, B_WIDTH), lambda i, j: (i, j, 0)),
                   pl.BlockSpec((nb, B_HEADS, HEAD, HEAD), lambda i, j: (i, 0, 0, 0))],
        out_shape=[jax.ShapeDtypeStruct((n, t, B_WIDTH), F32), jax.ShapeDtypeStruct(s0.shape, F32)],
        scratch_shapes=[pltpu.VMEM((nb, 1, B_COLS), F32)],
        compiler_params=_cparams("parallel", "arbitrary"),
        name="rwkv7",
    )(pb, prev, s0, mu, prm, lora, hsum)


def _band_attn_body(q_ref, k_ref, v_ref, acc_ref, st_ref, *, nblk):
    n = pl.program_id(2)
    nk = BAND if nblk == 1 else 2 * BAND
    cur = pl.multiple_of(n * BAND, BAND)
    prev = pl.multiple_of(jnp.maximum(n - 1, 0) * BAND, BAND)

    def keys(ref, lanes):
        if nblk == 1:
            return ref[:, lanes]
        return jnp.concatenate([ref[pl.ds(prev, BAND), lanes], ref[pl.ds(cur, BAND), lanes]], axis=0)

    qi = lax.broadcasted_iota(jnp.int32, (BAND, nk), 0)
    kc = lax.broadcasted_iota(jnp.int32, (BAND, nk), 1)
    if nblk == 1:
        mask = kc <= qi
    else:
        first_key = jnp.where(n > 0, 0, BAND)
        mask = (kc >= qi) & (kc <= qi + BAND) & (kc >= first_key)
    lane = lax.broadcasted_iota(jnp.int32, (BAND, LANES), 1)
    lane_k = lax.broadcasted_iota(jnp.int32, (nk, LANES), 1)
    stats = jnp.zeros((BAND, LANES), F32)
    group = 8 if nblk == 1 else 4
    for g0 in range(0, C_HEADS // 2, group):
        pairs = range(g0, g0 + group)
        scores = []
        for pair in pairs:
            lanes = slice(pair * LANES, (pair + 1) * LANES)
            q2, k2 = q_ref[:, lanes], keys(k_ref, lanes)
            for half in range(2):
                in_head = (lane >= HEAD) if half else (lane < HEAD)
                scores.append(_mm(jnp.where(in_head, q2, jnp.zeros_like(q2)), k2, NT))
        probs = []
        for i, s in enumerate(scores):
            s = jnp.where(mask, s, NEG)
            m = jnp.max(s, axis=-1, keepdims=True)
            probs.append(jnp.exp(s - m).astype(BF16))
            stats = jnp.where(lane == (i % 2) * (C_HEADS // 2) + g0 + i // 2, m, stats)
        for i, pair in enumerate(pairs):
            lanes = slice(pair * LANES, (pair + 1) * LANES)
            v2 = keys(v_ref, lanes)
            lo = _mm(probs[2 * i], jnp.where(lane_k < HEAD, v2, jnp.ones_like(v2)))
            hi = _mm(probs[2 * i + 1], jnp.where(lane_k >= HEAD, v2, jnp.ones_like(v2)))
            acc_ref[:, lanes] = jnp.where(lane < HEAD, lo, hi).astype(acc_ref.dtype)
            stats = jnp.where(lane == DEN_EVEN + pair, lo, stats)
            stats = jnp.where(lane == DEN_ODD + pair, hi, stats)
    st_ref[...] = stats


def _band_attn(q, k, v):
    nseq, dil, sub, _ = q.shape
    nblk = sub // BAND
    blk = lambda width: pl.BlockSpec((None, None, BAND, width), lambda b, r, n: (b, r, n, 0))
    cls = pl.BlockSpec((None, None, sub, D), lambda b, r, n: (b, r, 0, 0))
    return pl.pallas_call(
        functools.partial(_band_attn_body, nblk=nblk),
        grid=(nseq, dil, nblk),
        in_specs=[blk(D), cls, cls],
        out_specs=[blk(D), blk(LANES)],
        out_shape=[jax.ShapeDtypeStruct(q.shape, q.dtype), jax.ShapeDtypeStruct((nseq, dil, sub, LANES), F32)],
        compiler_params=_cparams("parallel", "parallel", "arbitrary"),
        name=f"band_attn_d{dil}",
    )(q, k, v)


def _mix_body(a1_ref, a4_ref, a16_ref, s1_ref, s4_ref, s16_ref, e_ref, w_ref, x_ref, gp_ref, gate_ref, o_ref,
              slab4, slab16, st4, st16, o_scr):
    tm = x_ref.shape[0]
    ngrp = D // LANES
    for dil, a_ref, s_ref, slab, st in ((4, a4_ref, s4_ref, slab4, st4), (16, a16_ref, s16_ref, slab16, st16)):
        for r in range(dil):
            rows = pl.ds(r, tm // dil, stride=dil)
            st[rows, :] = s_ref[r]
            for grp in range(ngrp):
                slab[grp, rows, :] = a_ref[r, :, grp * LANES:(grp + 1) * LANES].astype(F32)
    sts = [s1_ref[...], st4[...], st16[...]]
    lane = lax.broadcasted_iota(jnp.int32, sts[0].shape, 1)
    top = jnp.maximum(jnp.maximum(sts[0], sts[1]), sts[2])
    es = [jnp.exp(st - top) for st in sts]
    dens = [jnp.where(lane < C_HEADS // 2, pltpu.roll(st, LANES - DEN_EVEN, 1),
                      pltpu.roll(st, LANES - DEN_ODD + C_HEADS // 2, 1)) for st in sts]
    total = es[0] * dens[0] + es[1] * dens[1] + es[2] * dens[2]
    coefs = [_mm(jnp.where(lane < C_HEADS, e / total, 0.0), e_ref[...], passes=2) for e in es]
    for grp in range(ngrp):
        lanes = slice(grp * LANES, (grp + 1) * LANES)
        o = (coefs[0][:, lanes] * a1_ref[:, lanes].astype(F32) + coefs[1][:, lanes] * slab4[grp]
             + coefs[2][:, lanes] * slab16[grp])
        o_scr[:, lanes] = o.astype(BF16)
    y = _mm(o_scr[...], w_ref[...])
    o_ref[...] = x_ref[...] + gate_ref[...] * (_rms(y) * gp_ref[...])


def _mix_outproj(accs, sts, expand, w, x, gp, mods, layer, tm):
    r = x.shape[0]
    nseq = accs[1].shape[0]
    tps = (r // nseq) // tm
    cls_spec = lambda dil, width: pl.BlockSpec((None, dil, tm // dil, width), lambda i: (i // tps, 0, i % tps, 0))
    return pl.pallas_call(
        _mix_body,
        grid=(r // tm,),
        in_specs=[_row_spec(tm, D), cls_spec(4, D), cls_spec(16, D), _row_spec(tm, LANES), cls_spec(4, LANES),
                  cls_spec(16, LANES), _const_spec(expand.shape), _const_spec(w.shape), _row_spec(tm, D),
                  _const_spec((1, D)), _mod_spec(mods, layer, 2, tps)],
        out_specs=_row_spec(tm, D),
        out_shape=jax.ShapeDtypeStruct((r, D), F32),
        scratch_shapes=[pltpu.VMEM((D // LANES, tm, LANES), F32), pltpu.VMEM((D // LANES, tm, LANES), F32),
                        pltpu.VMEM((tm, LANES), F32), pltpu.VMEM((tm, LANES), F32), pltpu.VMEM((tm, D), BF16)],
        compiler_params=_cparams("parallel"),
        name="attn_mix_outproj",
    )(*accs, *sts, expand, w, x, gp, mods)


def _attn_sample_body(q_ref, kn_ref, vn_ref, kc_ref, vc_ref, o_ref, *, t_len, window):
    tq = lax.broadcasted_iota(jnp.int32, (t_len, window), 0)
    back = window + tq - lax.broadcasted_iota(jnp.int32, (t_len, window), 1)
    cnt = (jnp.where(back <= BAND, 1.0, 0.0) + jnp.where((back <= 4 * BAND) & ((back & 3) == 0), 1.0, 0.0)
           + jnp.where((back & 15) == 0, 1.0, 0.0))
    tn = lax.broadcasted_iota(jnp.int32, (t_len, t_len), 0)
    dn = tn - lax.broadcasted_iota(jnp.int32, (t_len, t_len), 1)
    mult = jnp.where(dn == 0, 3.0, 0.0) + jnp.where(dn > 0, 1.0, 0.0) + jnp.where(dn == 4, 1.0, 0.0)
    lane = lax.broadcasted_iota(jnp.int32, (t_len, LANES), 1)
    for pair in range(q_ref.shape[1] // LANES):
        lanes = slice(pair * LANES, (pair + 1) * LANES)
        q2, kn, vn = q_ref[:, lanes], kn_ref[:, lanes].astype(BF16), vn_ref[:, lanes].astype(BF16)
        kc, vc = kc_ref[lanes, :].astype(BF16), vc_ref[lanes, :].astype(BF16)
        outs = []
        for half in range(2):
            qh = jnp.where((lane >= HEAD) if half else (lane < HEAD), q2, 0.0).astype(BF16)
            s_c = jnp.where(cnt > 0.0, _mm(qh, kc), NEG)
            s_n = jnp.where(mult > 0.0, _mm(qh, kn, NT), NEG)
            top = jnp.maximum(jnp.max(s_c, axis=-1, keepdims=True), jnp.max(s_n, axis=-1, keepdims=True))
            p_c = jnp.exp(s_c - top) * cnt
            p_n = jnp.exp(s_n - top) * mult
            den = jnp.sum(p_c, axis=-1, keepdims=True) + jnp.sum(p_n, axis=-1, keepdims=True)
            outs.append((_mm(p_c, vc, NT) + _mm(p_n, vn)) / den)
        o_ref[:, lanes] = jnp.where(lane < HEAD, outs[0], outs[1])


def _attn_sample(q, kn, vn, cache_k, cache_v, layer, nseq):
    window = cache_k.shape[-1]
    t_len = q.shape[0] // nseq
    assert window == 16 * BAND and t_len == 8
    width = 2 * LANES
    new = pl.BlockSpec((t_len, width), lambda b, g: (b, g))
    cached = pl.BlockSpec((None, None, width, window), lambda b, g: (layer, b, g, 0))
    return pl.pallas_call(
        functools.partial(_attn_sample_body, t_len=t_len, window=window),
        grid=(nseq, D // width),
        in_specs=[new, new, new, cached, cached],
        out_specs=new,
        out_shape=jax.ShapeDtypeStruct(q.shape, F32),
        compiler_params=_cparams("parallel", "parallel"),
        name="attn_sample",
    )(q, kn, vn, cache_k, cache_v)


def kernel(x_prompt, x_sample, state_wkv, state_shift, cache_k, cache_v, c_prompt, c_sample, ada_w, ada_b, norm_mix_pre, norm_mix_post, norm_ffn_pre, norm_ffn_post, ffn_w1, ffn_w2, ev_w_in, ev_w_out, gm_ln_g, gm_ln_b, gm_ws, gm_bs, rw_mu, rw_w0, rw_w2, rw_a0, rw_a2, rw_g2, rw_kk, rw_ka, rw_rk, rw_lnx_g, rw_lnx_b, od_w_qkv, od_w_out):
    n_p, s_len, _ = x_prompt.shape
    n_s, t_len, _ = x_sample.shape
    r_p, r_s = n_p * s_len, n_s * t_len
    row = lambda t: t.reshape(1, -1).astype(F32)

    mods = _ada(jnp.concatenate([c_prompt, c_sample], axis=0).astype(F32), ada_w, ada_b)
    mods_p = mods[:, :n_p].reshape(DEPTH, n_p, 1, 6 * D)
    mods_s = jnp.repeat(mods[:, n_p:], t_len, axis=1).reshape(DEPTH, 1, r_s, 6 * D)

    xp = x_prompt.astype(F32).reshape(r_p, D)
    xs = x_sample.astype(F32).reshape(r_s, D)

    hsum = jnp.kron(jnp.eye(B_HEADS, dtype=F32), jnp.ones((HEAD, HEAD), F32)).astype(BF16)
    tile_head = 2 * (jnp.arange(LANES) % (C_HEADS // 2)) + jnp.arange(LANES) // (C_HEADS // 2)
    expand = ((tile_head[:, None] == jnp.arange(D)[None, :] // HEAD) & (jnp.arange(LANES)[:, None] < C_HEADS)).astype(BF16)
    tabs_p = _rope_tables(jnp.arange(s_len))
    tabs_s = _rope_tables(jnp.tile(PAST_LEN + jnp.arange(t_len), n_s))
    zeros_prev = jnp.zeros((n_p, 1, B_COLS), F32)
    zeros_state = jnp.zeros((n_p, B_HEADS, HEAD, HEAD), F32)
    reps = CHUNK // t_len
    feat_major = lambda c: jnp.transpose(c.astype(F32), (0, 1, 3, 4, 2)).reshape(c.shape[0], c.shape[1], D, c.shape[2])
    cache_kt, cache_vt = feat_major(cache_k), feat_major(cache_v)
    tok_major = lambda t: jnp.transpose(t.reshape(n_p, C_HEADS, HEAD, s_len), (0, 3, 1, 2))

    tm_p = 512
    outs = {k: [] for k in ("wkv_p", "shift_p", "k_p", "v_p", "wkv_s", "shift_s", "k_s", "v_s", "gv_s")}
    for l in range(DEPTH):
        g_pre, g_post = row(norm_mix_pre[l]), row(norm_mix_post[l])
        if l % 2 == 0:
            e = l // 2
            w_in = ev_w_in[e].astype(BF16)
            w_out = ev_w_out[e].astype(BF16)
            w_oa, w_ob = w_out[:A_WIDTH], w_out[A_WIDTH:]
            ln_g, ln_b = row(gm_ln_g[e]), row(gm_ln_b[e])
            ws = gm_ws[e].astype(F32)
            bias_p = jnp.repeat(gm_bs[e].T, A_WIDTH // A_GROUPS, axis=1)
            ws_s = jax.vmap(lambda m: jnp.kron(jnp.eye(reps, dtype=F32), m))(ws[:, :t_len, :t_len])
            bias_s = jnp.tile(bias_p[:t_len], (reps, 1))
            mu = row(rw_mu[e])
            zpad = jnp.zeros((7 * B_WIDTH,), F32)
            prm = jnp.concatenate([rw_w0[e], rw_a0[e], rw_kk[e], rw_ka[e], rw_rk[e].reshape(-1), rw_lnx_g[e],
                                   rw_lnx_b[e], zpad[:B_WIDTH]]).reshape(8, B_WIDTH).astype(F32)
            zl = lambda a, b: jnp.zeros((a, b), F32)
            lora = jnp.stack([
                jnp.concatenate([rw_w2[e], zl(96, B_WIDTH)], axis=0),
                jnp.concatenate([zl(32, B_WIDTH), rw_a2[e], zl(64, B_WIDTH)], axis=0),
                jnp.concatenate([zl(64, B_WIDTH), rw_g2[e]], axis=0)]).astype(BF16)

            oa_p, pb_p = _inproj(xp, g_pre, mods_p, l, tm_p, w_in, ln_g, ln_b, ws, bias_p, False)
            pb_p3 = pb_p.reshape(n_p, s_len, B_COLS)
            ob_p, st_p = _rwkv(pb_p3, zeros_prev, zeros_state, mu, prm, lora, hsum, 2, 128, 64)
            xp = _outproj([oa_p, ob_p.reshape(r_p, B_WIDTH)], [w_oa, w_ob], xp, g_post, mods_p, l, tm_p)
            outs["wkv_p"].append(st_p)
            outs["shift_p"].append(pb_p3[:, -1])

            oa_s, pb_s, gv = _inproj(xs, g_pre, mods_s, l, r_s, w_in, ln_g, ln_b, ws_s, bias_s, True)
            pb_s3 = pb_s.reshape(n_s, t_len, B_COLS)
            ob_s, st_s = _rwkv(pb_s3, state_shift[e].astype(F32).reshape(n_s, 1, B_COLS), state_wkv[e].astype(F32),
                               mu, prm, lora, hsum, 8, t_len, t_len)
            xs = _outproj([oa_s, ob_s.reshape(r_s, B_WIDTH)], [w_oa, w_ob], xs, g_post, mods_s, l, r_s)
            outs["wkv_s"].append(st_s)
            outs["shift_s"].append(pb_s3[:, -1])
            outs["gv_s"].append(gv.reshape(n_s, t_len, A_WIDTH))
        else:
            o = l // 2
            w_qkv = od_w_qkv[o].astype(BF16)
            w_out = od_w_out[o].astype(BF16)
            q1, k1, v1, q4, k4, v4, q16, k16, v16, kf, vf = _qkv_classes(xp, g_pre, mods_p, l, 256, w_qkv, tabs_p, n_p)
            nat = lambda t: t.reshape(n_p, 1, s_len, D)
            res = [_band_attn(*qkv) for qkv in ((nat(q1), nat(k1), nat(v1)), (q4, k4, v4), (q16, k16, v16))]
            accs = [res[0][0].reshape(r_p, D), res[1][0], res[2][0]]
            sts = [res[0][1].reshape(r_p, LANES), res[1][1], res[2][1]]
            xp = _mix_outproj(accs, sts, expand, w_out, xp, g_post, mods_p, l, 256)
            outs["k_p"].append(tok_major(kf))
            outs["v_p"].append(tok_major(vf))

            qs, kfs, vfs = _qkv(xs, g_pre, mods_s, l, r_s, w_qkv, tabs_s)
            heads = lambda t: t.reshape(n_s, t_len, C_HEADS, HEAD)
            att = _attn_sample(qs, kfs, vfs, cache_kt, cache_vt, o, n_s)
            xs = _outproj([att], [w_out], xs, g_post, mods_s, l, r_s)
            outs["k_s"].append(heads(kfs))
            outs["v_s"].append(heads(vfs))
        w1, w2 = ffn_w1[l].astype(BF16), ffn_w2[l].astype(BF16)
        g_fpre, g_fpost = row(norm_ffn_pre[l]), row(norm_ffn_post[l])
        xp = _ffn(xp, g_fpre, g_fpost, mods_p, l, 1024, 512, w1, w2)
        xs = _ffn(xs, g_fpre, g_fpost, mods_s, l, r_s, 512, w1, w2)

    st = lambda name: jnp.stack(outs[name])
    return (xp.reshape(n_p, s_len, D), xs.reshape(n_s, t_len, D), st("wkv_p"), st("shift_p"), st("k_p"),
            st("v_p"), st("wkv_s"), st("shift_s"), st("k_s"), st("v_s"), st("gv_s"))
```

```python
import functools
import math

import jax
import jax.numpy as jnp
from jax import lax
from jax.experimental import pallas as pl
from jax.experimental.pallas import tpu as pltpu

F32 = jnp.float32
BF16 = jnp.bfloat16

D = 1024
DEPTH = 4
D_FF = 4 * D
A_WIDTH = 512
A_GROUPS = 4
CHUNK = 128
B_WIDTH = 512
B_HEADS = 8
HEAD = 64
LORA = 128
B_COLS = 3 * B_WIDTH + LORA
C_HEADS = 16
ROT_DIM = 16
ROPE_THETA = 500000.0
DILATIONS = (1, 4, 16)
BAND = 128
PAST_LEN = 8192
NORM_EPS = 1e-6
LN_EPS = 1e-5
GN_EPS = 64e-5
NEG = -1e30

LANES = 128
DEN_EVEN, DEN_ODD = 64, 16
VMEM_LIMIT = 52 * 1024 * 1024


def _cparams(*sem):
    return pltpu.CompilerParams(dimension_semantics=sem, vmem_limit_bytes=VMEM_LIMIT)


def _split(x):
    hi = x.astype(BF16)
    return hi, (x - hi.astype(F32)).astype(BF16)


def _mm(a, b, dims=((1,), (0,)), passes=1):
    dn = (dims, ((), ()))
    dot = lambda p, q: lax.dot_general(p, q, dn, preferred_element_type=F32)
    if passes == 1:
        return dot(a.astype(BF16), b.astype(BF16))
    ah, al = _split(a)
    return dot(ah, b) + dot(al, b)


NT = ((1,), (1,))
TN = ((0,), (0,))


def _rms(x):
    return x * lax.rsqrt(jnp.mean(x * x, axis=-1, keepdims=True) + NORM_EPS)


def _mod_spec(mods, layer, chunk, tiles_per_group):
    return pl.BlockSpec((None, None, mods.shape[2], D), lambda i, *_: (layer, i // tiles_per_group, 0, chunk))


def _row_spec(tm, n):
    return pl.BlockSpec((tm, n), lambda i, *_: (i, 0))


def _const_spec(shape):
    nd = len(shape)
    return pl.BlockSpec(shape, lambda *_: (0,) * nd)


def _ada_body(c_ref, w_ref, b_ref, o_ref):
    c = c_ref[...]
    o_ref[...] = _mm(c * jax.nn.sigmoid(c), w_ref[...]) + b_ref[...]


def _ada(c_all, ada_w, ada_b):
    n = c_all.shape[0]
    tn = 1536
    return pl.pallas_call(
        _ada_body,
        grid=(DEPTH, 6 * D // tn),
        in_specs=[pl.BlockSpec((n, D), lambda l, j: (0, 0)),
                  pl.BlockSpec((None, D, tn), lambda l, j: (l, 0, j)),
                  pl.BlockSpec((None, 1, tn), lambda l, j: (l, 0, j))],
        out_specs=pl.BlockSpec((None, n, tn), lambda l, j: (l, 0, j)),
        out_shape=jax.ShapeDtypeStruct((DEPTH, n, 6 * D), F32),
        compiler_params=_cparams("parallel", "parallel"),
        name="ada",
    )(c_all, ada_w, ada_b.reshape(DEPTH, 1, 6 * D))


def _modnorm(x, g, sh, sc):
    return _rms(x) * g * (1.0 + sc) + sh


def _gelu(x):
    return 0.5 * x * (1.0 + lax.erf(x * (2.0 ** -0.5)))


def _inproj_body(x_ref, g_ref, sh_ref, sc_ref, w_ref, lng_ref, lnb_ref, ws_ref, bias_ref, oa_ref, pb_ref, *v_refs):
    h = _modnorm(x_ref[...], g_ref[...], sh_ref[...], sc_ref[...]).astype(BF16)
    pb_ref[...] = _mm(h, w_ref[:, 2 * A_WIDTH:])
    u = _gelu(_mm(h, w_ref[:, :A_WIDTH]))
    v = _gelu(_mm(h, w_ref[:, A_WIDTH:2 * A_WIDTH]))
    vc = v - jnp.mean(v, axis=-1, keepdims=True)
    var = jnp.mean(vc * vc, axis=-1, keepdims=True)
    vn = vc * lax.rsqrt(var + LN_EPS) * lng_ref[...] + lnb_ref[...]
    if v_refs:
        v_refs[0][...] = vn
    ii = lax.broadcasted_iota(jnp.int32, (CHUNK, CHUNK), 0)
    jj = lax.broadcasted_iota(jnp.int32, (CHUNK, CHUNK), 1)
    cg = A_WIDTH // A_GROUPS
    for grp in range(A_GROUPS):
        lanes = slice(grp * cg, (grp + 1) * cg)
        wc = jnp.where(jj <= ii, ws_ref[grp], 0.0).astype(BF16)
        for ci in range(x_ref.shape[0] // CHUNK):
            rows = slice(ci * CHUNK, (ci + 1) * CHUNK)
            z = _mm(wc, vn[rows, lanes]) + bias_ref[:, lanes]
            oa_ref[rows, lanes] = u[rows, lanes] * z


def _inproj(x, g, mods, layer, tm, w, ln_g, ln_b, ws, bias, emit_v):
    r = x.shape[0]
    tpg = (r // mods.shape[1]) // tm
    n = w.shape[1]
    half = jax.ShapeDtypeStruct((r, A_WIDTH), F32)
    return pl.pallas_call(
        _inproj_body,
        grid=(r // tm,),
        in_specs=[_row_spec(tm, D), _const_spec((1, D)), _mod_spec(mods, layer, 0, tpg),
                  _mod_spec(mods, layer, 1, tpg), _const_spec((D, n)), _const_spec((1, A_WIDTH)),
                  _const_spec((1, A_WIDTH)), _const_spec(ws.shape), _const_spec(bias.shape)],
        out_specs=[_row_spec(tm, A_WIDTH), _row_spec(tm, B_COLS)] + ([_row_spec(tm, A_WIDTH)] if emit_v else []),
        out_shape=[half, jax.ShapeDtypeStruct((r, B_COLS), F32)] + ([half] if emit_v else []),
        compiler_params=_cparams("parallel"),
        name="even_inproj_gmlp",
    )(x, g, mods, mods, w, ln_g, ln_b, ws, bias)


def _rope(seg, cos, s_lo, s_hi):
    return seg * cos + pltpu.roll(seg, LANES - ROT_DIM // 2, 1) * s_lo + pltpu.roll(seg, ROT_DIM // 2, 1) * s_hi


def _qkv_groups(x_ref, g_ref, sh_ref, sc_ref, w_ref, cos_ref, slo_ref, shi_ref):
    h = _modnorm(x_ref[...], g_ref[...], sh_ref[...], sc_ref[...]).astype(BF16)
    cos, s_lo, s_hi = cos_ref[...], slo_ref[...], shi_ref[...]
    wide = 2 * LANES
    for pair in range(D // wide):
        q2, k2, v2 = (_mm(h, w_ref[:, part * D + pair * wide:part * D + (pair + 1) * wide]) for part in range(3))
        for half in range(2):
            part = slice(half * LANES, (half + 1) * LANES)
            grp = 2 * pair + half
            yield (slice(grp * LANES, (grp + 1) * LANES), _rope(q2[:, part], cos, s_lo, s_hi) * HEAD ** -0.5,
                   _rope(k2[:, part], cos, s_lo, s_hi), v2[:, part])


def _qkv_body(*refs):
    q_ref, kf_ref, vf_ref = refs[8:]
    for lanes, q, k, v in _qkv_groups(*refs[:8]):
        q_ref[:, lanes] = q
        kf_ref[:, lanes] = k
        vf_ref[:, lanes] = v


def _qkv_classes_body(*refs):
    q1, k1, v1, q4, k4, v4, q16, k16, v16, kf_ref, vf_ref, slab, slab4 = refs[8:]
    tm = q1.shape[0]
    for lanes, q, k, v in _qkv_groups(*refs[:8]):
        kf_ref[lanes, :] = k.T
        vf_ref[lanes, :] = v.T
        for i, (val, nat, o4, o16) in enumerate(((q, q1, q4, q16), (k, k1, k4, k16), (v, v1, v4, v16))):
            nat[:, lanes] = val.astype(BF16)
            slab[i] = val
            for r4 in range(4):
                cls = slab[i, pl.ds(r4, tm // 4, stride=4), :]
                o4[r4, :, lanes] = cls.astype(BF16)
                slab4[i, r4 * (tm // 4):(r4 + 1) * (tm // 4), :] = cls
            for r16 in range(16):
                start = (r16 % 4) * (tm // 4) + r16 // 4
                o16[r16, :, lanes] = slab4[i, pl.ds(start, tm // 16, stride=4), :].astype(BF16)


def _rope_tables(pos):
    half = ROT_DIM // 2
    inv = ROPE_THETA ** (-jnp.arange(half, dtype=F32) * (2.0 / ROT_DIM))
    ang = pos.astype(F32)[:, None] * inv[None, :]
    cos, sin = jnp.cos(ang), jnp.sin(ang)
    t = pos.shape[0]
    one, zero, z8 = jnp.ones((t, HEAD - ROT_DIM), F32), jnp.zeros((t, HEAD - ROT_DIM), F32), jnp.zeros((t, half), F32)
    tab = lambda *parts: jnp.tile(jnp.concatenate(parts, axis=1), (1, LANES // HEAD))
    return tab(cos, cos, one), tab(-sin, z8, zero), tab(z8, sin, zero)


def _qkv_in_specs(mods, layer, tm, tpg, tables_per_seq):
    tab_spec = pl.BlockSpec((tm, LANES), lambda i: (i % tables_per_seq, 0))
    return [_row_spec(tm, D), _const_spec((1, D)), _mod_spec(mods, layer, 0, tpg), _mod_spec(mods, layer, 1, tpg),
            _const_spec((D, 3 * D)), tab_spec, tab_spec, tab_spec]


def _qkv(x, g, mods, layer, tm, w, tables):
    r = x.shape[0]
    tpg = (r // mods.shape[1]) // tm
    f32 = jax.ShapeDtypeStruct((r, D), F32)
    return pl.pallas_call(
        _qkv_body,
        grid=(r // tm,),
        in_specs=_qkv_in_specs(mods, layer, tm, tpg, 1),
        out_specs=[_row_spec(tm, D)] * 3,
        out_shape=[f32, f32, f32],
        compiler_params=_cparams("parallel"),
        name="odd_qkv_decode",
    )(x, g, mods, mods, w, *tables)


def _qkv_classes(x, g, mods, layer, tm, w, tables, nseq):
    r = x.shape[0]
    s_len = r // nseq
    tps = s_len // tm
    cls_spec = lambda dil: pl.BlockSpec((None, dil, tm // dil, D), lambda i: (i // tps, 0, i % tps, 0))
    cls_shape = lambda dil: jax.ShapeDtypeStruct((nseq, dil, s_len // dil, D), BF16)
    nat = jax.ShapeDtypeStruct((r, D), BF16)
    f32 = jax.ShapeDtypeStruct((nseq, D, s_len), F32)
    f32_spec = pl.BlockSpec((None, D, tm), lambda i: (i // tps, 0, i % tps))
    return pl.pallas_call(
        _qkv_classes_body,
        grid=(r // tm,),
        in_specs=_qkv_in_specs(mods, layer, tm, tps, tps),
        out_specs=[_row_spec(tm, D)] * 3 + [cls_spec(4)] * 3 + [cls_spec(16)] * 3 + [f32_spec] * 2,
        out_shape=[nat] * 3 + [cls_shape(4)] * 3 + [cls_shape(16)] * 3 + [f32, f32],
        scratch_shapes=[pltpu.VMEM((3, tm, LANES), F32), pltpu.VMEM((3, tm, LANES), F32)],
        compiler_params=_cparams("parallel"),
        name="odd_qkv",
    )(x, g, mods, mods, w, *tables)


def _outproj_body(*refs, n_in):
    ins, ws = refs[:n_in], refs[n_in:2 * n_in]
    x_ref, gp_ref, gate_ref, o_ref = refs[2 * n_in:]
    y = _mm(ins[0][...], ws[0][...])
    for a, w in zip(ins[1:], ws[1:]):
        y = y + _mm(a[...], w[...])
    o_ref[...] = x_ref[...] + gate_ref[...] * (_rms(y) * gp_ref[...])


def _outproj(ins, ws, x, gp, mods, layer, tm):
    r = x.shape[0]
    tpg = (r // mods.shape[1]) // tm
    return pl.pallas_call(
        functools.partial(_outproj_body, n_in=len(ins)),
        grid=(r // tm,),
        in_specs=[_row_spec(tm, a.shape[1]) for a in ins] + [_const_spec(w.shape) for w in ws]
        + [_row_spec(tm, D), _const_spec((1, D)), _mod_spec(mods, layer, 2, tpg)],
        out_specs=_row_spec(tm, D),
        out_shape=jax.ShapeDtypeStruct((r, D), F32),
        compiler_params=_cparams("parallel"),
        name="outproj",
    )(*ins, *ws, x, gp, mods)


def _ffn_body(x_ref, g_ref, sh_ref, sc_ref, gate_ref, gp_ref, w1_ref, w2_ref, o_ref, h_scr, acc_scr):
    j = pl.program_id(1)

    @pl.when(j == 0)
    def _():
        h_scr[...] = _modnorm(x_ref[...], g_ref[...], sh_ref[...], sc_ref[...]).astype(BF16)
        acc_scr[...] = jnp.zeros_like(acc_scr)

    t = jnp.maximum(_mm(h_scr[...], w1_ref[...]), 0.0)
    acc_scr[...] += _mm(t * t, w2_ref[...])

    @pl.when(j == pl.num_programs(1) - 1)
    def _():
        o_ref[...] = x_ref[...] + gate_ref[...] * (_rms(acc_scr[...]) * gp_ref[...])


def _ffn(x, g, gp, mods, layer, tm, tf, w1, w2):
    r = x.shape[0]
    tpg = (r // mods.shape[1]) // tm
    return pl.pallas_call(
        _ffn_body,
        grid=(r // tm, D_FF // tf),
        in_specs=[_row_spec(tm, D), _const_spec((1, D)), _mod_spec(mods, layer, 3, tpg),
                  _mod_spec(mods, layer, 4, tpg), _mod_spec(mods, layer, 5, tpg), _const_spec((1, D)),
                  pl.BlockSpec((D, tf), lambda i, j: (0, j)), pl.BlockSpec((tf, D), lambda i, j: (j, 0))],
        out_specs=_row_spec(tm, D),
        out_shape=jax.ShapeDtypeStruct((r, D), F32),
        scratch_shapes=[pltpu.VMEM((tm, D), BF16), pltpu.VMEM((tm, D), F32)],
        compiler_params=_cparams("parallel", "arbitrary"),
        name="ffn",
    )(x, g, mods, mods, mods, gp, w1, w2)


def _seg_cumsum(x, seg):
    row = lax.broadcasted_iota(jnp.int32, (x.shape[0], 1), 0) % seg
    s = 1
    while s < seg:
        x = x + jnp.where(row >= s, pltpu.roll(x, s, 0), 0.0)
        s *= 2
    return x


def _rwkv_body(pb_ref, prev_ref, s0_ref, mu_ref, prm_ref, lora_ref, hsum_ref, out_ref, s_ref, last_scr,
               *, nb, tt, c):
    @pl.when(pl.program_id(1) == 0)
    def _():
        s_ref[...] = s0_ref[...]
        last_scr[...] = prev_ref[...]

    mu = mu_ref[...]
    w0, a0, k_k, k_a, r_k, ln_g, ln_b = (prm_ref[i:i + 1, :] for i in range(7))
    hsum = hsum_ref[...]
    headsum = lambda t: _mm(t, hsum, passes=2)
    row = lax.broadcasted_iota(jnp.int32, (tt, 1), 0)
    ii = lax.broadcasted_iota(jnp.int32, (c, 2 * c), 0)
    jj = lax.broadcasted_iota(jnp.int32, (c, 2 * c), 1) % c
    strict, incl = jj < ii, jj <= ii
    eye = jnp.where(lax.broadcasted_iota(jnp.int32, (c, c), 0) == lax.broadcasted_iota(jnp.int32, (c, c), 1), 1.0, 0.0).astype(F32)
    zeros_c = jnp.zeros((c, HEAD), BF16)
    n_chunk = tt // c
    levels = int(math.log2(c)) - 1

    seqs = []
    for s in range(nb):
        x = pb_ref[s]
        shifted = jnp.where(row == 0, last_scr[s], pltpu.roll(x, 1, 0))
        last_scr[s] = x[tt - 1:tt, :]
        xm = x + mu * (shifted - x)
        r, k, v = xm[:, :B_WIDTH], xm[:, B_WIDTH:2 * B_WIDTH], xm[:, 2 * B_WIDTH:3 * B_WIDTH]
        lo = xm[:, 3 * B_WIDTH:]
        z = -(w0 + _mm(jnp.tanh(lo), lora_ref[0]))
        softplus = jnp.maximum(z, 0.0) + jnp.log1p(jnp.exp(-jnp.abs(z)))
        lw = -jnp.exp(-softplus - 0.5)
        a = jax.nn.sigmoid(a0 + _mm(lo, lora_ref[1]))
        g = _mm(jax.nn.sigmoid(lo), lora_ref[2])
        kk = k * k_k
        kk = kk * lax.rsqrt(headsum(kk * kk) + 1e-12)
        k2 = k * (1.0 + (a - 1.0) * k_a)
        lp = _seg_cumsum(lw, c)
        e_pos, e_neg = jnp.exp(lp), jnp.exp(-lp)
        seqs.append(dict(
            r=r, k2=k2, v=v, g=g, e_pos=e_pos,
            a_t=-kk * jnp.exp(lp - lw), b_t=kk * a * e_neg, k_t=k2 * e_neg, r_t=r * e_pos, v_b=v))

    def piece(s, name, ci, h):
        return seqs[s][name][ci * c:(ci + 1) * c, h * HEAD:(h + 1) * HEAD].astype(BF16)

    chains = [(s, h) for s in range(nb) for h in range(B_HEADS)]
    units = [(ci, s, h) for ci in range(n_chunk) for (s, h) in chains]

    a_u = {u: piece(u[1], "a_t", u[0], u[2]) for u in units}
    r_u = {u: piece(u[1], "r_t", u[0], u[2]) for u in units}
    v_u = {u: piece(u[1], "v_b", u[0], u[2]) for u in units}
    ar_u = {u: jnp.concatenate([a_u[u], r_u[u]], axis=0) for u in units}
    bk_u = {u: jnp.concatenate([piece(u[1], "b_t", u[0], u[2]), piece(u[1], "k_t", u[0], u[2])], axis=0) for u in units}
    zv_u = {u: jnp.concatenate([zeros_c, v_u[u]], axis=0) for u in units}
    g12 = {u: _mm(ar_u[u], bk_u[u], NT) for u in units}
    g1 = {u: jnp.where(strict, g12[u][:c], 0.0).astype(BF16) for u in units}
    g2 = {u: jnp.where(incl, g12[u][c:], 0.0).astype(BF16) for u in units}
    n_u = {u: g1[u][:, :c] for u in units}
    inv = {u: eye + n_u[u].astype(F32) for u in units}
    pw = {u: _mm(n_u[u], n_u[u]) for u in units}
    for level in range(levels):
        if level + 1 < levels:
            both = {u: _mm(jnp.concatenate([pw[u], inv[u]], axis=0), pw[u]) for u in units}
            pw = {u: both[u][:c] for u in units}
            inv = {u: inv[u] + both[u][c:] for u in units}
        else:
            inv = {u: inv[u] + _mm(inv[u], pw[u]) for u in units}
    xn = {u: _mm(jnp.concatenate(_split(inv[u]), axis=0), n_u[u]) for u in units}
    res = {u: eye - inv[u] + (xn[u][:c] + xn[u][c:]) for u in units}
    inv = {u: inv[u] + _mm(res[u], inv[u]) for u in units}

    state = {ch: s_ref[ch[0], ch[1]] for ch in chains}
    o_u = {}
    for ci in range(n_chunk):
        us = [(ci, s, h) for (s, h) in chains]
        ars = {u: _mm(ar_u[u], state[u[1:]], NT) for u in us}
        rhs = {u: ars[u][:c] + _mm(g1[u], zv_u[u]) for u in us}
        uu = {u: _mm(inv[u], rhs[u]) for u in us}
        uv = {u: jnp.concatenate([uu[u].astype(BF16), v_u[u]], axis=0) for u in us}
        for u in us:
            o_u[u] = ars[u][c:] + _mm(g2[u], uv[u])
        for u in us:
            s, h = u[1:]
            p_end = seqs[s]["e_pos"][(ci + 1) * c - 1:(ci + 1) * c, h * HEAD:(h + 1) * HEAD]
            state[(s, h)] = (state[(s, h)] + _mm(uv[u], bk_u[u], TN)) * p_end
    for (s, h) in chains:
        s_ref[s, h] = state[(s, h)]

    for s in range(nb):
        q = seqs[s]
        o_rows = [jnp.concatenate([o_u[(ci, s, h)] for h in range(B_HEADS)], axis=1) for ci in range(n_chunk)]
        o = o_rows[0] if n_chunk == 1 else jnp.concatenate(o_rows, axis=0)
        inv_n = 1.0 / HEAD
        oc = o - headsum(o) * inv_n
        on = oc * lax.rsqrt(headsum(oc * oc) * inv_n + GN_EPS) * ln_g + ln_b
        bonus = headsum(q["r"] * q["k2"] * r_k) * q["v"]
        out_ref[s] = (on + bonus) * q["g"]


def _rwkv(pb, prev, s0, mu, prm, lora, hsum, nb, tt, c):
    n, t, _ = pb.shape
    return pl.pallas_call(
        functools.partial(_rwkv_body, nb=nb, tt=tt, c=c),
        grid=(n // nb, t // tt),
        in_specs=[pl.BlockSpec((nb, tt, B_COLS), lambda i, j: (i, j, 0)),
                  pl.BlockSpec((nb, 1, B_COLS), lambda i, j: (i, 0, 0)),
                  pl.BlockSpec((nb, B_HEADS, HEAD, HEAD), lambda i, j: (i, 0, 0, 0)),
                  _const_spec(mu.shape), _const_spec(prm.shape), _const_spec(lora.shape), _const_spec(hsum.shape)],
        out_specs=[pl.BlockSpec((nb, tt, B_WIDTH), lambda i, j: (i, j, 0)),
                   pl.BlockSpec((nb, B_HEADS, HEAD, HEAD), lambda i, j: (i, 0, 0, 0))],
        out_shape=[jax.ShapeDtypeStruct((n, t, B_WIDTH), F32), jax.ShapeDtypeStruct(s0.shape, F32)],
        scratch_shapes=[pltpu.VMEM((nb, 1, B_COLS), F32)],
        compiler_params=_cparams("parallel", "arbitrary"),
        name="rwkv7",
    )(pb, prev, s0, mu, prm, lora, hsum)


def _band_attn_body(q_ref, k_ref, v_ref, acc_ref, st_ref, *, nblk, units):
    nk = BAND if nblk == 1 else 2 * BAND
    qi = lax.broadcasted_iota(jnp.int32, (BAND, nk), 0)
    kc = lax.broadcasted_iota(jnp.int32, (BAND, nk), 1)
    lane = lax.broadcasted_iota(jnp.int32, (BAND, LANES), 1)
    lane_k = lax.broadcasted_iota(jnp.int32, (nk, LANES), 1)
    group = 8 if nblk == 1 else 4
    for j in range(units):
        if nblk == 1:
            rows = slice(None)
            q_at, acc_at, st_at = q_ref.at[j], acc_ref.at[j], st_ref.at[j]
            keys = lambda ref, lanes: ref[j, :, lanes]
            mask = kc <= qi
        else:
            blk = pl.program_id(2) * units + j
            rows = slice(j * BAND, (j + 1) * BAND)
            q_at, acc_at, st_at = q_ref, acc_ref, st_ref
            cur = pl.multiple_of(blk * BAND, BAND)
            prev = pl.multiple_of(jnp.maximum(blk - 1, 0) * BAND, BAND)
            keys = lambda ref, lanes: jnp.concatenate([ref[pl.ds(prev, BAND), lanes], ref[pl.ds(cur, BAND), lanes]], axis=0)
            mask = (kc >= qi) & (kc <= qi + BAND) & (kc >= jnp.where(blk > 0, 0, BAND))
        stats = jnp.zeros((BAND, LANES), F32)
        for g0 in range(0, C_HEADS // 2, group):
            pairs = range(g0, g0 + group)
            scores = []
            for pair in pairs:
                lanes = slice(pair * LANES, (pair + 1) * LANES)
                q2, k2 = q_at[rows, lanes], keys(k_ref, lanes)
                for half in range(2):
                    in_head = (lane >= HEAD) if half else (lane < HEAD)
                    scores.append(_mm(jnp.where(in_head, q2, jnp.zeros_like(q2)), k2, NT))
            probs = []
            for i, s in enumerate(scores):
                s = jnp.where(mask, s, NEG)
                m = jnp.max(s, axis=-1, keepdims=True)
                probs.append(jnp.exp(s - m).astype(BF16))
                stats = jnp.where(lane == (i % 2) * (C_HEADS // 2) + g0 + i // 2, m, stats)
            for i, pair in enumerate(pairs):
                lanes = slice(pair * LANES, (pair + 1) * LANES)
                v2 = keys(v_ref, lanes)
                lo = _mm(probs[2 * i], jnp.where(lane_k < HEAD, v2, jnp.ones_like(v2)))
                hi = _mm(probs[2 * i + 1], jnp.where(lane_k >= HEAD, v2, jnp.ones_like(v2)))
                acc_at[rows, lanes] = jnp.where(lane < HEAD, lo, hi).astype(acc_ref.dtype)
                stats = jnp.where(lane == DEN_EVEN + pair, lo, stats)
                stats = jnp.where(lane == DEN_ODD + pair, hi, stats)
        st_at[rows, :] = stats


def _band_attn(q, k, v, units=2):
    nseq, dil, sub, _ = q.shape
    nblk = sub // BAND
    if nblk == 1:
        grid = (nseq, dil // units, 1)
        blk = lambda width: pl.BlockSpec((None, units, sub, width), lambda b, r, n: (b, r, 0, 0))
        cls = blk(D)
    else:
        grid = (nseq, dil, nblk // units)
        blk = lambda width: pl.BlockSpec((None, None, units * BAND, width), lambda b, r, n: (b, r, n, 0))
        cls = pl.BlockSpec((None, None, sub, D), lambda b, r, n: (b, r, 0, 0))
    return pl.pallas_call(
        functools.partial(_band_attn_body, nblk=nblk, units=units),
        grid=grid,
        in_specs=[blk(D), cls, cls],
        out_specs=[blk(D), blk(LANES)],
        out_shape=[jax.ShapeDtypeStruct(q.shape, q.dtype), jax.ShapeDtypeStruct((nseq, dil, sub, LANES), F32)],
        compiler_params=_cparams("parallel", "parallel", "arbitrary"),
        name=f"band_attn_d{dil}",
    )(q, k, v)


def _mix_body(a1_ref, a4_ref, a16_ref, s1_ref, s4_ref, s16_ref, e_ref, w_ref, x_ref, gp_ref, gate_ref, o_ref,
              slab4, slab16, st4, st16, o_scr):
    tm = x_ref.shape[0]
    ngrp = D // LANES
    for dil, a_ref, s_ref, slab, st in ((4, a4_ref, s4_ref, slab4, st4), (16, a16_ref, s16_ref, slab16, st16)):
        for r in range(dil):
            rows = pl.ds(r, tm // dil, stride=dil)
            st[rows, :] = s_ref[r]
            for grp in range(ngrp):
                slab[grp, rows, :] = a_ref[r, :, grp * LANES:(grp + 1) * LANES].astype(F32)
    sts = [s1_ref[...], st4[...], st16[...]]
    lane = lax.broadcasted_iota(jnp.int32, sts[0].shape, 1)
    top = jnp.maximum(jnp.maximum(sts[0], sts[1]), sts[2])
    es = [jnp.exp(st - top) for st in sts]
    dens = [jnp.where(lane < C_HEADS // 2, pltpu.roll(st, LANES - DEN_EVEN, 1),
                      pltpu.roll(st, LANES - DEN_ODD + C_HEADS // 2, 1)) for st in sts]
    total = es[0] * dens[0] + es[1] * dens[1] + es[2] * dens[2]
    coefs = [_mm(jnp.where(lane < C_HEADS, e / total, 0.0), e_ref[...], passes=2) for e in es]
    for grp in range(ngrp):
        lanes = slice(grp * LANES, (grp + 1) * LANES)
        o = (coefs[0][:, lanes] * a1_ref[:, lanes].astype(F32) + coefs[1][:, lanes] * slab4[grp]
             + coefs[2][:, lanes] * slab16[grp])
        o_scr[:, lanes] = o.astype(BF16)
    y = _mm(o_scr[...], w_ref[...])
    o_ref[...] = x_ref[...] + gate_ref[...] * (_rms(y) * gp_ref[...])


def _mix_outproj(accs, sts, expand, w, x, gp, mods, layer, tm):
    r = x.shape[0]
    nseq = accs[1].shape[0]
    tps = (r // nseq) // tm
    cls_spec = lambda dil, width: pl.BlockSpec((None, dil, tm // dil, width), lambda i: (i // tps, 0, i % tps, 0))
    return pl.pallas_call(
        _mix_body,
        grid=(r // tm,),
        in_specs=[_row_spec(tm, D), cls_spec(4, D), cls_spec(16, D), _row_spec(tm, LANES), cls_spec(4, LANES),
                  cls_spec(16, LANES), _const_spec(expand.shape), _const_spec(w.shape), _row_spec(tm, D),
                  _const_spec((1, D)), _mod_spec(mods, layer, 2, tps)],
        out_specs=_row_spec(tm, D),
        out_shape=jax.ShapeDtypeStruct((r, D), F32),
        scratch_shapes=[pltpu.VMEM((D // LANES, tm, LANES), F32), pltpu.VMEM((D // LANES, tm, LANES), F32),
                        pltpu.VMEM((tm, LANES), F32), pltpu.VMEM((tm, LANES), F32), pltpu.VMEM((tm, D), BF16)],
        compiler_params=_cparams("parallel"),
        name="attn_mix_outproj",
    )(*accs, *sts, expand, w, x, gp, mods)


def _attn_sample_body(q_ref, kn_ref, vn_ref, kc_ref, vc_ref, o_ref, *, t_len, window):
    tq = lax.broadcasted_iota(jnp.int32, (t_len, window), 0)
    back = window + tq - lax.broadcasted_iota(jnp.int32, (t_len, window), 1)
    cnt = (jnp.where(back <= BAND, 1.0, 0.0) + jnp.where((back <= 4 * BAND) & ((back & 3) == 0), 1.0, 0.0)
           + jnp.where((back & 15) == 0, 1.0, 0.0))
    tn = lax.broadcasted_iota(jnp.int32, (t_len, t_len), 0)
    dn = tn - lax.broadcasted_iota(jnp.int32, (t_len, t_len), 1)
    mult = jnp.where(dn == 0, 3.0, 0.0) + jnp.where(dn > 0, 1.0, 0.0) + jnp.where(dn == 4, 1.0, 0.0)
    lane = lax.broadcasted_iota(jnp.int32, (t_len, LANES), 1)
    for pair in range(q_ref.shape[1] // LANES):
        lanes = slice(pair * LANES, (pair + 1) * LANES)
        q2, kn, vn = q_ref[:, lanes], kn_ref[:, lanes].astype(BF16), vn_ref[:, lanes].astype(BF16)
        kc, vc = kc_ref[lanes, :].astype(BF16), vc_ref[lanes, :].astype(BF16)
        outs = []
        for half in range(2):
            qh = jnp.where((lane >= HEAD) if half else (lane < HEAD), q2, 0.0).astype(BF16)
            s_c = jnp.where(cnt > 0.0, _mm(qh, kc), NEG)
            s_n = jnp.where(mult > 0.0, _mm(qh, kn, NT), NEG)
            top = jnp.maximum(jnp.max(s_c, axis=-1, keepdims=True), jnp.max(s_n, axis=-1, keepdims=True))
            p_c = jnp.exp(s_c - top) * cnt
            p_n = jnp.exp(s_n - top) * mult
            den = jnp.sum(p_c, axis=-1, keepdims=True) + jnp.sum(p_n, axis=-1, keepdims=True)
            outs.append((_mm(p_c, vc, NT) + _mm(p_n, vn)) / den)
        o_ref[:, lanes] = jnp.where(lane < HEAD, outs[0], outs[1])


def _attn_sample(q, kn, vn, cache_k, cache_v, layer, nseq):
    window = cache_k.shape[-1]
    t_len = q.shape[0] // nseq
    assert window == 16 * BAND and t_len == 8
    width = 2 * LANES
    new = pl.BlockSpec((t_len, width), lambda b, g: (b, g))
    cached = pl.BlockSpec((None, None, width, window), lambda b, g: (layer, b, g, 0))
    return pl.pallas_call(
        functools.partial(_attn_sample_body, t_len=t_len, window=window),
        grid=(nseq, D // width),
        in_specs=[new, new, new, cached, cached],
        out_specs=new,
        out_shape=jax.ShapeDtypeStruct(q.shape, F32),
        compiler_params=_cparams("parallel", "parallel"),
        name="attn_sample",
    )(q, kn, vn, cache_k, cache_v)


def kernel(x_prompt, x_sample, state_wkv, state_shift, cache_k, cache_v, c_prompt, c_sample, ada_w, ada_b, norm_mix_pre, norm_mix_post, norm_ffn_pre, norm_ffn_post, ffn_w1, ffn_w2, ev_w_in, ev_w_out, gm_ln_g, gm_ln_b, gm_ws, gm_bs, rw_mu, rw_w0, rw_w2, rw_a0, rw_a2, rw_g2, rw_kk, rw_ka, rw_rk, rw_lnx_g, rw_lnx_b, od_w_qkv, od_w_out):
    n_p, s_len, _ = x_prompt.shape
    n_s, t_len, _ = x_sample.shape
    r_p, r_s = n_p * s_len, n_s * t_len
    row = lambda t: t.reshape(1, -1).astype(F32)

    mods = _ada(jnp.concatenate([c_prompt, c_sample], axis=0).astype(F32), ada_w, ada_b)
    mods_p = mods[:, :n_p].reshape(DEPTH, n_p, 1, 6 * D)
    mods_s = jnp.repeat(mods[:, n_p:], t_len, axis=1).reshape(DEPTH, 1, r_s, 6 * D)

    xp = x_prompt.astype(F32).reshape(r_p, D)
    xs = x_sample.astype(F32).reshape(r_s, D)

    hsum = jnp.kron(jnp.eye(B_HEADS, dtype=F32), jnp.ones((HEAD, HEAD), F32)).astype(BF16)
    tile_head = 2 * (jnp.arange(LANES) % (C_HEADS // 2)) + jnp.arange(LANES) // (C_HEADS // 2)
    expand = ((tile_head[:, None] == jnp.arange(D)[None, :] // HEAD) & (jnp.arange(LANES)[:, None] < C_HEADS)).astype(BF16)
    tabs_p = _rope_tables(jnp.arange(s_len))
    tabs_s = _rope_tables(jnp.tile(PAST_LEN + jnp.arange(t_len), n_s))
    zeros_prev = jnp.zeros((n_p, 1, B_COLS), F32)
    zeros_state = jnp.zeros((n_p, B_HEADS, HEAD, HEAD), F32)
    reps = CHUNK // t_len
    feat_major = lambda c: jnp.transpose(c.astype(F32), (0, 1, 3, 4, 2)).reshape(c.shape[0], c.shape[1], D, c.shape[2])
    cache_kt, cache_vt = feat_major(cache_k), feat_major(cache_v)
    tok_major = lambda t: jnp.transpose(t.reshape(n_p, C_HEADS, HEAD, s_len), (0, 3, 1, 2))

    tm_p = 512
    outs = {k: [] for k in ("wkv_p", "shift_p", "k_p", "v_p", "wkv_s", "shift_s", "k_s", "v_s", "gv_s")}
    for l in range(DEPTH):
        g_pre, g_post = row(norm_mix_pre[l]), row(norm_mix_post[l])
        if l % 2 == 0:
            e = l // 2
            w_in = ev_w_in[e].astype(BF16)
            w_out = ev_w_out[e].astype(BF16)
            w_oa, w_ob = w_out[:A_WIDTH], w_out[A_WIDTH:]
            ln_g, ln_b = row(gm_ln_g[e]), row(gm_ln_b[e])
            ws = gm_ws[e].astype(F32)
            bias_p = jnp.repeat(gm_bs[e].T, A_WIDTH // A_GROUPS, axis=1)
            ws_s = jax.vmap(lambda m: jnp.kron(jnp.eye(reps, dtype=F32), m))(ws[:, :t_len, :t_len])
            bias_s = jnp.tile(bias_p[:t_len], (reps, 1))
            mu = row(rw_mu[e])
            zpad = jnp.zeros((7 * B_WIDTH,), F32)
            prm = jnp.concatenate([rw_w0[e], rw_a0[e], rw_kk[e], rw_ka[e], rw_rk[e].reshape(-1), rw_lnx_g[e],
                                   rw_lnx_b[e], zpad[:B_WIDTH]]).reshape(8, B_WIDTH).astype(F32)
            zl = lambda a, b: jnp.zeros((a, b), F32)
            lora = jnp.stack([
                jnp.concatenate([rw_w2[e], zl(96, B_WIDTH)], axis=0),
                jnp.concatenate([zl(32, B_WIDTH), rw_a2[e], zl(64, B_WIDTH)], axis=0),
                jnp.concatenate([zl(64, B_WIDTH), rw_g2[e]], axis=0)]).astype(BF16)

            oa_p, pb_p = _inproj(xp, g_pre, mods_p, l, tm_p, w_in, ln_g, ln_b, ws, bias_p, False)
            pb_p3 = pb_p.reshape(n_p, s_len, B_COLS)
            ob_p, st_p = _rwkv(pb_p3, zeros_prev, zeros_state, mu, prm, lora, hsum, 2, 128, 64)
            xp = _outproj([oa_p, ob_p.reshape(r_p, B_WIDTH)], [w_oa, w_ob], xp, g_post, mods_p, l, tm_p)
            outs["wkv_p"].append(st_p)
            outs["shift_p"].append(pb_p3[:, -1])

            oa_s, pb_s, gv = _inproj(xs, g_pre, mods_s, l, r_s, w_in, ln_g, ln_b, ws_s, bias_s, True)
            pb_s3 = pb_s.reshape(n_s, t_len, B_COLS)
            ob_s, st_s = _rwkv(pb_s3, state_shift[e].astype(F32).reshape(n_s, 1, B_COLS), state_wkv[e].astype(F32),
                               mu, prm, lora, hsum, 8, t_len, t_len)
            xs = _outproj([oa_s, ob_s.reshape(r_s, B_WIDTH)], [w_oa, w_ob], xs, g_post, mods_s, l, r_s)
            outs["wkv_s"].append(st_s)
            outs["shift_s"].append(pb_s3[:, -1])
            outs["gv_s"].append(gv.reshape(n_s, t_len, A_WIDTH))
        else:
            o = l // 2
            w_qkv = od_w_qkv[o].astype(BF16)
            w_out = od_w_out[o].astype(BF16)
            q1, k1, v1, q4, k4, v4, q16, k16, v16, kf, vf = _qkv_classes(xp, g_pre, mods_p, l, 256, w_qkv, tabs_p, n_p)
            nat = lambda t: t.reshape(n_p, 1, s_len, D)
            res = [_band_attn(*qkv) for qkv in ((nat(q1), nat(k1), nat(v1)), (q4, k4, v4), (q16, k16, v16))]
            accs = [res[0][0].reshape(r_p, D), res[1][0], res[2][0]]
            sts = [res[0][1].reshape(r_p, LANES), res[1][1], res[2][1]]
            xp = _mix_outproj(accs, sts, expand, w_out, xp, g_post, mods_p, l, 256)
            outs["k_p"].append(tok_major(kf))
            outs["v_p"].append(tok_major(vf))

            qs, kfs, vfs = _qkv(xs, g_pre, mods_s, l, r_s, w_qkv, tabs_s)
            heads = lambda t: t.reshape(n_s, t_len, C_HEADS, HEAD)
            att = _attn_sample(qs, kfs, vfs, cache_kt, cache_vt, o, n_s)
            xs = _outproj([att], [w_out], xs, g_post, mods_s, l, r_s)
            outs["k_s"].append(heads(kfs))
            outs["v_s"].append(heads(vfs))
        w1, w2 = ffn_w1[l].astype(BF16), ffn_w2[l].astype(BF16)
        g_fpre, g_fpost = row(norm_ffn_pre[l]), row(norm_ffn_post[l])
        xp = _ffn(xp, g_fpre, g_fpost, mods_p, l, 1024, 1024, w1, w2)
        xs = _ffn(xs, g_fpre, g_fpost, mods_s, l, r_s, 512, w1, w2)

    st = lambda name: jnp.stack(outs[name])
    return (xp.reshape(n_p, s_len, D), xs.reshape(n_s, t_len, D), st("wkv_p"), st("shift_p"), st("k_p"),
            st("v_p"), st("wkv_s"), st("shift_s"), st("k_s"), st("v_s"), st("gv_s"))
```

```python
import functools
import math

import jax
import jax.numpy as jnp
from jax import lax
from jax.experimental import pallas as pl
from jax.experimental.pallas import tpu as pltpu

F32 = jnp.float32
BF16 = jnp.bfloat16

D = 1024
DEPTH = 4
D_FF = 4 * D
A_WIDTH = 512
A_GROUPS = 4
CHUNK = 128
B_WIDTH = 512
B_HEADS = 8
HEAD = 64
LORA = 128
B_COLS = 3 * B_WIDTH + LORA
C_HEADS = 16
ROT_DIM = 16
ROPE_THETA = 500000.0
DILATIONS = (1, 4, 16)
BAND = 128
PAST_LEN = 8192
NORM_EPS = 1e-6
LN_EPS = 1e-5
GN_EPS = 64e-5
NEG = -1e30

LANES = 128
DEN_EVEN, DEN_ODD = 64, 16
VMEM_LIMIT = 52 * 1024 * 1024


def _cparams(*sem):
    return pltpu.CompilerParams(dimension_semantics=sem, vmem_limit_bytes=VMEM_LIMIT)


def _split(x):
    hi = x.astype(BF16)
    return hi, (x - hi.astype(F32)).astype(BF16)


def _mm(a, b, dims=((1,), (0,)), passes=1):
    dn = (dims, ((), ()))
    dot = lambda p, q: lax.dot_general(p, q, dn, preferred_element_type=F32)
    if passes == 1:
        return dot(a.astype(BF16), b.astype(BF16))
    ah, al = _split(a)
    return dot(ah, b) + dot(al, b)


NT = ((1,), (1,))
TN = ((0,), (0,))


def _rms(x):
    return x * lax.rsqrt(jnp.mean(x * x, axis=-1, keepdims=True) + NORM_EPS)


def _mod_spec(mods, layer, chunk, tiles_per_group):
    return pl.BlockSpec((None, None, mods.shape[2], D), lambda i, *_: (layer, i // tiles_per_group, 0, chunk))


def _row_spec(tm, n):
    return pl.BlockSpec((tm, n), lambda i, *_: (i, 0))


def _const_spec(shape):
    nd = len(shape)
    return pl.BlockSpec(shape, lambda *_: (0,) * nd)


def _ada_body(c_ref, w_ref, b_ref, o_ref):
    c = c_ref[...]
    o_ref[...] = _mm(c * jax.nn.sigmoid(c), w_ref[...]) + b_ref[...]


def _ada(c_all, ada_w, ada_b):
    n = c_all.shape[0]
    tn = 1536
    return pl.pallas_call(
        _ada_body,
        grid=(DEPTH, 6 * D // tn),
        in_specs=[pl.BlockSpec((n, D), lambda l, j: (0, 0)),
                  pl.BlockSpec((None, D, tn), lambda l, j: (l, 0, j)),
                  pl.BlockSpec((None, 1, tn), lambda l, j: (l, 0, j))],
        out_specs=pl.BlockSpec((None, n, tn), lambda l, j: (l, 0, j)),
        out_shape=jax.ShapeDtypeStruct((DEPTH, n, 6 * D), F32),
        compiler_params=_cparams("parallel", "parallel"),
        name="ada",
    )(c_all, ada_w, ada_b.reshape(DEPTH, 1, 6 * D))


def _modnorm(x, g, sh, sc):
    return _rms(x) * g * (1.0 + sc) + sh


def _gelu(x):
    return 0.5 * x * (1.0 + lax.erf(x * (2.0 ** -0.5)))


def _inproj_body(x_ref, g_ref, sh_ref, sc_ref, w_ref, lng_ref, lnb_ref, ws_ref, bias_ref, oa_ref, pb_ref, *v_refs):
    h = _modnorm(x_ref[...], g_ref[...], sh_ref[...], sc_ref[...]).astype(BF16)
    pb_ref[...] = _mm(h, w_ref[:, 2 * A_WIDTH:])
    u = _gelu(_mm(h, w_ref[:, :A_WIDTH]))
    v = _gelu(_mm(h, w_ref[:, A_WIDTH:2 * A_WIDTH]))
    vc = v - jnp.mean(v, axis=-1, keepdims=True)
    var = jnp.mean(vc * vc, axis=-1, keepdims=True)
    vn = vc * lax.rsqrt(var + LN_EPS) * lng_ref[...] + lnb_ref[...]
    if v_refs:
        v_refs[0][...] = vn
    ii = lax.broadcasted_iota(jnp.int32, (CHUNK, CHUNK), 0)
    jj = lax.broadcasted_iota(jnp.int32, (CHUNK, CHUNK), 1)
    cg = A_WIDTH // A_GROUPS
    for grp in range(A_GROUPS):
        lanes = slice(grp * cg, (grp + 1) * cg)
        wc = jnp.where(jj <= ii, ws_ref[grp], 0.0).astype(BF16)
        for ci in range(x_ref.shape[0] // CHUNK):
            rows = slice(ci * CHUNK, (ci + 1) * CHUNK)
            z = _mm(wc, vn[rows, lanes]) + bias_ref[:, lanes]
            oa_ref[rows, lanes] = (u[rows, lanes] * z).astype(oa_ref.dtype)


def _inproj(x, g, mods, layer, tm, w, ln_g, ln_b, ws, bias, emit_v):
    r = x.shape[0]
    tpg = (r // mods.shape[1]) // tm
    n = w.shape[1]
    half = lambda dtype: jax.ShapeDtypeStruct((r, A_WIDTH), dtype)
    return pl.pallas_call(
        _inproj_body,
        grid=(r // tm,),
        in_specs=[_row_spec(tm, D), _const_spec((1, D)), _mod_spec(mods, layer, 0, tpg),
                  _mod_spec(mods, layer, 1, tpg), _const_spec((D, n)), _const_spec((1, A_WIDTH)),
                  _const_spec((1, A_WIDTH)), _const_spec(ws.shape), _const_spec(bias.shape)],
        out_specs=[_row_spec(tm, A_WIDTH), _row_spec(tm, B_COLS)] + ([_row_spec(tm, A_WIDTH)] if emit_v else []),
        out_shape=[half(BF16), jax.ShapeDtypeStruct((r, B_COLS), F32)] + ([half(F32)] if emit_v else []),
        compiler_params=_cparams("parallel"),
        name="even_inproj_gmlp",
    )(x, g, mods, mods, w, ln_g, ln_b, ws, bias)


def _rope(seg, cos, s_lo, s_hi):
    return seg * cos + pltpu.roll(seg, LANES - ROT_DIM // 2, 1) * s_lo + pltpu.roll(seg, ROT_DIM // 2, 1) * s_hi


def _qkv_groups(x_ref, g_ref, sh_ref, sc_ref, w_ref, cos_ref, slo_ref, shi_ref):
    h = _modnorm(x_ref[...], g_ref[...], sh_ref[...], sc_ref[...]).astype(BF16)
    cos, s_lo, s_hi = cos_ref[...], slo_ref[...], shi_ref[...]
    wide = 2 * LANES
    for pair in range(D // wide):
        q2, k2, v2 = (_mm(h, w_ref[:, part * D + pair * wide:part * D + (pair + 1) * wide]) for part in range(3))
        for half in range(2):
            part = slice(half * LANES, (half + 1) * LANES)
            grp = 2 * pair + half
            yield (slice(grp * LANES, (grp + 1) * LANES), _rope(q2[:, part], cos, s_lo, s_hi) * HEAD ** -0.5,
                   _rope(k2[:, part], cos, s_lo, s_hi), v2[:, part])


def _qkv_body(*refs):
    q_ref, kf_ref, vf_ref = refs[8:]
    for lanes, q, k, v in _qkv_groups(*refs[:8]):
        q_ref[:, lanes] = q
        kf_ref[:, lanes] = k
        vf_ref[:, lanes] = v


def _qkv_classes_body(*refs, tps):
    q4, k4, v4, q16, k16, v16, kf_ref, vf_ref, acc_ref, st_ref, slab, slab4, q_scr, k_scr, v_scr = refs[8:]
    tm = q_scr.shape[0]
    first_tile = pl.program_id(0) % tps == 0

    @pl.when(first_tile)
    def _():
        k_scr[:BAND, :] = jnp.zeros((BAND, D), BF16)
        v_scr[:BAND, :] = jnp.zeros((BAND, D), BF16)

    @pl.when(jnp.logical_not(first_tile))
    def _():
        k_scr[:BAND, :] = k_scr[tm:, :]
        v_scr[:BAND, :] = v_scr[tm:, :]

    for lanes, q, k, v in _qkv_groups(*refs[:8]):
        kf_ref[lanes, :] = k.T
        vf_ref[lanes, :] = v.T
        q_scr[:, lanes] = q.astype(BF16)
        k_scr[BAND:, lanes] = k.astype(BF16)
        v_scr[BAND:, lanes] = v.astype(BF16)
        for i, (val, o4, o16) in enumerate(((q, q4, q16), (k, k4, k16), (v, v4, v16))):
            slab[i] = val
            for r4 in range(4):
                cls = slab[i, pl.ds(r4, tm // 4, stride=4), :]
                o4[r4, :, lanes] = cls.astype(BF16)
                slab4[i, r4 * (tm // 4):(r4 + 1) * (tm // 4), :] = cls
            for r16 in range(16):
                start = (r16 % 4) * (tm // 4) + r16 // 4
                o16[r16, :, lanes] = slab4[i, pl.ds(start, tm // 16, stride=4), :].astype(BF16)
    for j in range(tm // BAND):
        keys = lambda ref, lanes: ref[j * BAND:(j + 2) * BAND, lanes]
        mask = _band_mask(first_tile) if j == 0 else _band_mask(False)
        _band_block(q_scr, slice(j * BAND, (j + 1) * BAND), keys, k_scr, v_scr, mask, acc_ref, st_ref, 4)


def _rope_tables(pos):
    half = ROT_DIM // 2
    inv = ROPE_THETA ** (-jnp.arange(half, dtype=F32) * (2.0 / ROT_DIM))
    ang = pos.astype(F32)[:, None] * inv[None, :]
    cos, sin = jnp.cos(ang), jnp.sin(ang)
    t = pos.shape[0]
    one, zero, z8 = jnp.ones((t, HEAD - ROT_DIM), F32), jnp.zeros((t, HEAD - ROT_DIM), F32), jnp.zeros((t, half), F32)
    tab = lambda *parts: jnp.tile(jnp.concatenate(parts, axis=1), (1, LANES // HEAD))
    return tab(cos, cos, one), tab(-sin, z8, zero), tab(z8, sin, zero)


def _qkv_in_specs(mods, layer, tm, tpg, tables_per_seq):
    tab_spec = pl.BlockSpec((tm, LANES), lambda i: (i % tables_per_seq, 0))
    return [_row_spec(tm, D), _const_spec((1, D)), _mod_spec(mods, layer, 0, tpg), _mod_spec(mods, layer, 1, tpg),
            _const_spec((D, 3 * D)), tab_spec, tab_spec, tab_spec]


def _qkv(x, g, mods, layer, tm, w, tables):
    r = x.shape[0]
    tpg = (r // mods.shape[1]) // tm
    f32 = jax.ShapeDtypeStruct((r, D), F32)
    return pl.pallas_call(
        _qkv_body,
        grid=(r // tm,),
        in_specs=_qkv_in_specs(mods, layer, tm, tpg, 1),
        out_specs=[_row_spec(tm, D)] * 3,
        out_shape=[f32, f32, f32],
        compiler_params=_cparams("parallel"),
        name="odd_qkv_decode",
    )(x, g, mods, mods, w, *tables)


def _qkv_classes(x, g, mods, layer, tm, w, tables, nseq):
    r = x.shape[0]
    s_len = r // nseq
    tps = s_len // tm
    cls_spec = lambda dil: pl.BlockSpec((None, dil, tm // dil, D), lambda i: (i // tps, 0, i % tps, 0))
    cls_shape = lambda dil: jax.ShapeDtypeStruct((nseq, dil, s_len // dil, D), BF16)
    f32 = jax.ShapeDtypeStruct((nseq, D, s_len), F32)
    f32_spec = pl.BlockSpec((None, D, tm), lambda i: (i // tps, 0, i % tps))
    return pl.pallas_call(
        functools.partial(_qkv_classes_body, tps=tps),
        grid=(r // tm,),
        in_specs=_qkv_in_specs(mods, layer, tm, tps, tps),
        out_specs=[cls_spec(4)] * 3 + [cls_spec(16)] * 3 + [f32_spec] * 2 + [_row_spec(tm, D), _row_spec(tm, LANES)],
        out_shape=[cls_shape(4)] * 3 + [cls_shape(16)] * 3 + [f32, f32, jax.ShapeDtypeStruct((r, D), BF16),
                                                                jax.ShapeDtypeStruct((r, LANES), F32)],
        scratch_shapes=[pltpu.VMEM((3, tm, LANES), F32), pltpu.VMEM((3, tm, LANES), F32), pltpu.VMEM((tm, D), BF16),
                        pltpu.VMEM((BAND + tm, D), BF16), pltpu.VMEM((BAND + tm, D), BF16)],
        compiler_params=_cparams("arbitrary"),
        name="odd_qkv_attn1",
    )(x, g, mods, mods, w, *tables)


def _outproj_body(*refs, n_in):
    ins, ws = refs[:n_in], refs[n_in:2 * n_in]
    x_ref, gp_ref, gate_ref, o_ref = refs[2 * n_in:]
    y = _mm(ins[0][...], ws[0][...])
    for a, w in zip(ins[1:], ws[1:]):
        y = y + _mm(a[...], w[...])
    o_ref[...] = x_ref[...] + gate_ref[...] * (_rms(y) * gp_ref[...])


def _outproj(ins, ws, x, gp, mods, layer, tm):
    r = x.shape[0]
    tpg = (r // mods.shape[1]) // tm
    return pl.pallas_call(
        functools.partial(_outproj_body, n_in=len(ins)),
        grid=(r // tm,),
        in_specs=[_row_spec(tm, a.shape[1]) for a in ins] + [_const_spec(w.shape) for w in ws]
        + [_row_spec(tm, D), _const_spec((1, D)), _mod_spec(mods, layer, 2, tpg)],
        out_specs=_row_spec(tm, D),
        out_shape=jax.ShapeDtypeStruct((r, D), F32),
        compiler_params=_cparams("parallel"),
        name="outproj",
    )(*ins, *ws, x, gp, mods)


def _ffn_body(x_ref, g_ref, sh_ref, sc_ref, gate_ref, gp_ref, w1_ref, w2_ref, o_ref, h_scr, acc_scr):
    j = pl.program_id(1)

    @pl.when(j == 0)
    def _():
        h_scr[...] = _modnorm(x_ref[...], g_ref[...], sh_ref[...], sc_ref[...]).astype(BF16)
        acc_scr[...] = jnp.zeros_like(acc_scr)

    t = jnp.maximum(_mm(h_scr[...], w1_ref[...]), 0.0)
    acc_scr[...] += _mm(t * t, w2_ref[...])

    @pl.when(j == pl.num_programs(1) - 1)
    def _():
        o_ref[...] = x_ref[...] + gate_ref[...] * (_rms(acc_scr[...]) * gp_ref[...])


def _ffn(x, g, gp, mods, layer, tm, tf, w1, w2):
    r = x.shape[0]
    tpg = (r // mods.shape[1]) // tm
    return pl.pallas_call(
        _ffn_body,
        grid=(r // tm, D_FF // tf),
        in_specs=[_row_spec(tm, D), _const_spec((1, D)), _mod_spec(mods, layer, 3, tpg),
                  _mod_spec(mods, layer, 4, tpg), _mod_spec(mods, layer, 5, tpg), _const_spec((1, D)),
                  pl.BlockSpec((D, tf), lambda i, j: (0, j)), pl.BlockSpec((tf, D), lambda i, j: (j, 0))],
        out_specs=_row_spec(tm, D),
        out_shape=jax.ShapeDtypeStruct((r, D), F32),
        scratch_shapes=[pltpu.VMEM((tm, D), BF16), pltpu.VMEM((tm, D), F32)],
        compiler_params=_cparams("parallel", "arbitrary"),
        name="ffn",
    )(x, g, mods, mods, mods, gp, w1, w2)


def _seg_cumsum(x, seg):
    row = lax.broadcasted_iota(jnp.int32, (x.shape[0], 1), 0) % seg
    s = 1
    while s < seg:
        x = x + jnp.where(row >= s, pltpu.roll(x, s, 0), 0.0)
        s *= 2
    return x


def _rwkv_body(pb_ref, prev_ref, s0_ref, mu_ref, prm_ref, lora_ref, hsum_ref, out_ref, s_ref, last_scr,
               *, nb, tt, c):
    @pl.when(pl.program_id(1) == 0)
    def _():
        s_ref[...] = s0_ref[...]
        last_scr[...] = prev_ref[...]

    mu = mu_ref[...]
    w0, a0, k_k, k_a, r_k, ln_g, ln_b = (prm_ref[i:i + 1, :] for i in range(7))
    hsum = hsum_ref[...]
    headsum = lambda t: _mm(t, hsum, passes=2)
    row = lax.broadcasted_iota(jnp.int32, (tt, 1), 0)
    ii = lax.broadcasted_iota(jnp.int32, (c, 2 * c), 0)
    jj = lax.broadcasted_iota(jnp.int32, (c, 2 * c), 1) % c
    strict, incl = jj < ii, jj <= ii
    eye = jnp.where(lax.broadcasted_iota(jnp.int32, (c, c), 0) == lax.broadcasted_iota(jnp.int32, (c, c), 1), 1.0, 0.0).astype(F32)
    zeros_c = jnp.zeros((c, HEAD), BF16)
    n_chunk = tt // c
    levels = int(math.log2(c)) - 1

    seqs = []
    for s in range(nb):
        x = pb_ref[s]
        shifted = jnp.where(row == 0, last_scr[s], pltpu.roll(x, 1, 0))
        last_scr[s] = x[tt - 1:tt, :]
        xm = x + mu * (shifted - x)
        r, k, v = xm[:, :B_WIDTH], xm[:, B_WIDTH:2 * B_WIDTH], xm[:, 2 * B_WIDTH:3 * B_WIDTH]
        lo = xm[:, 3 * B_WIDTH:]
        z = -(w0 + _mm(jnp.tanh(lo), lora_ref[0]))
        softplus = jnp.maximum(z, 0.0) + jnp.log1p(jnp.exp(-jnp.abs(z)))
        lw = -jnp.exp(-softplus - 0.5)
        a = jax.nn.sigmoid(a0 + _mm(lo, lora_ref[1]))
        g = _mm(jax.nn.sigmoid(lo), lora_ref[2])
        kk = k * k_k
        kk = kk * lax.rsqrt(headsum(kk * kk) + 1e-12)
        k2 = k * (1.0 + (a - 1.0) * k_a)
        lp = _seg_cumsum(lw, c)
        e_pos, e_neg = jnp.exp(lp), jnp.exp(-lp)
        seqs.append(dict(
            r=r, k2=k2, v=v, g=g, e_pos=e_pos,
            a_t=-kk * jnp.exp(lp - lw), b_t=kk * a * e_neg, k_t=k2 * e_neg, r_t=r * e_pos, v_b=v))

    def piece(s, name, ci, h):
        return seqs[s][name][ci * c:(ci + 1) * c, h * HEAD:(h + 1) * HEAD].astype(BF16)

    chains = [(s, h) for s in range(nb) for h in range(B_HEADS)]
    units = [(ci, s, h) for ci in range(n_chunk) for (s, h) in chains]

    a_u = {u: piece(u[1], "a_t", u[0], u[2]) for u in units}
    r_u = {u: piece(u[1], "r_t", u[0], u[2]) for u in units}
    v_u = {u: piece(u[1], "v_b", u[0], u[2]) for u in units}
    ar_u = {u: jnp.concatenate([a_u[u], r_u[u]], axis=0) for u in units}
    bk_u = {u: jnp.concatenate([piece(u[1], "b_t", u[0], u[2]), piece(u[1], "k_t", u[0], u[2])], axis=0) for u in units}
    zv_u = {u: jnp.concatenate([zeros_c, v_u[u]], axis=0) for u in units}
    g12 = {u: _mm(ar_u[u], bk_u[u], NT) for u in units}
    g1 = {u: jnp.where(strict, g12[u][:c], 0.0).astype(BF16) for u in units}
    g2 = {u: jnp.where(incl, g12[u][c:], 0.0).astype(BF16) for u in units}
    n_u = {u: g1[u][:, :c] for u in units}
    inv = {u: eye + n_u[u].astype(F32) for u in units}
    pw = {u: _mm(n_u[u], n_u[u]) for u in units}
    for level in range(levels):
        if level + 1 < levels:
            both = {u: _mm(jnp.concatenate([pw[u], inv[u]], axis=0), pw[u]) for u in units}
            pw = {u: both[u][:c] for u in units}
            inv = {u: inv[u] + both[u][c:] for u in units}
        else:
            inv = {u: inv[u] + _mm(inv[u], pw[u]) for u in units}
    xn = {u: _mm(jnp.concatenate(_split(inv[u]), axis=0), n_u[u]) for u in units}
    res = {u: eye - inv[u] + (xn[u][:c] + xn[u][c:]) for u in units}
    inv = {u: inv[u] + _mm(res[u], inv[u]) for u in units}

    state = {ch: s_ref[ch[0], ch[1]] for ch in chains}
    o_u = {}
    for ci in range(n_chunk):
        us = [(ci, s, h) for (s, h) in chains]
        ars = {u: _mm(ar_u[u], state[u[1:]], NT) for u in us}
        rhs = {u: ars[u][:c] + _mm(g1[u], zv_u[u]) for u in us}
        uu = {u: _mm(inv[u], rhs[u]) for u in us}
        uv = {u: jnp.concatenate([uu[u].astype(BF16), v_u[u]], axis=0) for u in us}
        for u in us:
            o_u[u] = ars[u][c:] + _mm(g2[u], uv[u])
        for u in us:
            s, h = u[1:]
            p_end = seqs[s]["e_pos"][(ci + 1) * c - 1:(ci + 1) * c, h * HEAD:(h + 1) * HEAD]
            state[(s, h)] = (state[(s, h)] + _mm(uv[u], bk_u[u], TN)) * p_end
    for (s, h) in chains:
        s_ref[s, h] = state[(s, h)]

    for s in range(nb):
        q = seqs[s]
        o_rows = [jnp.concatenate([o_u[(ci, s, h)] for h in range(B_HEADS)], axis=1) for ci in range(n_chunk)]
        o = o_rows[0] if n_chunk == 1 else jnp.concatenate(o_rows, axis=0)
        inv_n = 1.0 / HEAD
        oc = o - headsum(o) * inv_n
        on = oc * lax.rsqrt(headsum(oc * oc) * inv_n + GN_EPS) * ln_g + ln_b
        bonus = headsum(q["r"] * q["k2"] * r_k) * q["v"]
        out_ref[s] = ((on + bonus) * q["g"]).astype(out_ref.dtype)


def _rwkv(pb, prev, s0, mu, prm, lora, hsum, nb, tt, c):
    n, t, _ = pb.shape
    return pl.pallas_call(
        functools.partial(_rwkv_body, nb=nb, tt=tt, c=c),
        grid=(n // nb, t // tt),
        in_specs=[pl.BlockSpec((nb, tt, B_COLS), lambda i, j: (i, j, 0)),
                  pl.BlockSpec((nb, 1, B_COLS), lambda i, j: (i, 0, 0)),
                  pl.BlockSpec((nb, B_HEADS, HEAD, HEAD), lambda i, j: (i, 0, 0, 0)),
                  _const_spec(mu.shape), _const_spec(prm.shape), _const_spec(lora.shape), _const_spec(hsum.shape)],
        out_specs=[pl.BlockSpec((nb, tt, B_WIDTH), lambda i, j: (i, j, 0)),
                   pl.BlockSpec((nb, B_HEADS, HEAD, HEAD), lambda i, j: (i, 0, 0, 0))],
        out_shape=[jax.ShapeDtypeStruct((n, t, B_WIDTH), BF16), jax.ShapeDtypeStruct(s0.shape, F32)],
        scratch_shapes=[pltpu.VMEM((nb, 1, B_COLS), F32)],
        compiler_params=_cparams("parallel", "arbitrary"),
        name="rwkv7",
    )(pb, prev, s0, mu, prm, lora, hsum)


def _band_block(q_at, rows, keys, k_ref, v_ref, mask, acc_at, st_at, group):
    nk = mask.shape[1]
    lane = lax.broadcasted_iota(jnp.int32, (BAND, LANES), 1)
    lane_k = lax.broadcasted_iota(jnp.int32, (nk, LANES), 1)
    stats = jnp.zeros((BAND, LANES), F32)
    for g0 in range(0, C_HEADS // 2, group):
        pairs = range(g0, g0 + group)
        scores = []
        for pair in pairs:
            lanes = slice(pair * LANES, (pair + 1) * LANES)
            q2, k2 = q_at[rows, lanes], keys(k_ref, lanes)
            for half in range(2):
                in_head = (lane >= HEAD) if half else (lane < HEAD)
                scores.append(_mm(jnp.where(in_head, q2, jnp.zeros_like(q2)), k2, NT))
        probs = []
        for i, s in enumerate(scores):
            s = jnp.where(mask, s, NEG)
            m = jnp.max(s, axis=-1, keepdims=True)
            probs.append(jnp.exp(s - m).astype(BF16))
            stats = jnp.where(lane == (i % 2) * (C_HEADS // 2) + g0 + i // 2, m, stats)
        for i, pair in enumerate(pairs):
            lanes = slice(pair * LANES, (pair + 1) * LANES)
            v2 = keys(v_ref, lanes)
            lo = _mm(probs[2 * i], jnp.where(lane_k < HEAD, v2, jnp.ones_like(v2)))
            hi = _mm(probs[2 * i + 1], jnp.where(lane_k >= HEAD, v2, jnp.ones_like(v2)))
            acc_at[rows, lanes] = jnp.where(lane < HEAD, lo, hi).astype(acc_at.dtype)
            stats = jnp.where(lane == DEN_EVEN + pair, lo, stats)
            stats = jnp.where(lane == DEN_ODD + pair, hi, stats)
    st_at[rows, :] = stats


def _band_mask(first_block):
    qi = lax.broadcasted_iota(jnp.int32, (BAND, 2 * BAND), 0)
    kc = lax.broadcasted_iota(jnp.int32, (BAND, 2 * BAND), 1)
    return (kc >= qi) & (kc <= qi + BAND) & (kc >= jnp.where(first_block, BAND, 0))


def _band_attn_body(q_ref, k_ref, v_ref, acc_ref, st_ref, *, nblk, units):
    for j in range(units):
        if nblk == 1:
            qi = lax.broadcasted_iota(jnp.int32, (BAND, BAND), 0)
            kc = lax.broadcasted_iota(jnp.int32, (BAND, BAND), 1)
            _band_block(q_ref.at[j], slice(None), lambda ref, lanes: ref[j, :, lanes], k_ref, v_ref, kc <= qi,
                        acc_ref.at[j], st_ref.at[j], 8)
        else:
            blk = pl.program_id(2) * units + j
            cur = pl.multiple_of(blk * BAND, BAND)
            prev = pl.multiple_of(jnp.maximum(blk - 1, 0) * BAND, BAND)
            keys = lambda ref, lanes: jnp.concatenate([ref[pl.ds(prev, BAND), lanes], ref[pl.ds(cur, BAND), lanes]], axis=0)
            _band_block(q_ref, slice(j * BAND, (j + 1) * BAND), keys, k_ref, v_ref, _band_mask(blk == 0),
                        acc_ref, st_ref, 4)


def _band_attn(q, k, v, units=4):
    nseq, dil, sub, _ = q.shape
    nblk = sub // BAND
    if nblk == 1:
        grid = (nseq, dil // units, 1)
        blk = lambda width: pl.BlockSpec((None, units, sub, width), lambda b, r, n: (b, r, 0, 0))
        cls = blk(D)
    else:
        grid = (nseq, dil, nblk // units)
        blk = lambda width: pl.BlockSpec((None, None, units * BAND, width), lambda b, r, n: (b, r, n, 0))
        cls = pl.BlockSpec((None, None, sub, D), lambda b, r, n: (b, r, 0, 0))
    return pl.pallas_call(
        functools.partial(_band_attn_body, nblk=nblk, units=units),
        grid=grid,
        in_specs=[blk(D), cls, cls],
        out_specs=[blk(D), blk(LANES)],
        out_shape=[jax.ShapeDtypeStruct(q.shape, q.dtype), jax.ShapeDtypeStruct((nseq, dil, sub, LANES), F32)],
        compiler_params=_cparams("parallel", "parallel", "arbitrary"),
        name=f"band_attn_d{dil}",
    )(q, k, v)


def _mix_body(a1_ref, a4_ref, a16_ref, s1_ref, s4_ref, s16_ref, e_ref, w_ref, x_ref, gp_ref, gate_ref, o_ref,
              slab4, slab16, st4, st16, o_scr):
    tm = x_ref.shape[0]
    ngrp = D // LANES
    for dil, a_ref, s_ref, slab, st in ((4, a4_ref, s4_ref, slab4, st4), (16, a16_ref, s16_ref, slab16, st16)):
        for r in range(dil):
            rows = pl.ds(r, tm // dil, stride=dil)
            st[rows, :] = s_ref[r]
            for grp in range(ngrp):
                slab[grp, rows, :] = a_ref[r, :, grp * LANES:(grp + 1) * LANES].astype(F32)
    sts = [s1_ref[...], st4[...], st16[...]]
    lane = lax.broadcasted_iota(jnp.int32, sts[0].shape, 1)
    top = jnp.maximum(jnp.maximum(sts[0], sts[1]), sts[2])
    es = [jnp.exp(st - top) for st in sts]
    dens = [jnp.where(lane < C_HEADS // 2, pltpu.roll(st, LANES - DEN_EVEN, 1),
                      pltpu.roll(st, LANES - DEN_ODD + C_HEADS // 2, 1)) for st in sts]
    total = es[0] * dens[0] + es[1] * dens[1] + es[2] * dens[2]
    coefs = [_mm(jnp.where(lane < C_HEADS, e / total, 0.0), e_ref[...], passes=2) for e in es]
    for grp in range(ngrp):
        lanes = slice(grp * LANES, (grp + 1) * LANES)
        o = (coefs[0][:, lanes] * a1_ref[:, lanes].astype(F32) + coefs[1][:, lanes] * slab4[grp]
             + coefs[2][:, lanes] * slab16[grp])
        o_scr[:, lanes] = o.astype(BF16)
    y = _mm(o_scr[...], w_ref[...])
    o_ref[...] = x_ref[...] + gate_ref[...] * (_rms(y) * gp_ref[...])


def _mix_outproj(accs, sts, expand, w, x, gp, mods, layer, tm):
    r = x.shape[0]
    nseq = accs[1].shape[0]
    tps = (r // nseq) // tm
    cls_spec = lambda dil, width: pl.BlockSpec((None, dil, tm // dil, width), lambda i: (i // tps, 0, i % tps, 0))
    return pl.pallas_call(
        _mix_body,
        grid=(r // tm,),
        in_specs=[_row_spec(tm, D), cls_spec(4, D), cls_spec(16, D), _row_spec(tm, LANES), cls_spec(4, LANES),
                  cls_spec(16, LANES), _const_spec(expand.shape), _const_spec(w.shape), _row_spec(tm, D),
                  _const_spec((1, D)), _mod_spec(mods, layer, 2, tps)],
        out_specs=_row_spec(tm, D),
        out_shape=jax.ShapeDtypeStruct((r, D), F32),
        scratch_shapes=[pltpu.VMEM((D // LANES, tm, LANES), F32), pltpu.VMEM((D // LANES, tm, LANES), F32),
                        pltpu.VMEM((tm, LANES), F32), pltpu.VMEM((tm, LANES), F32), pltpu.VMEM((tm, D), BF16)],
        compiler_params=_cparams("parallel"),
        name="attn_mix_outproj",
    )(*accs, *sts, expand, w, x, gp, mods)


def _attn_sample_body(q_ref, kn_ref, vn_ref, kc_ref, vc_ref, o_ref, *, t_len, window):
    tq = lax.broadcasted_iota(jnp.int32, (t_len, window), 0)
    back = window + tq - lax.broadcasted_iota(jnp.int32, (t_len, window), 1)
    cnt = (jnp.where(back <= BAND, 1.0, 0.0) + jnp.where((back <= 4 * BAND) & ((back & 3) == 0), 1.0, 0.0)
           + jnp.where((back & 15) == 0, 1.0, 0.0))
    tn = lax.broadcasted_iota(jnp.int32, (t_len, t_len), 0)
    dn = tn - lax.broadcasted_iota(jnp.int32, (t_len, t_len), 1)
    mult = jnp.where(dn == 0, 3.0, 0.0) + jnp.where(dn > 0, 1.0, 0.0) + jnp.where(dn == 4, 1.0, 0.0)
    lane = lax.broadcasted_iota(jnp.int32, (t_len, LANES), 1)
    for pair in range(q_ref.shape[1] // LANES):
        lanes = slice(pair * LANES, (pair + 1) * LANES)
        q2, kn, vn = q_ref[:, lanes], kn_ref[:, lanes].astype(BF16), vn_ref[:, lanes].astype(BF16)
        kc, vc = kc_ref[lanes, :].astype(BF16), vc_ref[lanes, :].astype(BF16)
        outs = []
        for half in range(2):
            qh = jnp.where((lane >= HEAD) if half else (lane < HEAD), q2, 0.0).astype(BF16)
            s_c = jnp.where(cnt > 0.0, _mm(qh, kc), NEG)
            s_n = jnp.where(mult > 0.0, _mm(qh, kn, NT), NEG)
            top = jnp.maximum(jnp.max(s_c, axis=-1, keepdims=True), jnp.max(s_n, axis=-1, keepdims=True))
            p_c = jnp.exp(s_c - top) * cnt
            p_n = jnp.exp(s_n - top) * mult
            den = jnp.sum(p_c, axis=-1, keepdims=True) + jnp.sum(p_n, axis=-1, keepdims=True)
            outs.append((_mm(p_c, vc, NT) + _mm(p_n, vn)) / den)
        o_ref[:, lanes] = jnp.where(lane < HEAD, outs[0], outs[1])


def _attn_sample(q, kn, vn, cache_k, cache_v, layer, nseq):
    window = cache_k.shape[-1]
    t_len = q.shape[0] // nseq
    assert window == 16 * BAND and t_len == 8
    width = 4 * LANES
    new = pl.BlockSpec((t_len, width), lambda b, g: (b, g))
    cached = pl.BlockSpec((None, None, width, window), lambda b, g: (layer, b, g, 0))
    return pl.pallas_call(
        functools.partial(_attn_sample_body, t_len=t_len, window=window),
        grid=(nseq, D // width),
        in_specs=[new, new, new, cached, cached],
        out_specs=new,
        out_shape=jax.ShapeDtypeStruct(q.shape, F32),
        compiler_params=_cparams("parallel", "parallel"),
        name="attn_sample",
    )(q, kn, vn, cache_k, cache_v)


def kernel(x_prompt, x_sample, state_wkv, state_shift, cache_k, cache_v, c_prompt, c_sample, ada_w, ada_b, norm_mix_pre, norm_mix_post, norm_ffn_pre, norm_ffn_post, ffn_w1, ffn_w2, ev_w_in, ev_w_out, gm_ln_g, gm_ln_b, gm_ws, gm_bs, rw_mu, rw_w0, rw_w2, rw_a0, rw_a2, rw_g2, rw_kk, rw_ka, rw_rk, rw_lnx_g, rw_lnx_b, od_w_qkv, od_w_out):
    n_p, s_len, _ = x_prompt.shape
    n_s, t_len, _ = x_sample.shape
    r_p, r_s = n_p * s_len, n_s * t_len
    row = lambda t: t.reshape(1, -1).astype(F32)

    mods = _ada(jnp.concatenate([c_prompt, c_sample], axis=0).astype(F32), ada_w, ada_b)
    mods_p = mods[:, :n_p].reshape(DEPTH, n_p, 1, 6 * D)
    mods_s = jnp.repeat(mods[:, n_p:], t_len, axis=1).reshape(DEPTH, 1, r_s, 6 * D)

    xp = x_prompt.astype(F32).reshape(r_p, D)
    xs = x_sample.astype(F32).reshape(r_s, D)

    hsum = jnp.kron(jnp.eye(B_HEADS, dtype=F32), jnp.ones((HEAD, HEAD), F32)).astype(BF16)
    tile_head = 2 * (jnp.arange(LANES) % (C_HEADS // 2)) + jnp.arange(LANES) // (C_HEADS // 2)
    expand = ((tile_head[:, None] == jnp.arange(D)[None, :] // HEAD) & (jnp.arange(LANES)[:, None] < C_HEADS)).astype(BF16)
    tabs_p = _rope_tables(jnp.arange(s_len))
    tabs_s = _rope_tables(jnp.tile(PAST_LEN + jnp.arange(t_len), n_s))
    zeros_prev = jnp.zeros((n_p, 1, B_COLS), F32)
    zeros_state = jnp.zeros((n_p, B_HEADS, HEAD, HEAD), F32)
    reps = CHUNK // t_len
    feat_major = lambda c: jnp.transpose(c.astype(F32), (0, 1, 3, 4, 2)).reshape(c.shape[0], c.shape[1], D, c.shape[2])
    cache_kt, cache_vt = feat_major(cache_k), feat_major(cache_v)
    tok_major = lambda t: jnp.transpose(t.reshape(n_p, C_HEADS, HEAD, s_len), (0, 3, 1, 2))

    tm_p = 512
    outs = {k: [] for k in ("wkv_p", "shift_p", "k_p", "v_p", "wkv_s", "shift_s", "k_s", "v_s", "gv_s")}
    for l in range(DEPTH):
        g_pre, g_post = row(norm_mix_pre[l]), row(norm_mix_post[l])
        if l % 2 == 0:
            e = l // 2
            w_in = ev_w_in[e].astype(BF16)
            w_out = ev_w_out[e].astype(BF16)
            w_oa, w_ob = w_out[:A_WIDTH], w_out[A_WIDTH:]
            ln_g, ln_b = row(gm_ln_g[e]), row(gm_ln_b[e])
            ws = gm_ws[e].astype(F32)
            bias_p = jnp.repeat(gm_bs[e].T, A_WIDTH // A_GROUPS, axis=1)
            ws_s = jax.vmap(lambda m: jnp.kron(jnp.eye(reps, dtype=F32), m))(ws[:, :t_len, :t_len])
            bias_s = jnp.tile(bias_p[:t_len], (reps, 1))
            mu = row(rw_mu[e])
            zpad = jnp.zeros((7 * B_WIDTH,), F32)
            prm = jnp.concatenate([rw_w0[e], rw_a0[e], rw_kk[e], rw_ka[e], rw_rk[e].reshape(-1), rw_lnx_g[e],
                                   rw_lnx_b[e], zpad[:B_WIDTH]]).reshape(8, B_WIDTH).astype(F32)
            zl = lambda a, b: jnp.zeros((a, b), F32)
            lora = jnp.stack([
                jnp.concatenate([rw_w2[e], zl(96, B_WIDTH)], axis=0),
                jnp.concatenate([zl(32, B_WIDTH), rw_a2[e], zl(64, B_WIDTH)], axis=0),
                jnp.concatenate([zl(64, B_WIDTH), rw_g2[e]], axis=0)]).astype(BF16)

            oa_p, pb_p = _inproj(xp, g_pre, mods_p, l, tm_p, w_in, ln_g, ln_b, ws, bias_p, False)
            pb_p3 = pb_p.reshape(n_p, s_len, B_COLS)
            ob_p, st_p = _rwkv(pb_p3, zeros_prev, zeros_state, mu, prm, lora, hsum, 2, 128, 64)
            xp = _outproj([oa_p, ob_p.reshape(r_p, B_WIDTH)], [w_oa, w_ob], xp, g_post, mods_p, l, tm_p)
            outs["wkv_p"].append(st_p)
            outs["shift_p"].append(pb_p3[:, -1])

            oa_s, pb_s, gv = _inproj(xs, g_pre, mods_s, l, r_s, w_in, ln_g, ln_b, ws_s, bias_s, True)
            pb_s3 = pb_s.reshape(n_s, t_len, B_COLS)
            ob_s, st_s = _rwkv(pb_s3, state_shift[e].astype(F32).reshape(n_s, 1, B_COLS), state_wkv[e].astype(F32),
                               mu, prm, lora, hsum, 8, t_len, t_len)
            xs = _outproj([oa_s, ob_s.reshape(r_s, B_WIDTH)], [w_oa, w_ob], xs, g_post, mods_s, l, r_s)
            outs["wkv_s"].append(st_s)
            outs["shift_s"].append(pb_s3[:, -1])
            outs["gv_s"].append(gv.reshape(n_s, t_len, A_WIDTH))
        else:
            o = l // 2
            w_qkv = od_w_qkv[o].astype(BF16)
            w_out = od_w_out[o].astype(BF16)
            q4, k4, v4, q16, k16, v16, kf, vf, acc1, st1 = _qkv_classes(xp, g_pre, mods_p, l, 256, w_qkv, tabs_p, n_p)
            res = [_band_attn(q4, k4, v4), _band_attn(q16, k16, v16)]
            accs = [acc1, res[0][0], res[1][0]]
            sts = [st1, res[0][1], res[1][1]]
            xp = _mix_outproj(accs, sts, expand, w_out, xp, g_post, mods_p, l, 512)
            outs["k_p"].append(tok_major(kf))
            outs["v_p"].append(tok_major(vf))

            qs, kfs, vfs = _qkv(xs, g_pre, mods_s, l, r_s, w_qkv, tabs_s)
            heads = lambda t: t.reshape(n_s, t_len, C_HEADS, HEAD)
            att = _attn_sample(qs, kfs, vfs, cache_kt, cache_vt, o, n_s)
            xs = _outproj([att], [w_out], xs, g_post, mods_s, l, r_s)
            outs["k_s"].append(heads(kfs))
            outs["v_s"].append(heads(vfs))
        w1, w2 = ffn_w1[l].astype(BF16), ffn_w2[l].astype(BF16)
        g_fpre, g_fpost = row(norm_ffn_pre[l]), row(norm_ffn_post[l])
        xp = _ffn(xp, g_fpre, g_fpost, mods_p, l, 1024, 1024, w1, w2)
        xs = _ffn(xs, g_fpre, g_fpost, mods_s, l, r_s, 512, w1, w2)

    st = lambda name: jnp.stack(outs[name])
    return (xp.reshape(n_p, s_len, D), xs.reshape(n_s, t_len, D), st("wkv_p"), st("shift_p"), st("k_p"),
            st("v_p"), st("wkv_s"), st("shift_s"), st("k_s"), st("v_s"), st("gv_s"))
```

```python
import functools
import math

import jax
import jax.numpy as jnp
from jax import lax
from jax.experimental import pallas as pl
from jax.experimental.pallas import tpu as pltpu

F32 = jnp.float32
BF16 = jnp.bfloat16

D = 1024
DEPTH = 4
D_FF = 4 * D
A_WIDTH = 512
A_GROUPS = 4
CHUNK = 128
B_WIDTH = 512
B_HEADS = 8
HEAD = 64
LORA = 128
B_COLS = 3 * B_WIDTH + LORA
C_HEADS = 16
ROT_DIM = 16
ROPE_THETA = 500000.0
DILATIONS = (1, 4, 16)
BAND = 128
PAST_LEN = 8192
NORM_EPS = 1e-6
LN_EPS = 1e-5
GN_EPS = 64e-5
NEG = -1e30

LANES = 128
DEN_EVEN, DEN_ODD = 64, 16
VMEM_LIMIT = 52 * 1024 * 1024


def _cparams(*sem):
    return pltpu.CompilerParams(dimension_semantics=sem, vmem_limit_bytes=VMEM_LIMIT)


def _split(x):
    hi = x.astype(BF16)
    return hi, (x - hi.astype(F32)).astype(BF16)


def _mm(a, b, dims=((1,), (0,)), passes=1):
    dn = (dims, ((), ()))
    dot = lambda p, q: lax.dot_general(p, q, dn, preferred_element_type=F32)
    if passes == 1:
        return dot(a.astype(BF16), b.astype(BF16))
    ah, al = _split(a)
    return dot(ah, b) + dot(al, b)


NT = ((1,), (1,))
TN = ((0,), (0,))


def _rms(x):
    return x * lax.rsqrt(jnp.mean(x * x, axis=-1, keepdims=True) + NORM_EPS)


def _mod_spec(mods, layer, chunk, tiles_per_group):
    return pl.BlockSpec((None, None, mods.shape[2], D), lambda i, *_: (layer, i // tiles_per_group, 0, chunk))


def _row_spec(tm, n):
    return pl.BlockSpec((tm, n), lambda i, *_: (i, 0))


def _const_spec(shape):
    nd = len(shape)
    return pl.BlockSpec(shape, lambda *_: (0,) * nd)


def _ada_body(c_ref, w_ref, b_ref, o_ref):
    c = c_ref[...]
    o_ref[...] = _mm(c * jax.nn.sigmoid(c), w_ref[...]) + b_ref[...]


def _ada(c_all, ada_w, ada_b):
    n = c_all.shape[0]
    tn = 1536
    return pl.pallas_call(
        _ada_body,
        grid=(DEPTH, 6 * D // tn),
        in_specs=[pl.BlockSpec((n, D), lambda l, j: (0, 0)),
                  pl.BlockSpec((None, D, tn), lambda l, j: (l, 0, j)),
                  pl.BlockSpec((None, 1, tn), lambda l, j: (l, 0, j))],
        out_specs=pl.BlockSpec((None, n, tn), lambda l, j: (l, 0, j)),
        out_shape=jax.ShapeDtypeStruct((DEPTH, n, 6 * D), F32),
        compiler_params=_cparams("parallel", "parallel"),
        name="ada",
    )(c_all, ada_w, ada_b.reshape(DEPTH, 1, 6 * D))


def _modnorm(x, g, sh, sc):
    return _rms(x) * g * (1.0 + sc) + sh


def _gelu(x):
    return 0.5 * x * (1.0 + lax.erf(x * (2.0 ** -0.5)))


def _inproj_body(x_ref, g_ref, sh_ref, sc_ref, w_ref, lng_ref, lnb_ref, ws_ref, bias_ref, oa_ref, pb_ref, *v_refs):
    h = _modnorm(x_ref[...], g_ref[...], sh_ref[...], sc_ref[...]).astype(BF16)
    pb_ref[...] = _mm(h, w_ref[:, 2 * A_WIDTH:])
    u = _gelu(_mm(h, w_ref[:, :A_WIDTH]))
    v = _gelu(_mm(h, w_ref[:, A_WIDTH:2 * A_WIDTH]))
    vc = v - jnp.mean(v, axis=-1, keepdims=True)
    var = jnp.mean(vc * vc, axis=-1, keepdims=True)
    vn = vc * lax.rsqrt(var + LN_EPS) * lng_ref[...] + lnb_ref[...]
    if v_refs:
        v_refs[0][...] = vn
    ii = lax.broadcasted_iota(jnp.int32, (CHUNK, CHUNK), 0)
    jj = lax.broadcasted_iota(jnp.int32, (CHUNK, CHUNK), 1)
    cg = A_WIDTH // A_GROUPS
    for grp in range(A_GROUPS):
        lanes = slice(grp * cg, (grp + 1) * cg)
        wc = jnp.where(jj <= ii, ws_ref[grp], 0.0).astype(BF16)
        for ci in range(x_ref.shape[0] // CHUNK):
            rows = slice(ci * CHUNK, (ci + 1) * CHUNK)
            z = _mm(wc, vn[rows, lanes]) + bias_ref[:, lanes]
            oa_ref[rows, lanes] = (u[rows, lanes] * z).astype(oa_ref.dtype)


def _inproj(x, g, mods, layer, tm, w, ln_g, ln_b, ws, bias, emit_v):
    r = x.shape[0]
    tpg = (r // mods.shape[1]) // tm
    n = w.shape[1]
    half = lambda dtype: jax.ShapeDtypeStruct((r, A_WIDTH), dtype)
    return pl.pallas_call(
        _inproj_body,
        grid=(r // tm,),
        in_specs=[_row_spec(tm, D), _const_spec((1, D)), _mod_spec(mods, layer, 0, tpg),
                  _mod_spec(mods, layer, 1, tpg), _const_spec((D, n)), _const_spec((1, A_WIDTH)),
                  _const_spec((1, A_WIDTH)), _const_spec(ws.shape), _const_spec(bias.shape)],
        out_specs=[_row_spec(tm, A_WIDTH), _row_spec(tm, B_COLS)] + ([_row_spec(tm, A_WIDTH)] if emit_v else []),
        out_shape=[half(BF16), jax.ShapeDtypeStruct((r, B_COLS), F32)] + ([half(F32)] if emit_v else []),
        compiler_params=_cparams("parallel"),
        name="even_inproj_gmlp",
    )(x, g, mods, mods, w, ln_g, ln_b, ws, bias)


def _rope(seg, cos, s_lo, s_hi):
    return seg * cos + pltpu.roll(seg, LANES - ROT_DIM // 2, 1) * s_lo + pltpu.roll(seg, ROT_DIM // 2, 1) * s_hi


def _qkv_groups(x_ref, g_ref, sh_ref, sc_ref, w_ref, cos_ref, slo_ref, shi_ref):
    h = _modnorm(x_ref[...], g_ref[...], sh_ref[...], sc_ref[...]).astype(BF16)
    cos, s_lo, s_hi = cos_ref[...], slo_ref[...], shi_ref[...]
    wide = 2 * LANES
    for pair in range(D // wide):
        q2, k2, v2 = (_mm(h, w_ref[:, part * D + pair * wide:part * D + (pair + 1) * wide]) for part in range(3))
        for half in range(2):
            part = slice(half * LANES, (half + 1) * LANES)
            grp = 2 * pair + half
            yield (slice(grp * LANES, (grp + 1) * LANES), _rope(q2[:, part], cos, s_lo, s_hi) * HEAD ** -0.5,
                   _rope(k2[:, part], cos, s_lo, s_hi), v2[:, part])


def _qkv_body(*refs):
    q_ref, kf_ref, vf_ref = refs[8:]
    for lanes, q, k, v in _qkv_groups(*refs[:8]):
        q_ref[:, lanes] = q
        kf_ref[:, lanes] = k
        vf_ref[:, lanes] = v


def _qkv_classes_body(*refs, tps):
    q4, k4, v4, q16, k16, v16, kf_ref, vf_ref, acc_ref, st_ref, slab, slab4, q_scr, k_scr, v_scr = refs[8:]
    tm = q_scr.shape[0]
    first_tile = pl.program_id(0) % tps == 0

    @pl.when(first_tile)
    def _():
        k_scr[:BAND, :] = jnp.zeros((BAND, D), BF16)
        v_scr[:BAND, :] = jnp.zeros((BAND, D), BF16)

    @pl.when(jnp.logical_not(first_tile))
    def _():
        k_scr[:BAND, :] = k_scr[tm:, :]
        v_scr[:BAND, :] = v_scr[tm:, :]

    for lanes, q, k, v in _qkv_groups(*refs[:8]):
        kf_ref[lanes, :] = k.T
        vf_ref[lanes, :] = v.T
        q_scr[:, lanes] = q.astype(BF16)
        k_scr[BAND:, lanes] = k.astype(BF16)
        v_scr[BAND:, lanes] = v.astype(BF16)
        for i, (val, o4, o16) in enumerate(((q, q4, q16), (k, k4, k16), (v, v4, v16))):
            slab[i] = val
            for r4 in range(4):
                cls = slab[i, pl.ds(r4, tm // 4, stride=4), :]
                o4[r4, :, lanes] = cls.astype(BF16)
                slab4[i, r4 * (tm // 4):(r4 + 1) * (tm // 4), :] = cls
            for r16 in range(16):
                start = (r16 % 4) * (tm // 4) + r16 // 4
                o16[r16, :, lanes] = slab4[i, pl.ds(start, tm // 16, stride=4), :].astype(BF16)
    for j in range(tm // BAND):
        keys = lambda ref, lanes: ref[j * BAND:(j + 2) * BAND, lanes]
        mask = _band_mask(first_tile) if j == 0 else _band_mask(False)
        _band_block(q_scr, slice(j * BAND, (j + 1) * BAND), keys, k_scr, v_scr, mask, acc_ref, st_ref, 4)


def _rope_tables(pos):
    half = ROT_DIM // 2
    inv = ROPE_THETA ** (-jnp.arange(half, dtype=F32) * (2.0 / ROT_DIM))
    ang = pos.astype(F32)[:, None] * inv[None, :]
    cos, sin = jnp.cos(ang), jnp.sin(ang)
    t = pos.shape[0]
    one, zero, z8 = jnp.ones((t, HEAD - ROT_DIM), F32), jnp.zeros((t, HEAD - ROT_DIM), F32), jnp.zeros((t, half), F32)
    tab = lambda *parts: jnp.tile(jnp.concatenate(parts, axis=1), (1, LANES // HEAD))
    return tab(cos, cos, one), tab(-sin, z8, zero), tab(z8, sin, zero)


def _qkv_in_specs(mods, layer, tm, tpg, tables_per_seq):
    tab_spec = pl.BlockSpec((tm, LANES), lambda i: (i % tables_per_seq, 0))
    return [_row_spec(tm, D), _const_spec((1, D)), _mod_spec(mods, layer, 0, tpg), _mod_spec(mods, layer, 1, tpg),
            _const_spec((D, 3 * D)), tab_spec, tab_spec, tab_spec]


def _qkv(x, g, mods, layer, tm, w, tables):
    r = x.shape[0]
    tpg = (r // mods.shape[1]) // tm
    f32 = jax.ShapeDtypeStruct((r, D), F32)
    return pl.pallas_call(
        _qkv_body,
        grid=(r // tm,),
        in_specs=_qkv_in_specs(mods, layer, tm, tpg, 1),
        out_specs=[_row_spec(tm, D)] * 3,
        out_shape=[f32, f32, f32],
        compiler_params=_cparams("parallel"),
        name="odd_qkv_decode",
    )(x, g, mods, mods, w, *tables)


def _qkv_classes(x, g, mods, layer, tm, w, tables, nseq):
    r = x.shape[0]
    s_len = r // nseq
    tps = s_len // tm
    cls_spec = lambda dil: pl.BlockSpec((None, dil, tm // dil, D), lambda i: (i // tps, 0, i % tps, 0))
    cls_shape = lambda dil: jax.ShapeDtypeStruct((nseq, dil, s_len // dil, D), BF16)
    f32 = jax.ShapeDtypeStruct((nseq, D, s_len), F32)
    f32_spec = pl.BlockSpec((None, D, tm), lambda i: (i // tps, 0, i % tps))
    return pl.pallas_call(
        functools.partial(_qkv_classes_body, tps=tps),
        grid=(r // tm,),
        in_specs=_qkv_in_specs(mods, layer, tm, tps, tps),
        out_specs=[cls_spec(4)] * 3 + [cls_spec(16)] * 3 + [f32_spec] * 2 + [_row_spec(tm, D), _row_spec(tm, LANES)],
        out_shape=[cls_shape(4)] * 3 + [cls_shape(16)] * 3 + [f32, f32, jax.ShapeDtypeStruct((r, D), BF16),
                                                                jax.ShapeDtypeStruct((r, LANES), F32)],
        scratch_shapes=[pltpu.VMEM((3, tm, LANES), F32), pltpu.VMEM((3, tm, LANES), F32), pltpu.VMEM((tm, D), BF16),
                        pltpu.VMEM((BAND + tm, D), BF16), pltpu.VMEM((BAND + tm, D), BF16)],
        compiler_params=_cparams("arbitrary"),
        name="odd_qkv_attn1",
    )(x, g, mods, mods, w, *tables)


def _outproj_body(*refs, n_in):
    ins, ws = refs[:n_in], refs[n_in:2 * n_in]
    x_ref, gp_ref, gate_ref, o_ref = refs[2 * n_in:]
    y = _mm(ins[0][...], ws[0][...])
    for a, w in zip(ins[1:], ws[1:]):
        y = y + _mm(a[...], w[...])
    o_ref[...] = x_ref[...] + gate_ref[...] * (_rms(y) * gp_ref[...])


def _outproj(ins, ws, x, gp, mods, layer, tm):
    r = x.shape[0]
    tpg = (r // mods.shape[1]) // tm
    return pl.pallas_call(
        functools.partial(_outproj_body, n_in=len(ins)),
        grid=(r // tm,),
        in_specs=[_row_spec(tm, a.shape[1]) for a in ins] + [_const_spec(w.shape) for w in ws]
        + [_row_spec(tm, D), _const_spec((1, D)), _mod_spec(mods, layer, 2, tpg)],
        out_specs=_row_spec(tm, D),
        out_shape=jax.ShapeDtypeStruct((r, D), F32),
        compiler_params=_cparams("parallel"),
        name="outproj",
    )(*ins, *ws, x, gp, mods)


def _ffn_body(x_ref, g_ref, sh_ref, sc_ref, gate_ref, gp_ref, w1_ref, w2_ref, o_ref, t_scr, *, fc):
    x = x_ref[...]
    h = _modnorm(x, g_ref[...], sh_ref[...], sc_ref[...]).astype(BF16)
    for c in range(D_FF // fc):
        cols = slice(c * fc, (c + 1) * fc)
        t = jnp.maximum(_mm(h, w1_ref[:, cols]), 0.0)
        t_scr[:, cols] = (t * t).astype(BF16)
    o_ref[...] = x + gate_ref[...] * (_rms(_mm(t_scr[...], w2_ref[...])) * gp_ref[...])


def _ffn(x, g, gp, mods, layer, tm, fc, w1, w2):
    r = x.shape[0]
    tpg = (r // mods.shape[1]) // tm
    resident = lambda shape: pl.BlockSpec(shape, lambda i: (0, 0), pipeline_mode=pl.Buffered(1))
    return pl.pallas_call(
        functools.partial(_ffn_body, fc=fc),
        grid=(r // tm,),
        in_specs=[_row_spec(tm, D), _const_spec((1, D)), _mod_spec(mods, layer, 3, tpg),
                  _mod_spec(mods, layer, 4, tpg), _mod_spec(mods, layer, 5, tpg), _const_spec((1, D)),
                  resident((D, D_FF)), resident((D_FF, D))],
        out_specs=_row_spec(tm, D),
        out_shape=jax.ShapeDtypeStruct((r, D), F32),
        scratch_shapes=[pltpu.VMEM((tm, D_FF), BF16)],
        compiler_params=_cparams("parallel"),
        name="ffn",
    )(x, g, mods, mods, mods, gp, w1, w2)


def _seg_cumsum(x, seg):
    row = lax.broadcasted_iota(jnp.int32, (x.shape[0], 1), 0) % seg
    s = 1
    while s < seg:
        x = x + jnp.where(row >= s, pltpu.roll(x, s, 0), 0.0)
        s *= 2
    return x


def _rwkv_body(pb_ref, prev_ref, s0_ref, mu_ref, prm_ref, lora_ref, hsum_ref, out_ref, s_ref, last_scr,
               *, nb, tt, c):
    @pl.when(pl.program_id(1) == 0)
    def _():
        s_ref[...] = s0_ref[...]
        last_scr[...] = prev_ref[...]

    mu = mu_ref[...]
    w0, a0, k_k, k_a, r_k, ln_g, ln_b = (prm_ref[i:i + 1, :] for i in range(7))
    hsum = hsum_ref[...]

    def headsum(t):
        groups = [_mm(t[:, g * LANES:(g + 1) * LANES], hsum, passes=2) for g in range(B_WIDTH // LANES)]
        return jnp.concatenate(groups, axis=1)

    row = lax.broadcasted_iota(jnp.int32, (tt, 1), 0)
    ii = lax.broadcasted_iota(jnp.int32, (c, 2 * c), 0)
    jj = lax.broadcasted_iota(jnp.int32, (c, 2 * c), 1) % c
    strict, incl = jj < ii, jj <= ii
    eye = jnp.where(lax.broadcasted_iota(jnp.int32, (c, c), 0) == lax.broadcasted_iota(jnp.int32, (c, c), 1), 1.0, 0.0).astype(F32)
    zeros_c = jnp.zeros((c, HEAD), BF16)
    n_chunk = tt // c
    levels = int(math.log2(c)) - 1

    seqs = []
    for s in range(nb):
        x = pb_ref[s]
        shifted = jnp.where(row == 0, last_scr[s], pltpu.roll(x, 1, 0))
        last_scr[s] = x[tt - 1:tt, :]
        xm = x + mu * (shifted - x)
        r, k, v = xm[:, :B_WIDTH], xm[:, B_WIDTH:2 * B_WIDTH], xm[:, 2 * B_WIDTH:3 * B_WIDTH]
        lo = xm[:, 3 * B_WIDTH:]
        z = -(w0 + _mm(jnp.tanh(lo), lora_ref[0]))
        softplus = jnp.maximum(z, 0.0) + jnp.log1p(jnp.exp(-jnp.abs(z)))
        lw = -jnp.exp(-softplus - 0.5)
        a = jax.nn.sigmoid(a0 + _mm(lo, lora_ref[1]))
        g = _mm(jax.nn.sigmoid(lo), lora_ref[2])
        kk = k * k_k
        kk = kk * lax.rsqrt(headsum(kk * kk) + 1e-12)
        k2 = k * (1.0 + (a - 1.0) * k_a)
        lp = _seg_cumsum(lw, c)
        e_pos, e_neg = jnp.exp(lp), jnp.exp(-lp)
        seqs.append(dict(
            r=r, k2=k2, v=v, g=g, e_pos=e_pos,
            a_t=-kk * jnp.exp(lp - lw), b_t=kk * a * e_neg, k_t=k2 * e_neg, r_t=r * e_pos, v_b=v))

    def piece(s, name, ci, h):
        return seqs[s][name][ci * c:(ci + 1) * c, h * HEAD:(h + 1) * HEAD].astype(BF16)

    chains = [(s, h) for s in range(nb) for h in range(B_HEADS)]
    units = [(ci, s, h) for ci in range(n_chunk) for (s, h) in chains]

    a_u = {u: piece(u[1], "a_t", u[0], u[2]) for u in units}
    r_u = {u: piece(u[1], "r_t", u[0], u[2]) for u in units}
    v_u = {u: piece(u[1], "v_b", u[0], u[2]) for u in units}
    ar_u = {u: jnp.concatenate([a_u[u], r_u[u]], axis=0) for u in units}
    bk_u = {u: jnp.concatenate([piece(u[1], "b_t", u[0], u[2]), piece(u[1], "k_t", u[0], u[2])], axis=0) for u in units}
    zv_u = {u: jnp.concatenate([zeros_c, v_u[u]], axis=0) for u in units}
    g12 = {u: _mm(ar_u[u], bk_u[u], NT) for u in units}
    g1 = {u: jnp.where(strict, g12[u][:c], 0.0).astype(BF16) for u in units}
    g2 = {u: jnp.where(incl, g12[u][c:], 0.0).astype(BF16) for u in units}
    n_u = {u: g1[u][:, :c] for u in units}
    inv = {u: eye + n_u[u].astype(F32) for u in units}
    pw = {u: _mm(n_u[u], n_u[u]) for u in units}
    for level in range(levels):
        if level + 1 < levels:
            both = {u: _mm(jnp.concatenate([pw[u], inv[u]], axis=0), pw[u]) for u in units}
            pw = {u: both[u][:c] for u in units}
            inv = {u: inv[u] + both[u][c:] for u in units}
        else:
            inv = {u: inv[u] + _mm(inv[u], pw[u]) for u in units}
    xn = {u: _mm(jnp.concatenate(_split(inv[u]), axis=0), n_u[u]) for u in units}
    res = {u: eye - inv[u] + (xn[u][:c] + xn[u][c:]) for u in units}
    inv = {u: inv[u] + _mm(res[u], inv[u]) for u in units}

    state = {ch: s_ref[ch[0], ch[1]] for ch in chains}
    o_u = {}
    for ci in range(n_chunk):
        us = [(ci, s, h) for (s, h) in chains]
        ars = {u: _mm(ar_u[u], state[u[1:]], NT) for u in us}
        rhs = {u: ars[u][:c] + _mm(g1[u], zv_u[u]) for u in us}
        uu = {u: _mm(inv[u], rhs[u]) for u in us}
        uv = {u: jnp.concatenate([uu[u].astype(BF16), v_u[u]], axis=0) for u in us}
        for u in us:
            o_u[u] = ars[u][c:] + _mm(g2[u], uv[u])
        for u in us:
            s, h = u[1:]
            p_end = seqs[s]["e_pos"][(ci + 1) * c - 1:(ci + 1) * c, h * HEAD:(h + 1) * HEAD]
            state[(s, h)] = (state[(s, h)] + _mm(uv[u], bk_u[u], TN)) * p_end
    for (s, h) in chains:
        s_ref[s, h] = state[(s, h)]

    for s in range(nb):
        q = seqs[s]
        o_rows = [jnp.concatenate([o_u[(ci, s, h)] for h in range(B_HEADS)], axis=1) for ci in range(n_chunk)]
        o = o_rows[0] if n_chunk == 1 else jnp.concatenate(o_rows, axis=0)
        inv_n = 1.0 / HEAD
        oc = o - headsum(o) * inv_n
        on = oc * lax.rsqrt(headsum(oc * oc) * inv_n + GN_EPS) * ln_g + ln_b
        bonus = headsum(q["r"] * q["k2"] * r_k) * q["v"]
        out_ref[s] = ((on + bonus) * q["g"]).astype(out_ref.dtype)


def _rwkv(pb, prev, s0, mu, prm, lora, hsum, nb, tt, c):
    n, t, _ = pb.shape
    return pl.pallas_call(
        functools.partial(_rwkv_body, nb=nb, tt=tt, c=c),
        grid=(n // nb, t // tt),
        in_specs=[pl.BlockSpec((nb, tt, B_COLS), lambda i, j: (i, j, 0)),
                  pl.BlockSpec((nb, 1, B_COLS), lambda i, j: (i, 0, 0)),
                  pl.BlockSpec((nb, B_HEADS, HEAD, HEAD), lambda i, j: (i, 0, 0, 0)),
                  _const_spec(mu.shape), _const_spec(prm.shape), _const_spec(lora.shape), _const_spec(hsum.shape)],
        out_specs=[pl.BlockSpec((nb, tt, B_WIDTH), lambda i, j: (i, j, 0)),
                   pl.BlockSpec((nb, B_HEADS, HEAD, HEAD), lambda i, j: (i, 0, 0, 0))],
        out_shape=[jax.ShapeDtypeStruct((n, t, B_WIDTH), BF16), jax.ShapeDtypeStruct(s0.shape, F32)],
        scratch_shapes=[pltpu.VMEM((nb, 1, B_COLS), F32)],
        compiler_params=_cparams("parallel", "arbitrary"),
        name="rwkv7",
    )(pb, prev, s0, mu, prm, lora, hsum)


def _band_block(q_at, rows, keys, k_ref, v_ref, mask, acc_at, st_at, group):
    nk = mask.shape[1]
    lane = lax.broadcasted_iota(jnp.int32, (BAND, LANES), 1)
    lane_k = lax.broadcasted_iota(jnp.int32, (nk, LANES), 1)
    stats = jnp.zeros((BAND, LANES), F32)
    for g0 in range(0, C_HEADS // 2, group):
        pairs = range(g0, g0 + group)
        scores = []
        for pair in pairs:
            lanes = slice(pair * LANES, (pair + 1) * LANES)
            q2, k2 = q_at[rows, lanes], keys(k_ref, lanes)
            for half in range(2):
                in_head = (lane >= HEAD) if half else (lane < HEAD)
                scores.append(_mm(jnp.where(in_head, q2, jnp.zeros_like(q2)), k2, NT))
        probs = []
        for i, s in enumerate(scores):
            s = jnp.where(mask, s, NEG)
            m = jnp.max(s, axis=-1, keepdims=True)
            probs.append(jnp.exp(s - m).astype(BF16))
            stats = jnp.where(lane == (i % 2) * (C_HEADS // 2) + g0 + i // 2, m, stats)
        for i, pair in enumerate(pairs):
            lanes = slice(pair * LANES, (pair + 1) * LANES)
            v2 = keys(v_ref, lanes)
            lo = _mm(probs[2 * i], jnp.where(lane_k < HEAD, v2, jnp.ones_like(v2)))
            hi = _mm(probs[2 * i + 1], jnp.where(lane_k >= HEAD, v2, jnp.ones_like(v2)))
            acc_at[rows, lanes] = jnp.where(lane < HEAD, lo, hi).astype(acc_at.dtype)
            stats = jnp.where(lane == DEN_EVEN + pair, lo, stats)
            stats = jnp.where(lane == DEN_ODD + pair, hi, stats)
    st_at[rows, :] = stats


def _band_mask(first_block):
    qi = lax.broadcasted_iota(jnp.int32, (BAND, 2 * BAND), 0)
    kc = lax.broadcasted_iota(jnp.int32, (BAND, 2 * BAND), 1)
    return (kc >= qi) & (kc <= qi + BAND) & (kc >= jnp.where(first_block, BAND, 0))


def _band_attn_body(q_ref, k_ref, v_ref, acc_ref, st_ref, *, nblk, units):
    for j in range(units):
        if nblk == 1:
            qi = lax.broadcasted_iota(jnp.int32, (BAND, BAND), 0)
            kc = lax.broadcasted_iota(jnp.int32, (BAND, BAND), 1)
            _band_block(q_ref.at[j], slice(None), lambda ref, lanes: ref[j, :, lanes], k_ref, v_ref, kc <= qi,
                        acc_ref.at[j], st_ref.at[j], 8)
        else:
            blk = pl.program_id(2) * units + j
            cur = pl.multiple_of(blk * BAND, BAND)
            prev = pl.multiple_of(jnp.maximum(blk - 1, 0) * BAND, BAND)
            keys = lambda ref, lanes: jnp.concatenate([ref[pl.ds(prev, BAND), lanes], ref[pl.ds(cur, BAND), lanes]], axis=0)
            _band_block(q_ref, slice(j * BAND, (j + 1) * BAND), keys, k_ref, v_ref, _band_mask(blk == 0),
                        acc_ref, st_ref, 4)


def _band_attn(q, k, v, units=4):
    nseq, dil, sub, _ = q.shape
    nblk = sub // BAND
    if nblk == 1:
        grid = (nseq, dil // units, 1)
        blk = lambda width: pl.BlockSpec((None, units, sub, width), lambda b, r, n: (b, r, 0, 0))
        cls = blk(D)
    else:
        grid = (nseq, dil, nblk // units)
        blk = lambda width: pl.BlockSpec((None, None, units * BAND, width), lambda b, r, n: (b, r, n, 0))
        cls = pl.BlockSpec((None, None, sub, D), lambda b, r, n: (b, r, 0, 0))
    return pl.pallas_call(
        functools.partial(_band_attn_body, nblk=nblk, units=units),
        grid=grid,
        in_specs=[blk(D), cls, cls],
        out_specs=[blk(D), blk(LANES)],
        out_shape=[jax.ShapeDtypeStruct(q.shape, q.dtype), jax.ShapeDtypeStruct((nseq, dil, sub, LANES), F32)],
        compiler_params=_cparams("parallel", "parallel", "arbitrary"),
        name=f"band_attn_d{dil}",
    )(q, k, v)


def _mix_body(a1_ref, a4_ref, a16_ref, s1_ref, s4_ref, s16_ref, e_ref, w_ref, x_ref, gp_ref, gate_ref, o_ref,
              slab4, slab16, st4, st16, o_scr):
    tm = x_ref.shape[0]
    ngrp = D // LANES
    for dil, a_ref, s_ref, slab, st in ((4, a4_ref, s4_ref, slab4, st4), (16, a16_ref, s16_ref, slab16, st16)):
        for r in range(dil):
            rows = pl.ds(r, tm // dil, stride=dil)
            st[rows, :] = s_ref[r]
            for grp in range(ngrp):
                slab[grp, rows, :] = a_ref[r, :, grp * LANES:(grp + 1) * LANES].astype(F32)
    sts = [s1_ref[...], st4[...], st16[...]]
    lane = lax.broadcasted_iota(jnp.int32, sts[0].shape, 1)
    top = jnp.maximum(jnp.maximum(sts[0], sts[1]), sts[2])
    es = [jnp.exp(st - top) for st in sts]
    dens = [jnp.where(lane < C_HEADS // 2, pltpu.roll(st, LANES - DEN_EVEN, 1),
                      pltpu.roll(st, LANES - DEN_ODD + C_HEADS // 2, 1)) for st in sts]
    total = es[0] * dens[0] + es[1] * dens[1] + es[2] * dens[2]
    coefs = [_mm(jnp.where(lane < C_HEADS, e / total, 0.0), e_ref[...], passes=2) for e in es]
    for grp in range(ngrp):
        lanes = slice(grp * LANES, (grp + 1) * LANES)
        o = (coefs[0][:, lanes] * a1_ref[:, lanes].astype(F32) + coefs[1][:, lanes] * slab4[grp]
             + coefs[2][:, lanes] * slab16[grp])
        o_scr[:, lanes] = o.astype(BF16)
    y = _mm(o_scr[...], w_ref[...])
    o_ref[...] = x_ref[...] + gate_ref[...] * (_rms(y) * gp_ref[...])


def _mix_outproj(accs, sts, expand, w, x, gp, mods, layer, tm):
    r = x.shape[0]
    nseq = accs[1].shape[0]
    tps = (r // nseq) // tm
    cls_spec = lambda dil, width: pl.BlockSpec((None, dil, tm // dil, width), lambda i: (i // tps, 0, i % tps, 0))
    return pl.pallas_call(
        _mix_body,
        grid=(r // tm,),
        in_specs=[_row_spec(tm, D), cls_spec(4, D), cls_spec(16, D), _row_spec(tm, LANES), cls_spec(4, LANES),
                  cls_spec(16, LANES), _const_spec(expand.shape), _const_spec(w.shape), _row_spec(tm, D),
                  _const_spec((1, D)), _mod_spec(mods, layer, 2, tps)],
        out_specs=_row_spec(tm, D),
        out_shape=jax.ShapeDtypeStruct((r, D), F32),
        scratch_shapes=[pltpu.VMEM((D // LANES, tm, LANES), F32), pltpu.VMEM((D // LANES, tm, LANES), F32),
                        pltpu.VMEM((tm, LANES), F32), pltpu.VMEM((tm, LANES), F32), pltpu.VMEM((tm, D), BF16)],
        compiler_params=_cparams("parallel"),
        name="attn_mix_outproj",
    )(*accs, *sts, expand, w, x, gp, mods)


def _attn_sample_body(q_ref, kn_ref, vn_ref, kc_ref, vc_ref, o_ref, *, t_len, window):
    tq = lax.broadcasted_iota(jnp.int32, (t_len, window), 0)
    back = window + tq - lax.broadcasted_iota(jnp.int32, (t_len, window), 1)
    cnt = (jnp.where(back <= BAND, 1.0, 0.0) + jnp.where((back <= 4 * BAND) & ((back & 3) == 0), 1.0, 0.0)
           + jnp.where((back & 15) == 0, 1.0, 0.0))
    tn = lax.broadcasted_iota(jnp.int32, (t_len, t_len), 0)
    dn = tn - lax.broadcasted_iota(jnp.int32, (t_len, t_len), 1)
    mult = jnp.where(dn == 0, 3.0, 0.0) + jnp.where(dn > 0, 1.0, 0.0) + jnp.where(dn == 4, 1.0, 0.0)
    lane = lax.broadcasted_iota(jnp.int32, (t_len, LANES), 1)
    for pair in range(q_ref.shape[1] // LANES):
        lanes = slice(pair * LANES, (pair + 1) * LANES)
        q2, kn, vn = q_ref[:, lanes], kn_ref[:, lanes].astype(BF16), vn_ref[:, lanes].astype(BF16)
        kc, vc = kc_ref[lanes, :].astype(BF16), vc_ref[lanes, :].astype(BF16)
        outs = []
        for half in range(2):
            qh = jnp.where((lane >= HEAD) if half else (lane < HEAD), q2, 0.0).astype(BF16)
            s_c = jnp.where(cnt > 0.0, _mm(qh, kc), NEG)
            s_n = jnp.where(mult > 0.0, _mm(qh, kn, NT), NEG)
            top = jnp.maximum(jnp.max(s_c, axis=-1, keepdims=True), jnp.max(s_n, axis=-1, keepdims=True))
            p_c = jnp.exp(s_c - top) * cnt
            p_n = jnp.exp(s_n - top) * mult
            den = jnp.sum(p_c, axis=-1, keepdims=True) + jnp.sum(p_n, axis=-1, keepdims=True)
            outs.append((_mm(p_c, vc, NT) + _mm(p_n, vn)) / den)
        o_ref[:, lanes] = jnp.where(lane < HEAD, outs[0], outs[1])


def _attn_sample(q, kn, vn, cache_k, cache_v, layer, nseq):
    window = cache_k.shape[-1]
    t_len = q.shape[0] // nseq
    assert window == 16 * BAND and t_len == 8
    width = D
    new = pl.BlockSpec((t_len, width), lambda b, g: (b, g))
    cached = pl.BlockSpec((None, None, width, window), lambda b, g: (layer, b, g, 0))
    return pl.pallas_call(
        functools.partial(_attn_sample_body, t_len=t_len, window=window),
        grid=(nseq, D // width),
        in_specs=[new, new, new, cached, cached],
        out_specs=new,
        out_shape=jax.ShapeDtypeStruct(q.shape, F32),
        compiler_params=_cparams("parallel", "parallel"),
        name="attn_sample",
    )(q, kn, vn, cache_k, cache_v)


def kernel(x_prompt, x_sample, state_wkv, state_shift, cache_k, cache_v, c_prompt, c_sample, ada_w, ada_b, norm_mix_pre, norm_mix_post, norm_ffn_pre, norm_ffn_post, ffn_w1, ffn_w2, ev_w_in, ev_w_out, gm_ln_g, gm_ln_b, gm_ws, gm_bs, rw_mu, rw_w0, rw_w2, rw_a0, rw_a2, rw_g2, rw_kk, rw_ka, rw_rk, rw_lnx_g, rw_lnx_b, od_w_qkv, od_w_out):
    n_p, s_len, _ = x_prompt.shape
    n_s, t_len, _ = x_sample.shape
    r_p, r_s = n_p * s_len, n_s * t_len
    row = lambda t: t.reshape(1, -1).astype(F32)

    mods = _ada(jnp.concatenate([c_prompt, c_sample], axis=0).astype(F32), ada_w, ada_b)
    mods_p = mods[:, :n_p].reshape(DEPTH, n_p, 1, 6 * D)
    mods_s = jnp.repeat(mods[:, n_p:], t_len, axis=1).reshape(DEPTH, 1, r_s, 6 * D)

    xp = x_prompt.astype(F32).reshape(r_p, D)
    xs = x_sample.astype(F32).reshape(r_s, D)

    hsum = jnp.kron(jnp.eye(LANES // HEAD, dtype=F32), jnp.ones((HEAD, HEAD), F32)).astype(BF16)
    tile_head = 2 * (jnp.arange(LANES) % (C_HEADS // 2)) + jnp.arange(LANES) // (C_HEADS // 2)
    expand = ((tile_head[:, None] == jnp.arange(D)[None, :] // HEAD) & (jnp.arange(LANES)[:, None] < C_HEADS)).astype(BF16)
    tabs_p = _rope_tables(jnp.arange(s_len))
    tabs_s = _rope_tables(jnp.tile(PAST_LEN + jnp.arange(t_len), n_s))
    zeros_prev = jnp.zeros((n_p, 1, B_COLS), F32)
    zeros_state = jnp.zeros((n_p, B_HEADS, HEAD, HEAD), F32)
    reps = CHUNK // t_len
    feat_major = lambda c: jnp.transpose(c.astype(F32), (0, 1, 3, 4, 2)).reshape(c.shape[0], c.shape[1], D, c.shape[2])
    cache_kt, cache_vt = feat_major(cache_k), feat_major(cache_v)
    tok_major = lambda t: jnp.transpose(t.reshape(n_p, C_HEADS, HEAD, s_len), (0, 3, 1, 2))

    tm_p = 512
    outs = {k: [] for k in ("wkv_p", "shift_p", "k_p", "v_p", "wkv_s", "shift_s", "k_s", "v_s", "gv_s")}
    for l in range(DEPTH):
        g_pre, g_post = row(norm_mix_pre[l]), row(norm_mix_post[l])
        if l % 2 == 0:
            e = l // 2
            w_in = ev_w_in[e].astype(BF16)
            w_out = ev_w_out[e].astype(BF16)
            w_oa, w_ob = w_out[:A_WIDTH], w_out[A_WIDTH:]
            ln_g, ln_b = row(gm_ln_g[e]), row(gm_ln_b[e])
            ws = gm_ws[e].astype(F32)
            bias_p = jnp.repeat(gm_bs[e].T, A_WIDTH // A_GROUPS, axis=1)
            ws_s = jax.vmap(lambda m: jnp.kron(jnp.eye(reps, dtype=F32), m))(ws[:, :t_len, :t_len])
            bias_s = jnp.tile(bias_p[:t_len], (reps, 1))
            mu = row(rw_mu[e])
            zpad = jnp.zeros((7 * B_WIDTH,), F32)
            prm = jnp.concatenate([rw_w0[e], rw_a0[e], rw_kk[e], rw_ka[e], rw_rk[e].reshape(-1), rw_lnx_g[e],
                                   rw_lnx_b[e], zpad[:B_WIDTH]]).reshape(8, B_WIDTH).astype(F32)
            zl = lambda a, b: jnp.zeros((a, b), F32)
            lora = jnp.stack([
                jnp.concatenate([rw_w2[e], zl(96, B_WIDTH)], axis=0),
                jnp.concatenate([zl(32, B_WIDTH), rw_a2[e], zl(64, B_WIDTH)], axis=0),
                jnp.concatenate([zl(64, B_WIDTH), rw_g2[e]], axis=0)]).astype(BF16)

            oa_p, pb_p = _inproj(xp, g_pre, mods_p, l, tm_p, w_in, ln_g, ln_b, ws, bias_p, False)
            pb_p3 = pb_p.reshape(n_p, s_len, B_COLS)
            ob_p, st_p = _rwkv(pb_p3, zeros_prev, zeros_state, mu, prm, lora, hsum, 2, 128, 64)
            xp = _outproj([oa_p, ob_p.reshape(r_p, B_WIDTH)], [w_oa, w_ob], xp, g_post, mods_p, l, tm_p)
            outs["wkv_p"].append(st_p)
            outs["shift_p"].append(pb_p3[:, -1])

            oa_s, pb_s, gv = _inproj(xs, g_pre, mods_s, l, r_s, w_in, ln_g, ln_b, ws_s, bias_s, True)
            pb_s3 = pb_s.reshape(n_s, t_len, B_COLS)
            ob_s, st_s = _rwkv(pb_s3, state_shift[e].astype(F32).reshape(n_s, 1, B_COLS), state_wkv[e].astype(F32),
                               mu, prm, lora, hsum, 8, t_len, t_len)
            xs = _outproj([oa_s, ob_s.reshape(r_s, B_WIDTH)], [w_oa, w_ob], xs, g_post, mods_s, l, r_s)
            outs["wkv_s"].append(st_s)
            outs["shift_s"].append(pb_s3[:, -1])
            outs["gv_s"].append(gv.reshape(n_s, t_len, A_WIDTH))
        else:
            o = l // 2
            w_qkv = od_w_qkv[o].astype(BF16)
            w_out = od_w_out[o].astype(BF16)
            q4, k4, v4, q16, k16, v16, kf, vf, acc1, st1 = _qkv_classes(xp, g_pre, mods_p, l, 256, w_qkv, tabs_p, n_p)
            res = [_band_attn(q4, k4, v4), _band_attn(q16, k16, v16)]
            accs = [acc1, res[0][0], res[1][0]]
            sts = [st1, res[0][1], res[1][1]]
            xp = _mix_outproj(accs, sts, expand, w_out, xp, g_post, mods_p, l, 512)
            outs["k_p"].append(tok_major(kf))
            outs["v_p"].append(tok_major(vf))

            qs, kfs, vfs = _qkv(xs, g_pre, mods_s, l, r_s, w_qkv, tabs_s)
            heads = lambda t: t.reshape(n_s, t_len, C_HEADS, HEAD)
            att = _attn_sample(qs, kfs, vfs, cache_kt, cache_vt, o, n_s)
            xs = _outproj([att], [w_out], xs, g_post, mods_s, l, r_s)
            outs["k_s"].append(heads(kfs))
            outs["v_s"].append(heads(vfs))
        w1, w2 = ffn_w1[l].astype(BF16), ffn_w2[l].astype(BF16)
        g_fpre, g_fpost = row(norm_ffn_pre[l]), row(norm_ffn_post[l])
        xp = _ffn(xp, g_fpre, g_fpost, mods_p, l, 512, 1024, w1, w2)
        xs = _ffn(xs, g_fpre, g_fpost, mods_s, l, r_s, 1024, w1, w2)

    st = lambda name: jnp.stack(outs[name])
    return (xp.reshape(n_p, s_len, D), xs.reshape(n_s, t_len, D), st("wkv_p"), st("shift_p"), st("k_p"),
            st("v_p"), st("wkv_s"), st("shift_s"), st("k_s"), st("v_s"), st("gv_s"))
```

```python
import functools
import math

import jax
import jax.numpy as jnp
from jax import lax
from jax.experimental import pallas as pl
from jax.experimental.pallas import tpu as pltpu

F32 = jnp.float32
BF16 = jnp.bfloat16

D = 1024
DEPTH = 4
D_FF = 4 * D
A_WIDTH = 512
A_GROUPS = 4
CHUNK = 128
B_WIDTH = 512
B_HEADS = 8
HEAD = 64
LORA = 128
B_COLS = 3 * B_WIDTH + LORA
C_HEADS = 16
ROT_DIM = 16
ROPE_THETA = 500000.0
DILATIONS = (1, 4, 16)
BAND = 128
PAST_LEN = 8192
NORM_EPS = 1e-6
LN_EPS = 1e-5
GN_EPS = 64e-5
NEG = -1e30

LANES = 128
DEN_EVEN, DEN_ODD = 64, 16
VMEM_LIMIT = 52 * 1024 * 1024


def _cparams(*sem):
    return pltpu.CompilerParams(dimension_semantics=sem, vmem_limit_bytes=VMEM_LIMIT)


def _split(x):
    hi = x.astype(BF16)
    return hi, (x - hi.astype(F32)).astype(BF16)


def _mm(a, b, dims=((1,), (0,)), passes=1):
    dn = (dims, ((), ()))
    dot = lambda p, q: lax.dot_general(p, q, dn, preferred_element_type=F32)
    if passes == 1:
        return dot(a.astype(BF16), b.astype(BF16))
    ah, al = _split(a)
    return dot(ah, b) + dot(al, b)


NT = ((1,), (1,))
TN = ((0,), (0,))


def _rms(x):
    return x * lax.rsqrt(jnp.mean(x * x, axis=-1, keepdims=True) + NORM_EPS)


def _mod_spec(mods, layer, chunk, tiles_per_group):
    return pl.BlockSpec((None, None, mods.shape[2], D), lambda i, *_: (layer, i // tiles_per_group, 0, chunk))


def _row_spec(tm, n):
    return pl.BlockSpec((tm, n), lambda i, *_: (i, 0))


def _layer_spec(w, layer, rows=None, row_block=0, **kw):
    rows = w.shape[1] if rows is None else rows
    return pl.BlockSpec((None, rows, w.shape[2]), lambda *_: (layer, row_block, 0), **kw)


def _const_spec(shape):
    nd = len(shape)
    return pl.BlockSpec(shape, lambda *_: (0,) * nd)


def _ada_body(c_ref, w_ref, b_ref, o_ref):
    c = c_ref[...]
    o_ref[...] = _mm(c * jax.nn.sigmoid(c), w_ref[...]) + b_ref[...]


def _ada(c_all, ada_w, ada_b):
    n = c_all.shape[0]
    tn = 1536
    return pl.pallas_call(
        _ada_body,
        grid=(DEPTH, 6 * D // tn),
        in_specs=[pl.BlockSpec((n, D), lambda l, j: (0, 0)),
                  pl.BlockSpec((None, D, tn), lambda l, j: (l, 0, j)),
                  pl.BlockSpec((None, 1, tn), lambda l, j: (l, 0, j))],
        out_specs=pl.BlockSpec((None, n, tn), lambda l, j: (l, 0, j)),
        out_shape=jax.ShapeDtypeStruct((DEPTH, n, 6 * D), F32),
        compiler_params=_cparams("parallel", "parallel"),
        name="ada",
    )(c_all, ada_w, ada_b.reshape(DEPTH, 1, 6 * D))


def _modnorm(x, g, sh, sc):
    return _rms(x) * g * (1.0 + sc) + sh


def _gelu(x):
    return 0.5 * x * (1.0 + lax.erf(x * (2.0 ** -0.5)))


def _inproj_body(x_ref, g_ref, sh_ref, sc_ref, w_ref, lng_ref, lnb_ref, ws_ref, bias_ref, oa_ref, pb_ref, *v_refs):
    h = _modnorm(x_ref[...], g_ref[...], sh_ref[...], sc_ref[...]).astype(BF16)
    pb_ref[...] = _mm(h, w_ref[:, 2 * A_WIDTH:])
    u = _gelu(_mm(h, w_ref[:, :A_WIDTH]))
    v = _gelu(_mm(h, w_ref[:, A_WIDTH:2 * A_WIDTH]))
    vc = v - jnp.mean(v, axis=-1, keepdims=True)
    var = jnp.mean(vc * vc, axis=-1, keepdims=True)
    vn = vc * lax.rsqrt(var + LN_EPS) * lng_ref[...] + lnb_ref[...]
    if v_refs:
        v_refs[0][...] = vn
    ii = lax.broadcasted_iota(jnp.int32, (CHUNK, CHUNK), 0)
    jj = lax.broadcasted_iota(jnp.int32, (CHUNK, CHUNK), 1)
    cg = A_WIDTH // A_GROUPS
    for grp in range(A_GROUPS):
        lanes = slice(grp * cg, (grp + 1) * cg)
        wc = jnp.where(jj <= ii, ws_ref[grp], 0.0).astype(BF16)
        for ci in range(x_ref.shape[0] // CHUNK):
            rows = slice(ci * CHUNK, (ci + 1) * CHUNK)
            z = _mm(wc, vn[rows, lanes]) + bias_ref[:, lanes]
            oa_ref[rows, lanes] = (u[rows, lanes] * z).astype(oa_ref.dtype)


def _inproj(x, g, mods, layer, tm, w, e, ln_g, ln_b, ws, bias, emit_v):
    r = x.shape[0]
    tpg = (r // mods.shape[1]) // tm
    half = lambda dtype: jax.ShapeDtypeStruct((r, A_WIDTH), dtype)
    return pl.pallas_call(
        _inproj_body,
        grid=(r // tm,),
        in_specs=[_row_spec(tm, D), _const_spec((1, D)), _mod_spec(mods, layer, 0, tpg),
                  _mod_spec(mods, layer, 1, tpg), _layer_spec(w, e), _const_spec((1, A_WIDTH)),
                  _const_spec((1, A_WIDTH)), _const_spec(ws.shape), _const_spec(bias.shape)],
        out_specs=[_row_spec(tm, A_WIDTH), _row_spec(tm, B_COLS)] + ([_row_spec(tm, A_WIDTH)] if emit_v else []),
        out_shape=[half(BF16), jax.ShapeDtypeStruct((r, B_COLS), F32)] + ([half(F32)] if emit_v else []),
        compiler_params=_cparams("parallel"),
        name="even_inproj_gmlp",
    )(x, g, mods, mods, w, ln_g, ln_b, ws, bias)


def _rope(seg, cos, s_lo, s_hi):
    return seg * cos + pltpu.roll(seg, LANES - ROT_DIM // 2, 1) * s_lo + pltpu.roll(seg, ROT_DIM // 2, 1) * s_hi


def _qkv_groups(x_ref, g_ref, sh_ref, sc_ref, w_ref, cos_ref, slo_ref, shi_ref):
    h = _modnorm(x_ref[...], g_ref[...], sh_ref[...], sc_ref[...]).astype(BF16)
    cos, s_lo, s_hi = cos_ref[...], slo_ref[...], shi_ref[...]
    wide = 2 * LANES
    for pair in range(D // wide):
        q2, k2, v2 = (_mm(h, w_ref[:, part * D + pair * wide:part * D + (pair + 1) * wide]) for part in range(3))
        for half in range(2):
            part = slice(half * LANES, (half + 1) * LANES)
            grp = 2 * pair + half
            yield (slice(grp * LANES, (grp + 1) * LANES), _rope(q2[:, part], cos, s_lo, s_hi) * HEAD ** -0.5,
                   _rope(k2[:, part], cos, s_lo, s_hi), v2[:, part])


def _qkv_body(*refs):
    q_ref, kf_ref, vf_ref = refs[8:]
    for lanes, q, k, v in _qkv_groups(*refs[:8]):
        q_ref[:, lanes] = q
        kf_ref[:, lanes] = k
        vf_ref[:, lanes] = v


def _qkv_classes_body(*refs, tps):
    q4, k4, v4, q16, k16, v16, kf_ref, vf_ref, acc_ref, st_ref, slab, slab4, q_scr, k_scr, v_scr = refs[8:]
    tm = q_scr.shape[0]
    first_tile = pl.program_id(0) % tps == 0

    @pl.when(first_tile)
    def _():
        k_scr[:BAND, :] = jnp.zeros((BAND, D), BF16)
        v_scr[:BAND, :] = jnp.zeros((BAND, D), BF16)

    @pl.when(jnp.logical_not(first_tile))
    def _():
        k_scr[:BAND, :] = k_scr[tm:, :]
        v_scr[:BAND, :] = v_scr[tm:, :]

    for lanes, q, k, v in _qkv_groups(*refs[:8]):
        kf_ref[lanes, :] = k.T
        vf_ref[lanes, :] = v.T
        q_scr[:, lanes] = q.astype(BF16)
        k_scr[BAND:, lanes] = k.astype(BF16)
        v_scr[BAND:, lanes] = v.astype(BF16)
        for i, (val, o4, o16) in enumerate(((q, q4, q16), (k, k4, k16), (v, v4, v16))):
            slab[i] = val
            for r4 in range(4):
                cls = slab[i, pl.ds(r4, tm // 4, stride=4), :]
                o4[r4, :, lanes] = cls.astype(BF16)
                slab4[i, r4 * (tm // 4):(r4 + 1) * (tm // 4), :] = cls
            for r16 in range(16):
                start = (r16 % 4) * (tm // 4) + r16 // 4
                o16[r16, :, lanes] = slab4[i, pl.ds(start, tm // 16, stride=4), :].astype(BF16)
    for j in range(tm // BAND):
        keys = lambda ref, lanes: ref[j * BAND:(j + 2) * BAND, lanes]
        mask = _band_mask(first_tile) if j == 0 else _band_mask(False)
        _band_block(q_scr, slice(j * BAND, (j + 1) * BAND), keys, k_scr, v_scr, mask, acc_ref, st_ref, 4)


def _rope_tables(pos):
    half = ROT_DIM // 2
    inv = ROPE_THETA ** (-jnp.arange(half, dtype=F32) * (2.0 / ROT_DIM))
    ang = pos.astype(F32)[:, None] * inv[None, :]
    cos, sin = jnp.cos(ang), jnp.sin(ang)
    t = pos.shape[0]
    one, zero, z8 = jnp.ones((t, HEAD - ROT_DIM), F32), jnp.zeros((t, HEAD - ROT_DIM), F32), jnp.zeros((t, half), F32)
    tab = lambda *parts: jnp.tile(jnp.concatenate(parts, axis=1), (1, LANES // HEAD))
    return tab(cos, cos, one), tab(-sin, z8, zero), tab(z8, sin, zero)


def _qkv_in_specs(mods, layer, tm, tpg, tables_per_seq, w, o):
    tab_spec = pl.BlockSpec((tm, LANES), lambda i: (i % tables_per_seq, 0))
    return [_row_spec(tm, D), _const_spec((1, D)), _mod_spec(mods, layer, 0, tpg), _mod_spec(mods, layer, 1, tpg),
            _layer_spec(w, o), tab_spec, tab_spec, tab_spec]


def _qkv(x, g, mods, layer, tm, w, o, tables):
    r = x.shape[0]
    tpg = (r // mods.shape[1]) // tm
    f32 = jax.ShapeDtypeStruct((r, D), F32)
    return pl.pallas_call(
        _qkv_body,
        grid=(r // tm,),
        in_specs=_qkv_in_specs(mods, layer, tm, tpg, 1, w, o),
        out_specs=[_row_spec(tm, D)] * 3,
        out_shape=[f32, f32, f32],
        compiler_params=_cparams("parallel"),
        name="odd_qkv_decode",
    )(x, g, mods, mods, w, *tables)


def _qkv_classes(x, g, mods, layer, tm, w, o, tables, nseq):
    r = x.shape[0]
    s_len = r // nseq
    tps = s_len // tm
    cls_spec = lambda dil: pl.BlockSpec((None, dil, tm // dil, D), lambda i: (i // tps, 0, i % tps, 0))
    cls_shape = lambda dil: jax.ShapeDtypeStruct((nseq, dil, s_len // dil, D), BF16)
    f32 = jax.ShapeDtypeStruct((nseq, D, s_len), F32)
    f32_spec = pl.BlockSpec((None, D, tm), lambda i: (i // tps, 0, i % tps))
    return pl.pallas_call(
        functools.partial(_qkv_classes_body, tps=tps),
        grid=(r // tm,),
        in_specs=_qkv_in_specs(mods, layer, tm, tps, tps, w, o),
        out_specs=[cls_spec(4)] * 3 + [cls_spec(16)] * 3 + [f32_spec] * 2 + [_row_spec(tm, D), _row_spec(tm, LANES)],
        out_shape=[cls_shape(4)] * 3 + [cls_shape(16)] * 3 + [f32, f32, jax.ShapeDtypeStruct((r, D), BF16),
                                                                jax.ShapeDtypeStruct((r, LANES), F32)],
        scratch_shapes=[pltpu.VMEM((3, tm, LANES), F32), pltpu.VMEM((3, tm, LANES), F32), pltpu.VMEM((tm, D), BF16),
                        pltpu.VMEM((BAND + tm, D), BF16), pltpu.VMEM((BAND + tm, D), BF16)],
        compiler_params=_cparams("arbitrary"),
        name="odd_qkv_attn1",
    )(x, g, mods, mods, w, *tables)


def _outproj_body(*refs, n_in):
    ins, ws = refs[:n_in], refs[n_in:2 * n_in]
    x_ref, gp_ref, gate_ref, o_ref = refs[2 * n_in:]
    y = _mm(ins[0][...], ws[0][...])
    for a, w in zip(ins[1:], ws[1:]):
        y = y + _mm(a[...], w[...])
    o_ref[...] = x_ref[...] + gate_ref[...] * (_rms(y) * gp_ref[...])


def _outproj(ins, ws, x, gp, mods, layer, tm):
    r = x.shape[0]
    tpg = (r // mods.shape[1]) // tm
    return pl.pallas_call(
        functools.partial(_outproj_body, n_in=len(ins)),
        grid=(r // tm,),
        in_specs=[_row_spec(tm, a.shape[1]) for a in ins]
        + [_layer_spec(w, l, a.shape[1], b) for a, (w, l, b) in zip(ins, ws)]
        + [_row_spec(tm, D), _const_spec((1, D)), _mod_spec(mods, layer, 2, tpg)],
        out_specs=_row_spec(tm, D),
        out_shape=jax.ShapeDtypeStruct((r, D), F32),
        compiler_params=_cparams("parallel"),
        name="outproj",
    )(*ins, *(w for w, _, _ in ws), x, gp, mods)


def _ffn_body(x_ref, g_ref, sh_ref, sc_ref, gate_ref, gp_ref, w1_ref, w2_ref, o_ref, t_scr, *, fc):
    x = x_ref[...]
    h = _modnorm(x, g_ref[...], sh_ref[...], sc_ref[...]).astype(BF16)
    for c in range(D_FF // fc):
        cols = slice(c * fc, (c + 1) * fc)
        t = jnp.maximum(_mm(h, w1_ref[:, cols]), 0.0)
        t_scr[:, cols] = (t * t).astype(BF16)
    o_ref[...] = x + gate_ref[...] * (_rms(_mm(t_scr[...], w2_ref[...])) * gp_ref[...])


def _ffn(x, g, gp, mods, layer, tm, fc, w1, w2):
    r = x.shape[0]
    tpg = (r // mods.shape[1]) // tm
    resident = lambda w: _layer_spec(w, layer, pipeline_mode=pl.Buffered(1))
    return pl.pallas_call(
        functools.partial(_ffn_body, fc=fc),
        grid=(r // tm,),
        in_specs=[_row_spec(tm, D), _const_spec((1, D)), _mod_spec(mods, layer, 3, tpg),
                  _mod_spec(mods, layer, 4, tpg), _mod_spec(mods, layer, 5, tpg), _const_spec((1, D)),
                  resident(w1), resident(w2)],
        out_specs=_row_spec(tm, D),
        out_shape=jax.ShapeDtypeStruct((r, D), F32),
        scratch_shapes=[pltpu.VMEM((tm, D_FF), BF16)],
        compiler_params=_cparams("parallel"),
        name="ffn",
    )(x, g, mods, mods, mods, gp, w1, w2)


def _seg_cumsum(x, seg):
    row = lax.broadcasted_iota(jnp.int32, (x.shape[0], 1), 0) % seg
    s = 1
    while s < seg:
        x = x + jnp.where(row >= s, pltpu.roll(x, s, 0), 0.0)
        s *= 2
    return x


def _rwkv_body(pb_ref, prev_ref, s0_ref, mu_ref, prm_ref, lora_ref, hsum_ref, out_ref, s_ref, last_scr,
               *, nb, tt, c):
    @pl.when(pl.program_id(1) == 0)
    def _():
        s_ref[...] = s0_ref[...]
        last_scr[...] = prev_ref[...]

    mu = mu_ref[...]
    w0, a0, k_k, k_a, r_k, ln_g, ln_b = (prm_ref[i:i + 1, :] for i in range(7))
    hsum = hsum_ref[...]

    def headsum(t):
        groups = [_mm(t[:, g * LANES:(g + 1) * LANES], hsum, passes=2) for g in range(B_WIDTH // LANES)]
        return jnp.concatenate(groups, axis=1)

    row = lax.broadcasted_iota(jnp.int32, (tt, 1), 0)
    ii = lax.broadcasted_iota(jnp.int32, (c, 2 * c), 0)
    jj = lax.broadcasted_iota(jnp.int32, (c, 2 * c), 1) % c
    strict, incl = jj < ii, jj <= ii
    eye = jnp.where(lax.broadcasted_iota(jnp.int32, (c, c), 0) == lax.broadcasted_iota(jnp.int32, (c, c), 1), 1.0, 0.0).astype(F32)
    zeros_c = jnp.zeros((c, HEAD), BF16)
    n_chunk = tt // c
    levels = int(math.log2(c)) - 1

    seqs = []
    for s in range(nb):
        x = pb_ref[s]
        shifted = jnp.where(row == 0, last_scr[s], pltpu.roll(x, 1, 0))
        last_scr[s] = x[tt - 1:tt, :]
        xm = x + mu * (shifted - x)
        r, k, v = xm[:, :B_WIDTH], xm[:, B_WIDTH:2 * B_WIDTH], xm[:, 2 * B_WIDTH:3 * B_WIDTH]
        lo = xm[:, 3 * B_WIDTH:]
        z = -(w0 + _mm(jnp.tanh(lo), lora_ref[0]))
        softplus = jnp.maximum(z, 0.0) + jnp.log1p(jnp.exp(-jnp.abs(z)))
        lw = -jnp.exp(-softplus - 0.5)
        a = jax.nn.sigmoid(a0 + _mm(lo, lora_ref[1]))
        g = _mm(jax.nn.sigmoid(lo), lora_ref[2])
        kk = k * k_k
        kk = kk * lax.rsqrt(headsum(kk * kk) + 1e-12)
        k2 = k * (1.0 + (a - 1.0) * k_a)
        lp = _seg_cumsum(lw, c)
        e_pos, e_neg = jnp.exp(lp), jnp.exp(-lp)
        seqs.append(dict(
            r=r, k2=k2, v=v, g=g, e_pos=e_pos,
            a_t=-kk * jnp.exp(lp - lw), b_t=kk * a * e_neg, k_t=k2 * e_neg, r_t=r * e_pos, v_b=v))

    def piece(s, name, ci, h):
        return seqs[s][name][ci * c:(ci + 1) * c, h * HEAD:(h + 1) * HEAD].astype(BF16)

    chains = [(s, h) for s in range(nb) for h in range(B_HEADS)]
    units = [(ci, s, h) for ci in range(n_chunk) for (s, h) in chains]

    a_u = {u: piece(u[1], "a_t", u[0], u[2]) for u in units}
    r_u = {u: piece(u[1], "r_t", u[0], u[2]) for u in units}
    v_u = {u: piece(u[1], "v_b", u[0], u[2]) for u in units}
    ar_u = {u: jnp.concatenate([a_u[u], r_u[u]], axis=0) for u in units}
    bk_u = {u: jnp.concatenate([piece(u[1], "b_t", u[0], u[2]), piece(u[1], "k_t", u[0], u[2])], axis=0) for u in units}
    zv_u = {u: jnp.concatenate([zeros_c, v_u[u]], axis=0) for u in units}
    g12 = {u: _mm(ar_u[u], bk_u[u], NT) for u in units}
    g1 = {u: jnp.where(strict, g12[u][:c], 0.0).astype(BF16) for u in units}
    g2 = {u: jnp.where(incl, g12[u][c:], 0.0).astype(BF16) for u in units}
    n_u = {u: g1[u][:, :c] for u in units}
    inv = {u: eye + n_u[u].astype(F32) for u in units}
    pw = {u: _mm(n_u[u], n_u[u]) for u in units}
    for level in range(levels):
        if level + 1 < levels:
            both = {u: _mm(jnp.concatenate([pw[u], inv[u]], axis=0), pw[u]) for u in units}
            pw = {u: both[u][:c] for u in units}
            inv = {u: inv[u] + both[u][c:] for u in units}
        else:
            inv = {u: inv[u] + _mm(inv[u], pw[u]) for u in units}
    xn = {u: _mm(jnp.concatenate(_split(inv[u]), axis=0), n_u[u]) for u in units}
    res = {u: eye - inv[u] + (xn[u][:c] + xn[u][c:]) for u in units}
    inv = {u: inv[u] + _mm(res[u], inv[u]) for u in units}

    state = {ch: s_ref[ch[0], ch[1]] for ch in chains}
    o_u = {}
    for ci in range(n_chunk):
        us = [(ci, s, h) for (s, h) in chains]
        ars = {u: _mm(ar_u[u], state[u[1:]], NT) for u in us}
        rhs = {u: ars[u][:c] + _mm(g1[u], zv_u[u]) for u in us}
        uu = {u: _mm(inv[u], rhs[u]) for u in us}
        uv = {u: jnp.concatenate([uu[u].astype(BF16), v_u[u]], axis=0) for u in us}
        for u in us:
            o_u[u] = ars[u][c:] + _mm(g2[u], uv[u])
        for u in us:
            s, h = u[1:]
            p_end = seqs[s]["e_pos"][(ci + 1) * c - 1:(ci + 1) * c, h * HEAD:(h + 1) * HEAD]
            state[(s, h)] = (state[(s, h)] + _mm(uv[u], bk_u[u], TN)) * p_end
    for (s, h) in chains:
        s_ref[s, h] = state[(s, h)]

    for s in range(nb):
        q = seqs[s]
        o_rows = [jnp.concatenate([o_u[(ci, s, h)] for h in range(B_HEADS)], axis=1) for ci in range(n_chunk)]
        o = o_rows[0] if n_chunk == 1 else jnp.concatenate(o_rows, axis=0)
        inv_n = 1.0 / HEAD
        oc = o - headsum(o) * inv_n
        on = oc * lax.rsqrt(headsum(oc * oc) * inv_n + GN_EPS) * ln_g + ln_b
        bonus = headsum(q["r"] * q["k2"] * r_k) * q["v"]
        out_ref[s] = ((on + bonus) * q["g"]).astype(out_ref.dtype)


def _rwkv(pb, prev, s0, mu, prm, lora, hsum, nb, tt, c):
    n, t, _ = pb.shape
    return pl.pallas_call(
        functools.partial(_rwkv_body, nb=nb, tt=tt, c=c),
        grid=(n // nb, t // tt),
        in_specs=[pl.BlockSpec((nb, tt, B_COLS), lambda i, j: (i, j, 0)),
                  pl.BlockSpec((nb, 1, B_COLS), lambda i, j: (i, 0, 0)),
                  pl.BlockSpec((nb, B_HEADS, HEAD, HEAD), lambda i, j: (i, 0, 0, 0)),
                  _const_spec(mu.shape), _const_spec(prm.shape), _const_spec(lora.shape), _const_spec(hsum.shape)],
        out_specs=[pl.BlockSpec((nb, tt, B_WIDTH), lambda i, j: (i, j, 0)),
                   pl.BlockSpec((nb, B_HEADS, HEAD, HEAD), lambda i, j: (i, 0, 0, 0))],
        out_shape=[jax.ShapeDtypeStruct((n, t, B_WIDTH), BF16), jax.ShapeDtypeStruct(s0.shape, F32)],
        scratch_shapes=[pltpu.VMEM((nb, 1, B_COLS), F32)],
        compiler_params=_cparams("parallel", "arbitrary"),
        name="rwkv7",
    )(pb, prev, s0, mu, prm, lora, hsum)


def _band_block(q_at, rows, keys, k_ref, v_ref, mask, acc_at, st_at, group):
    nk = mask.shape[1]
    lane = lax.broadcasted_iota(jnp.int32, (BAND, LANES), 1)
    lane_k = lax.broadcasted_iota(jnp.int32, (nk, LANES), 1)
    stats = jnp.zeros((BAND, LANES), F32)
    for g0 in range(0, C_HEADS // 2, group):
        pairs = range(g0, g0 + group)
        scores = []
        for pair in pairs:
            lanes = slice(pair * LANES, (pair + 1) * LANES)
            q2, k2 = q_at[rows, lanes], keys(k_ref, lanes)
            for half in range(2):
                in_head = (lane >= HEAD) if half else (lane < HEAD)
                scores.append(_mm(jnp.where(in_head, q2, jnp.zeros_like(q2)), k2, NT))
        probs = []
        for i, s in enumerate(scores):
            s = jnp.where(mask, s, NEG)
            m = jnp.max(s, axis=-1, keepdims=True)
            probs.append(jnp.exp(s - m).astype(BF16))
            stats = jnp.where(lane == (i % 2) * (C_HEADS // 2) + g0 + i // 2, m, stats)
        for i, pair in enumerate(pairs):
            lanes = slice(pair * LANES, (pair + 1) * LANES)
            v2 = keys(v_ref, lanes)
            lo = _mm(probs[2 * i], jnp.where(lane_k < HEAD, v2, jnp.ones_like(v2)))
            hi = _mm(probs[2 * i + 1], jnp.where(lane_k >= HEAD, v2, jnp.ones_like(v2)))
            acc_at[rows, lanes] = jnp.where(lane < HEAD, lo, hi).astype(acc_at.dtype)
            stats = jnp.where(lane == DEN_EVEN + pair, lo, stats)
            stats = jnp.where(lane == DEN_ODD + pair, hi, stats)
    st_at[rows, :] = stats


def _band_mask(first_block):
    qi = lax.broadcasted_iota(jnp.int32, (BAND, 2 * BAND), 0)
    kc = lax.broadcasted_iota(jnp.int32, (BAND, 2 * BAND), 1)
    return (kc >= qi) & (kc <= qi + BAND) & (kc >= jnp.where(first_block, BAND, 0))


def _band_attn_body(q_ref, k_ref, v_ref, acc_ref, st_ref, *, nblk, units):
    for j in range(units):
        if nblk == 1:
            qi = lax.broadcasted_iota(jnp.int32, (BAND, BAND), 0)
            kc = lax.broadcasted_iota(jnp.int32, (BAND, BAND), 1)
            _band_block(q_ref.at[j], slice(None), lambda ref, lanes: ref[j, :, lanes], k_ref, v_ref, kc <= qi,
                        acc_ref.at[j], st_ref.at[j], 8)
        else:
            blk = pl.program_id(2) * units + j
            cur = pl.multiple_of(blk * BAND, BAND)
            prev = pl.multiple_of(jnp.maximum(blk - 1, 0) * BAND, BAND)
            keys = lambda ref, lanes: jnp.concatenate([ref[pl.ds(prev, BAND), lanes], ref[pl.ds(cur, BAND), lanes]], axis=0)
            _band_block(q_ref, slice(j * BAND, (j + 1) * BAND), keys, k_ref, v_ref, _band_mask(blk == 0),
                        acc_ref, st_ref, 4)


def _band_attn(q, k, v, units=4):
    nseq, dil, sub, _ = q.shape
    nblk = sub // BAND
    if nblk == 1:
        grid = (nseq, dil // units, 1)
        blk = lambda width: pl.BlockSpec((None, units, sub, width), lambda b, r, n: (b, r, 0, 0))
        cls = blk(D)
    else:
        grid = (nseq, dil, nblk // units)
        blk = lambda width: pl.BlockSpec((None, None, units * BAND, width), lambda b, r, n: (b, r, n, 0))
        cls = pl.BlockSpec((None, None, sub, D), lambda b, r, n: (b, r, 0, 0))
    return pl.pallas_call(
        functools.partial(_band_attn_body, nblk=nblk, units=units),
        grid=grid,
        in_specs=[blk(D), cls, cls],
        out_specs=[blk(D), blk(LANES)],
        out_shape=[jax.ShapeDtypeStruct(q.shape, q.dtype), jax.ShapeDtypeStruct((nseq, dil, sub, LANES), F32)],
        compiler_params=_cparams("parallel", "parallel", "arbitrary"),
        name=f"band_attn_d{dil}",
    )(q, k, v)


def _mix_body(a1_ref, a4_ref, a16_ref, s1_ref, s4_ref, s16_ref, e_ref, w_ref, x_ref, gp_ref, gate_ref, o_ref,
              slab4, slab16, st4, st16, o_scr):
    tm = x_ref.shape[0]
    ngrp = D // LANES
    for dil, a_ref, s_ref, slab, st in ((4, a4_ref, s4_ref, slab4, st4), (16, a16_ref, s16_ref, slab16, st16)):
        for r in range(dil):
            rows = pl.ds(r, tm // dil, stride=dil)
            st[rows, :] = s_ref[r]
            for grp in range(ngrp):
                slab[grp, rows, :] = a_ref[r, :, grp * LANES:(grp + 1) * LANES].astype(F32)
    sts = [s1_ref[...], st4[...], st16[...]]
    lane = lax.broadcasted_iota(jnp.int32, sts[0].shape, 1)
    top = jnp.maximum(jnp.maximum(sts[0], sts[1]), sts[2])
    es = [jnp.exp(st - top) for st in sts]
    dens = [jnp.where(lane < C_HEADS // 2, pltpu.roll(st, LANES - DEN_EVEN, 1),
                      pltpu.roll(st, LANES - DEN_ODD + C_HEADS // 2, 1)) for st in sts]
    total = es[0] * dens[0] + es[1] * dens[1] + es[2] * dens[2]
    coefs = [_mm(jnp.where(lane < C_HEADS, e / total, 0.0), e_ref[...], passes=2) for e in es]
    for grp in range(ngrp):
        lanes = slice(grp * LANES, (grp + 1) * LANES)
        o = (coefs[0][:, lanes] * a1_ref[:, lanes].astype(F32) + coefs[1][:, lanes] * slab4[grp]
             + coefs[2][:, lanes] * slab16[grp])
        o_scr[:, lanes] = o.astype(BF16)
    y = _mm(o_scr[...], w_ref[...])
    o_ref[...] = x_ref[...] + gate_ref[...] * (_rms(y) * gp_ref[...])


def _mix_outproj(accs, sts, expand, w, o, x, gp, mods, layer, tm):
    r = x.shape[0]
    nseq = accs[1].shape[0]
    tps = (r // nseq) // tm
    cls_spec = lambda dil, width: pl.BlockSpec((None, dil, tm // dil, width), lambda i: (i // tps, 0, i % tps, 0))
    return pl.pallas_call(
        _mix_body,
        grid=(r // tm,),
        in_specs=[_row_spec(tm, D), cls_spec(4, D), cls_spec(16, D), _row_spec(tm, LANES), cls_spec(4, LANES),
                  cls_spec(16, LANES), _const_spec(expand.shape), _layer_spec(w, o), _row_spec(tm, D),
                  _const_spec((1, D)), _mod_spec(mods, layer, 2, tps)],
        out_specs=_row_spec(tm, D),
        out_shape=jax.ShapeDtypeStruct((r, D), F32),
        scratch_shapes=[pltpu.VMEM((D // LANES, tm, LANES), F32), pltpu.VMEM((D // LANES, tm, LANES), F32),
                        pltpu.VMEM((tm, LANES), F32), pltpu.VMEM((tm, LANES), F32), pltpu.VMEM((tm, D), BF16)],
        compiler_params=_cparams("parallel"),
        name="attn_mix_outproj",
    )(*accs, *sts, expand, w, x, gp, mods)


def _attn_sample_body(q_ref, kn_ref, vn_ref, kc_ref, vc_ref, o_ref, *, t_len, window):
    tq = lax.broadcasted_iota(jnp.int32, (t_len, window), 0)
    back = window + tq - lax.broadcasted_iota(jnp.int32, (t_len, window), 1)
    cnt = (jnp.where(back <= BAND, 1.0, 0.0) + jnp.where((back <= 4 * BAND) & ((back & 3) == 0), 1.0, 0.0)
           + jnp.where((back & 15) == 0, 1.0, 0.0))
    tn = lax.broadcasted_iota(jnp.int32, (t_len, t_len), 0)
    dn = tn - lax.broadcasted_iota(jnp.int32, (t_len, t_len), 1)
    mult = jnp.where(dn == 0, 3.0, 0.0) + jnp.where(dn > 0, 1.0, 0.0) + jnp.where(dn == 4, 1.0, 0.0)
    lane = lax.broadcasted_iota(jnp.int32, (t_len, LANES), 1)
    for pair in range(q_ref.shape[1] // LANES):
        lanes = slice(pair * LANES, (pair + 1) * LANES)
        q2, kn, vn = q_ref[:, lanes], kn_ref[:, lanes].astype(BF16), vn_ref[:, lanes].astype(BF16)
        kc, vc = kc_ref[lanes, :].astype(BF16), vc_ref[lanes, :].astype(BF16)
        outs = []
        for half in range(2):
            qh = jnp.where((lane >= HEAD) if half else (lane < HEAD), q2, 0.0).astype(BF16)
            s_c = jnp.where(cnt > 0.0, _mm(qh, kc), NEG)
            s_n = jnp.where(mult > 0.0, _mm(qh, kn, NT), NEG)
            top = jnp.maximum(jnp.max(s_c, axis=-1, keepdims=True), jnp.max(s_n, axis=-1, keepdims=True))
            p_c = jnp.exp(s_c - top) * cnt
            p_n = jnp.exp(s_n - top) * mult
            den = jnp.sum(p_c, axis=-1, keepdims=True) + jnp.sum(p_n, axis=-1, keepdims=True)
            outs.append((_mm(p_c, vc, NT) + _mm(p_n, vn)) / den)
        o_ref[:, lanes] = jnp.where(lane < HEAD, outs[0], outs[1])


def _attn_sample(q, kn, vn, cache_k, cache_v, layer, nseq):
    window = cache_k.shape[-1]
    t_len = q.shape[0] // nseq
    assert window == 16 * BAND and t_len == 8
    width = D
    new = pl.BlockSpec((t_len, width), lambda b, g: (b, g))
    cached = pl.BlockSpec((None, None, width, window), lambda b, g: (layer, b, g, 0))
    return pl.pallas_call(
        functools.partial(_attn_sample_body, t_len=t_len, window=window),
        grid=(nseq, D // width),
        in_specs=[new, new, new, cached, cached],
        out_specs=new,
        out_shape=jax.ShapeDtypeStruct(q.shape, F32),
        compiler_params=_cparams("parallel", "parallel"),
        name="attn_sample",
    )(q, kn, vn, cache_k, cache_v)


def kernel(x_prompt, x_sample, state_wkv, state_shift, cache_k, cache_v, c_prompt, c_sample, ada_w, ada_b, norm_mix_pre, norm_mix_post, norm_ffn_pre, norm_ffn_post, ffn_w1, ffn_w2, ev_w_in, ev_w_out, gm_ln_g, gm_ln_b, gm_ws, gm_bs, rw_mu, rw_w0, rw_w2, rw_a0, rw_a2, rw_g2, rw_kk, rw_ka, rw_rk, rw_lnx_g, rw_lnx_b, od_w_qkv, od_w_out):
    n_p, s_len, _ = x_prompt.shape
    n_s, t_len, _ = x_sample.shape
    r_p, r_s = n_p * s_len, n_s * t_len
    row = lambda t: t.reshape(1, -1).astype(F32)

    mods = _ada(jnp.concatenate([c_prompt, c_sample], axis=0).astype(F32), ada_w, ada_b)
    mods_p = mods[:, :n_p].reshape(DEPTH, n_p, 1, 6 * D)
    mods_s = jnp.repeat(mods[:, n_p:], t_len, axis=1).reshape(DEPTH, 1, r_s, 6 * D)

    xp = x_prompt.astype(F32).reshape(r_p, D)
    xs = x_sample.astype(F32).reshape(r_s, D)

    hsum = jnp.kron(jnp.eye(LANES // HEAD, dtype=F32), jnp.ones((HEAD, HEAD), F32)).astype(BF16)
    tile_head = 2 * (jnp.arange(LANES) % (C_HEADS // 2)) + jnp.arange(LANES) // (C_HEADS // 2)
    expand = ((tile_head[:, None] == jnp.arange(D)[None, :] // HEAD) & (jnp.arange(LANES)[:, None] < C_HEADS)).astype(BF16)
    tabs_p = _rope_tables(jnp.arange(s_len))
    tabs_s = _rope_tables(jnp.tile(PAST_LEN + jnp.arange(t_len), n_s))
    zeros_prev = jnp.zeros((n_p, 1, B_COLS), F32)
    zeros_state = jnp.zeros((n_p, B_HEADS, HEAD, HEAD), F32)
    reps = CHUNK // t_len
    feat_major = lambda c: jnp.transpose(c.astype(F32), (0, 1, 3, 4, 2)).reshape(c.shape[0], c.shape[1], D, c.shape[2])
    cache_kt, cache_vt = feat_major(cache_k), feat_major(cache_v)
    tok_major = lambda t: jnp.transpose(t.reshape(n_p, C_HEADS, HEAD, s_len), (0, 3, 1, 2))

    w_in, w_out = ev_w_in.astype(BF16), ev_w_out.astype(BF16)
    w_qkv, w_att = od_w_qkv.astype(BF16), od_w_out.astype(BF16)
    w1, w2 = ffn_w1.astype(BF16), ffn_w2.astype(BF16)
    tm_p = 512
    outs = {k: [] for k in ("wkv_p", "shift_p", "k_p", "v_p", "wkv_s", "shift_s", "k_s", "v_s", "gv_s")}
    for l in range(DEPTH):
        g_pre, g_post = row(norm_mix_pre[l]), row(norm_mix_post[l])
        if l % 2 == 0:
            e = l // 2
            w_ab = [(w_out, e, 0), (w_out, e, 1)]
            ln_g, ln_b = row(gm_ln_g[e]), row(gm_ln_b[e])
            ws = gm_ws[e].astype(F32)
            bias_p = jnp.repeat(gm_bs[e].T, A_WIDTH // A_GROUPS, axis=1)
            ws_s = jax.vmap(lambda m: jnp.kron(jnp.eye(reps, dtype=F32), m))(ws[:, :t_len, :t_len])
            bias_s = jnp.tile(bias_p[:t_len], (reps, 1))
            mu = row(rw_mu[e])
            zpad = jnp.zeros((7 * B_WIDTH,), F32)
            prm = jnp.concatenate([rw_w0[e], rw_a0[e], rw_kk[e], rw_ka[e], rw_rk[e].reshape(-1), rw_lnx_g[e],
                                   rw_lnx_b[e], zpad[:B_WIDTH]]).reshape(8, B_WIDTH).astype(F32)
            zl = lambda a, b: jnp.zeros((a, b), F32)
            lora = jnp.stack([
                jnp.concatenate([rw_w2[e], zl(96, B_WIDTH)], axis=0),
                jnp.concatenate([zl(32, B_WIDTH), rw_a2[e], zl(64, B_WIDTH)], axis=0),
                jnp.concatenate([zl(64, B_WIDTH), rw_g2[e]], axis=0)]).astype(BF16)

            oa_p, pb_p = _inproj(xp, g_pre, mods_p, l, tm_p, w_in, e, ln_g, ln_b, ws, bias_p, False)
            pb_p3 = pb_p.reshape(n_p, s_len, B_COLS)
            ob_p, st_p = _rwkv(pb_p3, zeros_prev, zeros_state, mu, prm, lora, hsum, 2, 128, 64)
            xp = _outproj([oa_p, ob_p.reshape(r_p, B_WIDTH)], w_ab, xp, g_post, mods_p, l, tm_p)
            outs["wkv_p"].append(st_p)
            outs["shift_p"].append(pb_p3[:, -1])

            oa_s, pb_s, gv = _inproj(xs, g_pre, mods_s, l, r_s, w_in, e, ln_g, ln_b, ws_s, bias_s, True)
            pb_s3 = pb_s.reshape(n_s, t_len, B_COLS)
            ob_s, st_s = _rwkv(pb_s3, state_shift[e].astype(F32).reshape(n_s, 1, B_COLS), state_wkv[e].astype(F32),
                               mu, prm, lora, hsum, 8, t_len, t_len)
            xs = _outproj([oa_s, ob_s.reshape(r_s, B_WIDTH)], w_ab, xs, g_post, mods_s, l, r_s)
            outs["wkv_s"].append(st_s)
            outs["shift_s"].append(pb_s3[:, -1])
            outs["gv_s"].append(gv.reshape(n_s, t_len, A_WIDTH))
        else:
            o = l // 2
            q4, k4, v4, q16, k16, v16, kf, vf, acc1, st1 = _qkv_classes(xp, g_pre, mods_p, l, 256, w_qkv, o, tabs_p, n_p)
            res = [_band_attn(q4, k4, v4), _band_attn(q16, k16, v16)]
            accs = [acc1, res[0][0], res[1][0]]
            sts = [st1, res[0][1], res[1][1]]
            xp = _mix_outproj(accs, sts, expand, w_att, o, xp, g_post, mods_p, l, 512)
            outs["k_p"].append(tok_major(kf))
            outs["v_p"].append(tok_major(vf))

            qs, kfs, vfs = _qkv(xs, g_pre, mods_s, l, r_s, w_qkv, o, tabs_s)
            heads = lambda t: t.reshape(n_s, t_len, C_HEADS, HEAD)
            att = _attn_sample(qs, kfs, vfs, cache_kt, cache_vt, o, n_s)
            xs = _outproj([att], [(w_att, o, 0)], xs, g_post, mods_s, l, r_s)
            outs["k_s"].append(heads(kfs))
            outs["v_s"].append(heads(vfs))
        g_fpre, g_fpost = row(norm_ffn_pre[l]), row(norm_ffn_post[l])
        xp = _ffn(xp, g_fpre, g_fpost, mods_p, l, 512, 1024, w1, w2)
        xs = _ffn(xs, g_fpre, g_fpost, mods_s, l, r_s, 1024, w1, w2)

    st = lambda name: jnp.stack(outs[name])
    return (xp.reshape(n_p, s_len, D), xs.reshape(n_s, t_len, D), st("wkv_p"), st("shift_p"), st("k_p"),
            st("v_p"), st("wkv_s"), st("shift_s"), st("k_s"), st("v_s"), st("gv_s"))
```

```python
import functools
import math

import jax
import jax.numpy as jnp
from jax import lax
from jax.experimental import pallas as pl
from jax.experimental.pallas import tpu as pltpu

F32 = jnp.float32
BF16 = jnp.bfloat16

D = 1024
DEPTH = 4
D_FF = 4 * D
A_WIDTH = 512
A_GROUPS = 4
CHUNK = 128
B_WIDTH = 512
B_HEADS = 8
HEAD = 64
LORA = 128
B_COLS = 3 * B_WIDTH + LORA
C_HEADS = 16
ROT_DIM = 16
ROPE_THETA = 500000.0
DILATIONS = (1, 4, 16)
BAND = 128
PAST_LEN = 8192
NORM_EPS = 1e-6
LN_EPS = 1e-5
GN_EPS = 64e-5
NEG = -1e30

LANES = 128
DEN_EVEN, DEN_ODD = 64, 16
VMEM_LIMIT = 52 * 1024 * 1024


def _cparams(*sem):
    return pltpu.CompilerParams(dimension_semantics=sem, vmem_limit_bytes=VMEM_LIMIT)


def _split(x):
    hi = x.astype(BF16)
    return hi, (x - hi.astype(F32)).astype(BF16)


def _mm(a, b, dims=((1,), (0,)), passes=1):
    dn = (dims, ((), ()))
    dot = lambda p, q: lax.dot_general(p, q, dn, preferred_element_type=F32)
    if passes == 1:
        return dot(a.astype(BF16), b.astype(BF16))
    ah, al = _split(a)
    return dot(ah, b) + dot(al, b)


NT = ((1,), (1,))
TN = ((0,), (0,))


def _rms(x):
    return x * lax.rsqrt(jnp.mean(x * x, axis=-1, keepdims=True) + NORM_EPS)


def _mod_spec(mods, layer, chunk, tiles_per_group):
    return pl.BlockSpec((None, None, mods.shape[2], D), lambda i, *_: (layer, i // tiles_per_group, 0, chunk))


def _row_spec(tm, n):
    return pl.BlockSpec((tm, n), lambda i, *_: (i, 0))


def _layer_spec(w, layer, rows=None, row_block=0, **kw):
    rows = w.shape[1] if rows is None else rows
    return pl.BlockSpec((None, rows, w.shape[2]), lambda *_: (layer, row_block, 0), **kw)


def _const_spec(shape):
    nd = len(shape)
    return pl.BlockSpec(shape, lambda *_: (0,) * nd)


def _ada_body(c_ref, w_ref, b_ref, o_ref):
    c = c_ref[...]
    o_ref[...] = _mm(c * jax.nn.sigmoid(c), w_ref[...]) + b_ref[...]


def _ada(c_all, ada_w, ada_b):
    n = c_all.shape[0]
    tn = 1536
    return pl.pallas_call(
        _ada_body,
        grid=(DEPTH, 6 * D // tn),
        in_specs=[pl.BlockSpec((n, D), lambda l, j: (0, 0)),
                  pl.BlockSpec((None, D, tn), lambda l, j: (l, 0, j)),
                  pl.BlockSpec((None, 1, tn), lambda l, j: (l, 0, j))],
        out_specs=pl.BlockSpec((None, n, tn), lambda l, j: (l, 0, j)),
        out_shape=jax.ShapeDtypeStruct((DEPTH, n, 6 * D), F32),
        compiler_params=_cparams("parallel", "parallel"),
        name="ada",
    )(c_all, ada_w, ada_b.reshape(DEPTH, 1, 6 * D))


def _modnorm(x, g, sh, sc):
    return _rms(x) * g * (1.0 + sc) + sh


def _gelu(x):
    return 0.5 * x * (1.0 + lax.erf(x * (2.0 ** -0.5)))


def _inproj_body(x_ref, g_ref, sh_ref, sc_ref, w_ref, lng_ref, lnb_ref, ws_ref, bias_ref, oa_ref, pb_ref, *v_refs):
    h = _modnorm(x_ref[...], g_ref[...], sh_ref[...], sc_ref[...]).astype(BF16)
    pb_ref[...] = _mm(h, w_ref[:, 2 * A_WIDTH:])
    u = _gelu(_mm(h, w_ref[:, :A_WIDTH]))
    v = _gelu(_mm(h, w_ref[:, A_WIDTH:2 * A_WIDTH]))
    vc = v - jnp.mean(v, axis=-1, keepdims=True)
    var = jnp.mean(vc * vc, axis=-1, keepdims=True)
    vn = vc * lax.rsqrt(var + LN_EPS) * lng_ref[...] + lnb_ref[...]
    if v_refs:
        v_refs[0][...] = vn
    ii = lax.broadcasted_iota(jnp.int32, (CHUNK, CHUNK), 0)
    jj = lax.broadcasted_iota(jnp.int32, (CHUNK, CHUNK), 1)
    cg = A_WIDTH // A_GROUPS
    for grp in range(A_GROUPS):
        lanes = slice(grp * cg, (grp + 1) * cg)
        wc = jnp.where(jj <= ii, ws_ref[grp], 0.0).astype(BF16)
        for ci in range(x_ref.shape[0] // CHUNK):
            rows = slice(ci * CHUNK, (ci + 1) * CHUNK)
            z = _mm(wc, vn[rows, lanes]) + bias_ref[:, lanes]
            oa_ref[rows, lanes] = (u[rows, lanes] * z).astype(oa_ref.dtype)


def _inproj(x, g, mods, layer, tm, w, e, ln_g, ln_b, ws, bias, emit_v):
    r = x.shape[0]
    tpg = (r // mods.shape[1]) // tm
    half = lambda dtype: jax.ShapeDtypeStruct((r, A_WIDTH), dtype)
    return pl.pallas_call(
        _inproj_body,
        grid=(r // tm,),
        in_specs=[_row_spec(tm, D), _const_spec((1, D)), _mod_spec(mods, layer, 0, tpg),
                  _mod_spec(mods, layer, 1, tpg), _layer_spec(w, e), _const_spec((1, A_WIDTH)),
                  _const_spec((1, A_WIDTH)), _const_spec(ws.shape), _const_spec(bias.shape)],
        out_specs=[_row_spec(tm, A_WIDTH), _row_spec(tm, B_COLS)] + ([_row_spec(tm, A_WIDTH)] if emit_v else []),
        out_shape=[half(BF16), jax.ShapeDtypeStruct((r, B_COLS), F32)] + ([half(F32)] if emit_v else []),
        compiler_params=_cparams("parallel"),
        name="even_inproj_gmlp",
    )(x, g, mods, mods, w, ln_g, ln_b, ws, bias)


def _rope(seg, cos, s_lo, s_hi):
    return seg * cos + pltpu.roll(seg, LANES - ROT_DIM // 2, 1) * s_lo + pltpu.roll(seg, ROT_DIM // 2, 1) * s_hi


def _qkv_groups(x_ref, g_ref, sh_ref, sc_ref, w_ref, cos_ref, slo_ref, shi_ref):
    h = _modnorm(x_ref[...], g_ref[...], sh_ref[...], sc_ref[...]).astype(BF16)
    cos, s_lo, s_hi = cos_ref[...], slo_ref[...], shi_ref[...]
    wide = 2 * LANES
    for pair in range(D // wide):
        q2, k2, v2 = (_mm(h, w_ref[:, part * D + pair * wide:part * D + (pair + 1) * wide]) for part in range(3))
        for half in range(2):
            part = slice(half * LANES, (half + 1) * LANES)
            grp = 2 * pair + half
            yield (slice(grp * LANES, (grp + 1) * LANES), _rope(q2[:, part], cos, s_lo, s_hi) * HEAD ** -0.5,
                   _rope(k2[:, part], cos, s_lo, s_hi), v2[:, part])


def _qkv_body(*refs):
    q_ref, kf_ref, vf_ref = refs[8:]
    for lanes, q, k, v in _qkv_groups(*refs[:8]):
        q_ref[:, lanes] = q
        kf_ref[:, lanes] = k
        vf_ref[:, lanes] = v


def _qkv_classes_body(*refs, tps):
    q4, k4, v4, q16, k16, v16, kf_ref, vf_ref, acc_ref, st_ref, slab, slab4, q_scr, k_scr, v_scr = refs[8:]
    tm = q_scr.shape[0]
    first_tile = pl.program_id(0) % tps == 0

    @pl.when(first_tile)
    def _():
        k_scr[:BAND, :] = jnp.zeros((BAND, D), BF16)
        v_scr[:BAND, :] = jnp.zeros((BAND, D), BF16)

    @pl.when(jnp.logical_not(first_tile))
    def _():
        k_scr[:BAND, :] = k_scr[tm:, :]
        v_scr[:BAND, :] = v_scr[tm:, :]

    for lanes, q, k, v in _qkv_groups(*refs[:8]):
        kf_ref[lanes, :] = k.T
        vf_ref[lanes, :] = v.T
        q_scr[:, lanes] = q.astype(BF16)
        k_scr[BAND:, lanes] = k.astype(BF16)
        v_scr[BAND:, lanes] = v.astype(BF16)
        for i, (val, o4, o16) in enumerate(((q, q4, q16), (k, k4, k16), (v, v4, v16))):
            slab[i] = val
            for r4 in range(4):
                cls = slab[i, pl.ds(r4, tm // 4, stride=4), :]
                o4[r4, :, lanes] = cls.astype(BF16)
                slab4[i, r4 * (tm // 4):(r4 + 1) * (tm // 4), :] = cls
            for r16 in range(16):
                start = (r16 % 4) * (tm // 4) + r16 // 4
                o16[r16, :, lanes] = slab4[i, pl.ds(start, tm // 16, stride=4), :].astype(BF16)
    for j in range(tm // BAND):
        keys = lambda ref, lanes: ref[j * BAND:(j + 2) * BAND, lanes]
        mask = _band_mask(first_tile) if j == 0 else _band_mask(False)
        _band_block(q_scr, slice(j * BAND, (j + 1) * BAND), keys, k_scr, v_scr, mask, acc_ref, st_ref, 4)


def _rope_tables(pos):
    half = ROT_DIM // 2
    inv = ROPE_THETA ** (-jnp.arange(half, dtype=F32) * (2.0 / ROT_DIM))
    ang = pos.astype(F32)[:, None] * inv[None, :]
    cos, sin = jnp.cos(ang), jnp.sin(ang)
    t = pos.shape[0]
    one, zero, z8 = jnp.ones((t, HEAD - ROT_DIM), F32), jnp.zeros((t, HEAD - ROT_DIM), F32), jnp.zeros((t, half), F32)
    tab = lambda *parts: jnp.tile(jnp.concatenate(parts, axis=1), (1, LANES // HEAD))
    return tab(cos, cos, one), tab(-sin, z8, zero), tab(z8, sin, zero)


def _qkv_in_specs(mods, layer, tm, tpg, tables_per_seq, w, o):
    tab_spec = pl.BlockSpec((tm, LANES), lambda i: (i % tables_per_seq, 0))
    return [_row_spec(tm, D), _const_spec((1, D)), _mod_spec(mods, layer, 0, tpg), _mod_spec(mods, layer, 1, tpg),
            _layer_spec(w, o), tab_spec, tab_spec, tab_spec]


def _qkv(x, g, mods, layer, tm, w, o, tables):
    r = x.shape[0]
    tpg = (r // mods.shape[1]) // tm
    f32 = jax.ShapeDtypeStruct((r, D), F32)
    return pl.pallas_call(
        _qkv_body,
        grid=(r // tm,),
        in_specs=_qkv_in_specs(mods, layer, tm, tpg, 1, w, o),
        out_specs=[_row_spec(tm, D)] * 3,
        out_shape=[f32, f32, f32],
        compiler_params=_cparams("parallel"),
        name="odd_qkv_decode",
    )(x, g, mods, mods, w, *tables)


def _qkv_classes(x, g, mods, layer, tm, w, o, tables, nseq):
    r = x.shape[0]
    s_len = r // nseq
    tps = s_len // tm
    cls_spec = lambda dil: pl.BlockSpec((None, dil, tm // dil, D), lambda i: (i // tps, 0, i % tps, 0))
    cls_shape = lambda dil: jax.ShapeDtypeStruct((nseq, dil, s_len // dil, D), BF16)
    f32 = jax.ShapeDtypeStruct((nseq, D, s_len), F32)
    f32_spec = pl.BlockSpec((None, D, tm), lambda i: (i // tps, 0, i % tps))
    return pl.pallas_call(
        functools.partial(_qkv_classes_body, tps=tps),
        grid=(r // tm,),
        in_specs=_qkv_in_specs(mods, layer, tm, tps, tps, w, o),
        out_specs=[cls_spec(4)] * 3 + [cls_spec(16)] * 3 + [f32_spec] * 2 + [_row_spec(tm, D), _row_spec(tm, LANES)],
        out_shape=[cls_shape(4)] * 3 + [cls_shape(16)] * 3 + [f32, f32, jax.ShapeDtypeStruct((r, D), BF16),
                                                                jax.ShapeDtypeStruct((r, LANES), F32)],
        scratch_shapes=[pltpu.VMEM((3, tm, LANES), F32), pltpu.VMEM((3, tm, LANES), F32), pltpu.VMEM((tm, D), BF16),
                        pltpu.VMEM((BAND + tm, D), BF16), pltpu.VMEM((BAND + tm, D), BF16)],
        compiler_params=_cparams("arbitrary"),
        name="odd_qkv_attn1",
    )(x, g, mods, mods, w, *tables)


def _outproj_body(*refs, n_in):
    ins, ws = refs[:n_in], refs[n_in:2 * n_in]
    x_ref, gp_ref, gate_ref, o_ref = refs[2 * n_in:]
    y = _mm(ins[0][...], ws[0][...])
    for a, w in zip(ins[1:], ws[1:]):
        y = y + _mm(a[...], w[...])
    o_ref[...] = x_ref[...] + gate_ref[...] * (_rms(y) * gp_ref[...])


def _outproj(ins, ws, x, gp, mods, layer, tm):
    r = x.shape[0]
    tpg = (r // mods.shape[1]) // tm
    return pl.pallas_call(
        functools.partial(_outproj_body, n_in=len(ins)),
        grid=(r // tm,),
        in_specs=[_row_spec(tm, a.shape[1]) for a in ins]
        + [_layer_spec(w, l, a.shape[1], b) for a, (w, l, b) in zip(ins, ws)]
        + [_row_spec(tm, D), _const_spec((1, D)), _mod_spec(mods, layer, 2, tpg)],
        out_specs=_row_spec(tm, D),
        out_shape=jax.ShapeDtypeStruct((r, D), F32),
        compiler_params=_cparams("parallel"),
        name="outproj",
    )(*ins, *(w for w, _, _ in ws), x, gp, mods)


def _ffn_body(x_ref, g_ref, sh_ref, sc_ref, gate_ref, gp_ref, w1_ref, w2_ref, o_ref, t_scr, *, fc):
    x = x_ref[...]
    h = _modnorm(x, g_ref[...], sh_ref[...], sc_ref[...]).astype(BF16)
    for c in range(D_FF // fc):
        cols = slice(c * fc, (c + 1) * fc)
        t = jnp.maximum(_mm(h, w1_ref[:, cols]), 0.0)
        t_scr[:, cols] = (t * t).astype(BF16)
    o_ref[...] = x + gate_ref[...] * (_rms(_mm(t_scr[...], w2_ref[...])) * gp_ref[...])


def _ffn(x, g, gp, mods, layer, tm, fc, w1, w2):
    r = x.shape[0]
    tpg = (r // mods.shape[1]) // tm
    resident = lambda w: _layer_spec(w, layer, pipeline_mode=pl.Buffered(1))
    return pl.pallas_call(
        functools.partial(_ffn_body, fc=fc),
        grid=(r // tm,),
        in_specs=[_row_spec(tm, D), _const_spec((1, D)), _mod_spec(mods, layer, 3, tpg),
                  _mod_spec(mods, layer, 4, tpg), _mod_spec(mods, layer, 5, tpg), _const_spec((1, D)),
                  resident(w1), resident(w2)],
        out_specs=_row_spec(tm, D),
        out_shape=jax.ShapeDtypeStruct((r, D), F32),
        scratch_shapes=[pltpu.VMEM((tm, D_FF), BF16)],
        compiler_params=_cparams("parallel"),
        name="ffn",
    )(x, g, mods, mods, mods, gp, w1, w2)


def _seg_cumsum(x, seg):
    row = lax.broadcasted_iota(jnp.int32, (x.shape[0], 1), 0) % seg
    s = 1
    while s < seg:
        x = x + jnp.where(row >= s, pltpu.roll(x, s, 0), 0.0)
        s *= 2
    return x


def _rwkv_body(pb_ref, prev_ref, s0_ref, mu_ref, prm_ref, lora_ref, hsum_ref, out_ref, s_ref, last_scr,
               *, nb, tt, c):
    @pl.when(pl.program_id(1) == 0)
    def _():
        s_ref[...] = s0_ref[...]
        last_scr[...] = prev_ref[...]

    mu = mu_ref[...]
    w0, a0, k_k, k_a, r_k, ln_g, ln_b = (prm_ref[i:i + 1, :] for i in range(7))
    hsum = hsum_ref[...]

    def headsum(t):
        groups = [_mm(t[:, g * LANES:(g + 1) * LANES], hsum, passes=2) for g in range(B_WIDTH // LANES)]
        return jnp.concatenate(groups, axis=1)

    row = lax.broadcasted_iota(jnp.int32, (tt, 1), 0)
    ii = lax.broadcasted_iota(jnp.int32, (c, 2 * c), 0)
    jj = lax.broadcasted_iota(jnp.int32, (c, 2 * c), 1) % c
    strict, incl = jj < ii, jj <= ii
    eye = jnp.where(lax.broadcasted_iota(jnp.int32, (c, c), 0) == lax.broadcasted_iota(jnp.int32, (c, c), 1), 1.0, 0.0).astype(F32)
    zeros_c = jnp.zeros((c, HEAD), BF16)
    n_chunk = tt // c
    levels = int(math.log2(c)) - 1

    seqs = []
    for s in range(nb):
        x = pb_ref[s]
        shifted = jnp.where(row == 0, last_scr[s], pltpu.roll(x, 1, 0))
        last_scr[s] = x[tt - 1:tt, :]
        xm = x + mu * (shifted - x)
        r, k, v = xm[:, :B_WIDTH], xm[:, B_WIDTH:2 * B_WIDTH], xm[:, 2 * B_WIDTH:3 * B_WIDTH]
        lo = xm[:, 3 * B_WIDTH:]
        z = -(w0 + _mm(jnp.tanh(lo), lora_ref[0]))
        softplus = jnp.maximum(z, 0.0) + jnp.log1p(jnp.exp(-jnp.abs(z)))
        lw = -jnp.exp(-softplus - 0.5)
        a = jax.nn.sigmoid(a0 + _mm(lo, lora_ref[1]))
        g = _mm(jax.nn.sigmoid(lo), lora_ref[2])
        kk = k * k_k
        kk = kk * lax.rsqrt(headsum(kk * kk) + 1e-12)
        k2 = k * (1.0 + (a - 1.0) * k_a)
        lp = _seg_cumsum(lw, c)
        e_pos, e_neg = jnp.exp(lp), jnp.exp(-lp)
        seqs.append(dict(
            r=r, k2=k2, v=v, g=g, e_pos=e_pos,
            a_t=-kk * jnp.exp(lp - lw), b_t=kk * a * e_neg, k_t=k2 * e_neg, r_t=r * e_pos, v_b=v))

    def piece(s, name, ci, h):
        return seqs[s][name][ci * c:(ci + 1) * c, h * HEAD:(h + 1) * HEAD].astype(BF16)

    chains = [(s, h) for s in range(nb) for h in range(B_HEADS)]
    units = [(ci, s, h) for ci in range(n_chunk) for (s, h) in chains]

    a_u = {u: piece(u[1], "a_t", u[0], u[2]) for u in units}
    r_u = {u: piece(u[1], "r_t", u[0], u[2]) for u in units}
    v_u = {u: piece(u[1], "v_b", u[0], u[2]) for u in units}
    ar_u = {u: jnp.concatenate([a_u[u], r_u[u]], axis=0) for u in units}
    bk_u = {u: jnp.concatenate([piece(u[1], "b_t", u[0], u[2]), piece(u[1], "k_t", u[0], u[2])], axis=0) for u in units}
    zv_u = {u: jnp.concatenate([zeros_c, v_u[u]], axis=0) for u in units}
    g12 = {u: _mm(ar_u[u], bk_u[u], NT) for u in units}
    g1 = {u: jnp.where(strict, g12[u][:c], 0.0).astype(BF16) for u in units}
    g2 = {u: jnp.where(incl, g12[u][c:], 0.0).astype(BF16) for u in units}
    n_u = {u: g1[u][:, :c] for u in units}
    inv = {u: eye + n_u[u].astype(F32) for u in units}
    pw = {u: _mm(n_u[u], n_u[u]) for u in units}
    for level in range(levels):
        if level + 1 < levels:
            both = {u: _mm(jnp.concatenate([pw[u], inv[u]], axis=0), pw[u]) for u in units}
            pw = {u: both[u][:c] for u in units}
            inv = {u: inv[u] + both[u][c:] for u in units}
        else:
            inv = {u: inv[u] + _mm(inv[u], pw[u]) for u in units}
    xn = {u: _mm(jnp.concatenate(_split(inv[u]), axis=0), n_u[u]) for u in units}
    res = {u: eye - inv[u] + (xn[u][:c] + xn[u][c:]) for u in units}
    inv = {u: inv[u] + _mm(res[u], inv[u]) for u in units}

    state = {ch: s_ref[ch[0], ch[1]] for ch in chains}
    o_u = {}
    for ci in range(n_chunk):
        us = [(ci, s, h) for (s, h) in chains]
        ars = {u: _mm(ar_u[u], state[u[1:]], NT) for u in us}
        rhs = {u: ars[u][:c] + _mm(g1[u], zv_u[u]) for u in us}
        uu = {u: _mm(inv[u], rhs[u]) for u in us}
        uv = {u: jnp.concatenate([uu[u].astype(BF16), v_u[u]], axis=0) for u in us}
        for u in us:
            o_u[u] = ars[u][c:] + _mm(g2[u], uv[u])
        for u in us:
            s, h = u[1:]
            p_end = seqs[s]["e_pos"][(ci + 1) * c - 1:(ci + 1) * c, h * HEAD:(h + 1) * HEAD]
            state[(s, h)] = (state[(s, h)] + _mm(uv[u], bk_u[u], TN)) * p_end
    for (s, h) in chains:
        s_ref[s, h] = state[(s, h)]

    for s in range(nb):
        q = seqs[s]
        o_rows = [jnp.concatenate([o_u[(ci, s, h)] for h in range(B_HEADS)], axis=1) for ci in range(n_chunk)]
        o = o_rows[0] if n_chunk == 1 else jnp.concatenate(o_rows, axis=0)
        inv_n = 1.0 / HEAD
        oc = o - headsum(o) * inv_n
        on = oc * lax.rsqrt(headsum(oc * oc) * inv_n + GN_EPS) * ln_g + ln_b
        bonus = headsum(q["r"] * q["k2"] * r_k) * q["v"]
        out_ref[s] = ((on + bonus) * q["g"]).astype(out_ref.dtype)


def _rwkv(pb, prev, s0, mu, prm, lora, hsum, nb, tt, c):
    n, t, _ = pb.shape
    return pl.pallas_call(
        functools.partial(_rwkv_body, nb=nb, tt=tt, c=c),
        grid=(n // nb, t // tt),
        in_specs=[pl.BlockSpec((nb, tt, B_COLS), lambda i, j: (i, j, 0)),
                  pl.BlockSpec((nb, 1, B_COLS), lambda i, j: (i, 0, 0)),
                  pl.BlockSpec((nb, B_HEADS, HEAD, HEAD), lambda i, j: (i, 0, 0, 0)),
                  _const_spec(mu.shape), _const_spec(prm.shape), _const_spec(lora.shape), _const_spec(hsum.shape)],
        out_specs=[pl.BlockSpec((nb, tt, B_WIDTH), lambda i, j: (i, j, 0)),
                   pl.BlockSpec((nb, B_HEADS, HEAD, HEAD), lambda i, j: (i, 0, 0, 0))],
        out_shape=[jax.ShapeDtypeStruct((n, t, B_WIDTH), BF16), jax.ShapeDtypeStruct(s0.shape, F32)],
        scratch_shapes=[pltpu.VMEM((nb, 1, B_COLS), F32)],
        compiler_params=_cparams("parallel", "arbitrary"),
        name="rwkv7",
    )(pb, prev, s0, mu, prm, lora, hsum)


def _band_block(q_at, rows, keys, k_ref, v_ref, mask, acc_at, st_at, group):
    nk = mask.shape[1]
    lane = lax.broadcasted_iota(jnp.int32, (BAND, LANES), 1)
    lane_k = lax.broadcasted_iota(jnp.int32, (nk, LANES), 1)
    stats = jnp.zeros((BAND, LANES), F32)
    for g0 in range(0, C_HEADS // 2, group):
        pairs = range(g0, g0 + group)
        scores = []
        for pair in pairs:
            lanes = slice(pair * LANES, (pair + 1) * LANES)
            q2, k2 = q_at[rows, lanes], keys(k_ref, lanes)
            for half in range(2):
                in_head = (lane >= HEAD) if half else (lane < HEAD)
                scores.append(_mm(jnp.where(in_head, q2, jnp.zeros_like(q2)), k2, NT))
        probs = []
        for i, s in enumerate(scores):
            s = jnp.where(mask, s, NEG)
            m = jnp.max(s, axis=-1, keepdims=True)
            probs.append(jnp.exp(s - m).astype(BF16))
            stats = jnp.where(lane == (i % 2) * (C_HEADS // 2) + g0 + i // 2, m, stats)
        for i, pair in enumerate(pairs):
            lanes = slice(pair * LANES, (pair + 1) * LANES)
            v2 = keys(v_ref, lanes)
            lo = _mm(probs[2 * i], jnp.where(lane_k < HEAD, v2, jnp.ones_like(v2)))
            hi = _mm(probs[2 * i + 1], jnp.where(lane_k >= HEAD, v2, jnp.ones_like(v2)))
            acc_at[rows, lanes] = jnp.where(lane < HEAD, lo, hi).astype(acc_at.dtype)
            stats = jnp.where(lane == DEN_EVEN + pair, lo, stats)
            stats = jnp.where(lane == DEN_ODD + pair, hi, stats)
    st_at[rows, :] = stats


def _band_mask(first_block):
    qi = lax.broadcasted_iota(jnp.int32, (BAND, 2 * BAND), 0)
    kc = lax.broadcasted_iota(jnp.int32, (BAND, 2 * BAND), 1)
    return (kc >= qi) & (kc <= qi + BAND) & (kc >= jnp.where(first_block, BAND, 0))


def _band_attn_body(q_ref, k_ref, v_ref, acc_ref, st_ref, *, nblk, units):
    for j in range(units):
        if nblk == 1:
            qi = lax.broadcasted_iota(jnp.int32, (BAND, BAND), 0)
            kc = lax.broadcasted_iota(jnp.int32, (BAND, BAND), 1)
            _band_block(q_ref.at[j], slice(None), lambda ref, lanes: ref[j, :, lanes], k_ref, v_ref, kc <= qi,
                        acc_ref.at[j], st_ref.at[j], 8)
        else:
            blk = pl.program_id(2) * units + j
            cur = pl.multiple_of(blk * BAND, BAND)
            prev = pl.multiple_of(jnp.maximum(blk - 1, 0) * BAND, BAND)
            keys = lambda ref, lanes: jnp.concatenate([ref[pl.ds(prev, BAND), lanes], ref[pl.ds(cur, BAND), lanes]], axis=0)
            _band_block(q_ref, slice(j * BAND, (j + 1) * BAND), keys, k_ref, v_ref, _band_mask(blk == 0),
                        acc_ref, st_ref, 4)


def _band_attn(q, k, v, units=4):
    nseq, dil, sub, _ = q.shape
    nblk = sub // BAND
    if nblk == 1:
        grid = (nseq, dil // units, 1)
        blk = lambda width: pl.BlockSpec((None, units, sub, width), lambda b, r, n: (b, r, 0, 0))
        cls = blk(D)
    else:
        grid = (nseq, dil, nblk // units)
        blk = lambda width: pl.BlockSpec((None, None, units * BAND, width), lambda b, r, n: (b, r, n, 0))
        cls = pl.BlockSpec((None, None, sub, D), lambda b, r, n: (b, r, 0, 0))
    return pl.pallas_call(
        functools.partial(_band_attn_body, nblk=nblk, units=units),
        grid=grid,
        in_specs=[blk(D), cls, cls],
        out_specs=[blk(D), blk(LANES)],
        out_shape=[jax.ShapeDtypeStruct(q.shape, q.dtype), jax.ShapeDtypeStruct((nseq, dil, sub, LANES), F32)],
        compiler_params=_cparams("parallel", "parallel", "arbitrary"),
        name=f"band_attn_d{dil}",
    )(q, k, v)


def _mix_body(a1_ref, a4_ref, a16_ref, s1_ref, s4_ref, s16_ref, e_ref, w_ref, x_ref, gp_ref, gate_ref, o_ref,
              slab4, slab16, st4, st16, o_scr):
    tm = x_ref.shape[0]
    ngrp = D // LANES
    for dil, a_ref, s_ref, slab, st in ((4, a4_ref, s4_ref, slab4, st4), (16, a16_ref, s16_ref, slab16, st16)):
        for r in range(dil):
            rows = pl.ds(r, tm // dil, stride=dil)
            st[rows, :] = s_ref[r]
            for grp in range(ngrp):
                slab[grp, rows, :] = a_ref[r, :, grp * LANES:(grp + 1) * LANES].astype(F32)
    sts = [s1_ref[...], st4[...], st16[...]]
    lane = lax.broadcasted_iota(jnp.int32, sts[0].shape, 1)
    top = jnp.maximum(jnp.maximum(sts[0], sts[1]), sts[2])
    es = [jnp.exp(st - top) for st in sts]
    dens = [jnp.where(lane < C_HEADS // 2, pltpu.roll(st, LANES - DEN_EVEN, 1),
                      pltpu.roll(st, LANES - DEN_ODD + C_HEADS // 2, 1)) for st in sts]
    total = es[0] * dens[0] + es[1] * dens[1] + es[2] * dens[2]
    coefs = [_mm(jnp.where(lane < C_HEADS, e / total, 0.0), e_ref[...], passes=2) for e in es]
    for grp in range(ngrp):
        lanes = slice(grp * LANES, (grp + 1) * LANES)
        o = (coefs[0][:, lanes] * a1_ref[:, lanes].astype(F32) + coefs[1][:, lanes] * slab4[grp]
             + coefs[2][:, lanes] * slab16[grp])
        o_scr[:, lanes] = o.astype(BF16)
    y = _mm(o_scr[...], w_ref[...])
    o_ref[...] = x_ref[...] + gate_ref[...] * (_rms(y) * gp_ref[...])


def _mix_outproj(accs, sts, expand, w, o, x, gp, mods, layer, tm):
    r = x.shape[0]
    nseq = accs[1].shape[0]
    tps = (r // nseq) // tm
    cls_spec = lambda dil, width: pl.BlockSpec((None, dil, tm // dil, width), lambda i: (i // tps, 0, i % tps, 0))
    return pl.pallas_call(
        _mix_body,
        grid=(r // tm,),
        in_specs=[_row_spec(tm, D), cls_spec(4, D), cls_spec(16, D), _row_spec(tm, LANES), cls_spec(4, LANES),
                  cls_spec(16, LANES), _const_spec(expand.shape), _layer_spec(w, o), _row_spec(tm, D),
                  _const_spec((1, D)), _mod_spec(mods, layer, 2, tps)],
        out_specs=_row_spec(tm, D),
        out_shape=jax.ShapeDtypeStruct((r, D), F32),
        scratch_shapes=[pltpu.VMEM((D // LANES, tm, LANES), F32), pltpu.VMEM((D // LANES, tm, LANES), F32),
                        pltpu.VMEM((tm, LANES), F32), pltpu.VMEM((tm, LANES), F32), pltpu.VMEM((tm, D), BF16)],
        compiler_params=_cparams("parallel"),
        name="attn_mix_outproj",
    )(*accs, *sts, expand, w, x, gp, mods)


def _attn_sample_body(q_ref, kn_ref, vn_ref, kc_ref, vc_ref, o_ref, *, t_len, window):
    tq = lax.broadcasted_iota(jnp.int32, (t_len, window), 0)
    back = window + tq - lax.broadcasted_iota(jnp.int32, (t_len, window), 1)
    cnt = (jnp.where(back <= BAND, 1.0, 0.0) + jnp.where((back <= 4 * BAND) & ((back & 3) == 0), 1.0, 0.0)
           + jnp.where((back & 15) == 0, 1.0, 0.0))
    tn = lax.broadcasted_iota(jnp.int32, (t_len, t_len), 0)
    dn = tn - lax.broadcasted_iota(jnp.int32, (t_len, t_len), 1)
    mult = jnp.where(dn == 0, 3.0, 0.0) + jnp.where(dn > 0, 1.0, 0.0) + jnp.where(dn == 4, 1.0, 0.0)
    lane = lax.broadcasted_iota(jnp.int32, (t_len, LANES), 1)
    for pair in range(q_ref.shape[1] // LANES):
        lanes = slice(pair * LANES, (pair + 1) * LANES)
        q2, kn, vn = q_ref[:, lanes], kn_ref[:, lanes].astype(BF16), vn_ref[:, lanes].astype(BF16)
        kc, vc = kc_ref[lanes, :].astype(BF16), vc_ref[lanes, :].astype(BF16)
        outs = []
        for half in range(2):
            qh = jnp.where((lane >= HEAD) if half else (lane < HEAD), q2, 0.0).astype(BF16)
            s_c = jnp.where(cnt > 0.0, _mm(qh, kc), NEG)
            s_n = jnp.where(mult > 0.0, _mm(qh, kn, NT), NEG)
            top = jnp.maximum(jnp.max(s_c, axis=-1, keepdims=True), jnp.max(s_n, axis=-1, keepdims=True))
            p_c = jnp.exp(s_c - top) * cnt
            p_n = jnp.exp(s_n - top) * mult
            den = jnp.sum(p_c, axis=-1, keepdims=True) + jnp.sum(p_n, axis=-1, keepdims=True)
            outs.append((_mm(p_c, vc, NT) + _mm(p_n, vn)) / den)
        o_ref[:, lanes] = jnp.where(lane < HEAD, outs[0], outs[1])


def _attn_sample(q, kn, vn, cache_k, cache_v, layer, nseq):
    window = cache_k.shape[-1]
    t_len = q.shape[0] // nseq
    assert window == 16 * BAND and t_len == 8
    width = D
    new = pl.BlockSpec((t_len, width), lambda b, g: (b, g))
    cached = pl.BlockSpec((None, None, width, window), lambda b, g: (layer, b, g, 0))
    return pl.pallas_call(
        functools.partial(_attn_sample_body, t_len=t_len, window=window),
        grid=(nseq, D // width),
        in_specs=[new, new, new, cached, cached],
        out_specs=new,
        out_shape=jax.ShapeDtypeStruct(q.shape, F32),
        compiler_params=_cparams("parallel", "parallel"),
        name="attn_sample",
    )(q, kn, vn, cache_k, cache_v)


def kernel(x_prompt, x_sample, state_wkv, state_shift, cache_k, cache_v, c_prompt, c_sample, ada_w, ada_b, norm_mix_pre, norm_mix_post, norm_ffn_pre, norm_ffn_post, ffn_w1, ffn_w2, ev_w_in, ev_w_out, gm_ln_g, gm_ln_b, gm_ws, gm_bs, rw_mu, rw_w0, rw_w2, rw_a0, rw_a2, rw_g2, rw_kk, rw_ka, rw_rk, rw_lnx_g, rw_lnx_b, od_w_qkv, od_w_out):
    n_p, s_len, _ = x_prompt.shape
    n_s, t_len, _ = x_sample.shape
    r_p, r_s = n_p * s_len, n_s * t_len
    row = lambda t: t.reshape(1, -1).astype(F32)

    mods = _ada(jnp.concatenate([c_prompt, c_sample], axis=0).astype(F32), ada_w, ada_b)
    mods_p = mods[:, :n_p].reshape(DEPTH, n_p, 1, 6 * D)
    mods_s = jnp.repeat(mods[:, n_p:], t_len, axis=1).reshape(DEPTH, 1, r_s, 6 * D)

    xp = x_prompt.astype(F32).reshape(r_p, D)
    xs = x_sample.astype(F32).reshape(r_s, D)

    hsum = jnp.kron(jnp.eye(LANES // HEAD, dtype=F32), jnp.ones((HEAD, HEAD), F32)).astype(BF16)
    tile_head = 2 * (jnp.arange(LANES) % (C_HEADS // 2)) + jnp.arange(LANES) // (C_HEADS // 2)
    expand = ((tile_head[:, None] == jnp.arange(D)[None, :] // HEAD) & (jnp.arange(LANES)[:, None] < C_HEADS)).astype(BF16)
    tabs_p = _rope_tables(jnp.arange(s_len))
    tabs_s = _rope_tables(jnp.tile(PAST_LEN + jnp.arange(t_len), n_s))
    zeros_prev = jnp.zeros((n_p, 1, B_COLS), F32)
    zeros_state = jnp.zeros((n_p, B_HEADS, HEAD, HEAD), F32)
    reps = CHUNK // t_len
    feat_major = lambda c: jnp.transpose(c.astype(F32), (0, 1, 3, 4, 2)).reshape(c.shape[0], c.shape[1], D, c.shape[2])
    cache_kt, cache_vt = feat_major(cache_k), feat_major(cache_v)
    tok_major = lambda t: jnp.transpose(t.reshape(n_p, C_HEADS, HEAD, s_len), (0, 3, 1, 2))

    w_in, w_out = ev_w_in.astype(BF16), ev_w_out.astype(BF16)
    w_qkv, w_att = od_w_qkv.astype(BF16), od_w_out.astype(BF16)
    w1, w2 = ffn_w1.astype(BF16), ffn_w2.astype(BF16)
    tm_p = 1024
    outs = {k: [] for k in ("wkv_p", "shift_p", "k_p", "v_p", "wkv_s", "shift_s", "k_s", "v_s", "gv_s")}
    for l in range(DEPTH):
        g_pre, g_post = row(norm_mix_pre[l]), row(norm_mix_post[l])
        if l % 2 == 0:
            e = l // 2
            w_ab = [(w_out, e, 0), (w_out, e, 1)]
            ln_g, ln_b = row(gm_ln_g[e]), row(gm_ln_b[e])
            ws = gm_ws[e].astype(F32)
            bias_p = jnp.repeat(gm_bs[e].T, A_WIDTH // A_GROUPS, axis=1)
            ws_s = jax.vmap(lambda m: jnp.kron(jnp.eye(reps, dtype=F32), m))(ws[:, :t_len, :t_len])
            bias_s = jnp.tile(bias_p[:t_len], (reps, 1))
            mu = row(rw_mu[e])
            zpad = jnp.zeros((7 * B_WIDTH,), F32)
            prm = jnp.concatenate([rw_w0[e], rw_a0[e], rw_kk[e], rw_ka[e], rw_rk[e].reshape(-1), rw_lnx_g[e],
                                   rw_lnx_b[e], zpad[:B_WIDTH]]).reshape(8, B_WIDTH).astype(F32)
            zl = lambda a, b: jnp.zeros((a, b), F32)
            lora = jnp.stack([
                jnp.concatenate([rw_w2[e], zl(96, B_WIDTH)], axis=0),
                jnp.concatenate([zl(32, B_WIDTH), rw_a2[e], zl(64, B_WIDTH)], axis=0),
                jnp.concatenate([zl(64, B_WIDTH), rw_g2[e]], axis=0)]).astype(BF16)

            oa_p, pb_p = _inproj(xp, g_pre, mods_p, l, tm_p, w_in, e, ln_g, ln_b, ws, bias_p, False)
            pb_p3 = pb_p.reshape(n_p, s_len, B_COLS)
            ob_p, st_p = _rwkv(pb_p3, zeros_prev, zeros_state, mu, prm, lora, hsum, 2, 128, 64)
            xp = _outproj([oa_p, ob_p.reshape(r_p, B_WIDTH)], w_ab, xp, g_post, mods_p, l, tm_p)
            outs["wkv_p"].append(st_p)
            outs["shift_p"].append(pb_p3[:, -1])

            oa_s, pb_s, gv = _inproj(xs, g_pre, mods_s, l, r_s, w_in, e, ln_g, ln_b, ws_s, bias_s, True)
            pb_s3 = pb_s.reshape(n_s, t_len, B_COLS)
            ob_s, st_s = _rwkv(pb_s3, state_shift[e].astype(F32).reshape(n_s, 1, B_COLS), state_wkv[e].astype(F32),
                               mu, prm, lora, hsum, 8, t_len, t_len)
            xs = _outproj([oa_s, ob_s.reshape(r_s, B_WIDTH)], w_ab, xs, g_post, mods_s, l, r_s)
            outs["wkv_s"].append(st_s)
            outs["shift_s"].append(pb_s3[:, -1])
            outs["gv_s"].append(gv.reshape(n_s, t_len, A_WIDTH))
        else:
            o = l // 2
            q4, k4, v4, q16, k16, v16, kf, vf, acc1, st1 = _qkv_classes(xp, g_pre, mods_p, l, 512, w_qkv, o, tabs_p, n_p)
            res = [_band_attn(q4, k4, v4), _band_attn(q16, k16, v16)]
            accs = [acc1, res[0][0], res[1][0]]
            sts = [st1, res[0][1], res[1][1]]
            xp = _mix_outproj(accs, sts, expand, w_att, o, xp, g_post, mods_p, l, 512)
            outs["k_p"].append(tok_major(kf))
            outs["v_p"].append(tok_major(vf))

            qs, kfs, vfs = _qkv(xs, g_pre, mods_s, l, r_s, w_qkv, o, tabs_s)
            heads = lambda t: t.reshape(n_s, t_len, C_HEADS, HEAD)
            att = _attn_sample(qs, kfs, vfs, cache_kt, cache_vt, o, n_s)
            xs = _outproj([att], [(w_att, o, 0)], xs, g_post, mods_s, l, r_s)
            outs["k_s"].append(heads(kfs))
            outs["v_s"].append(heads(vfs))
        g_fpre, g_fpost = row(norm_ffn_pre[l]), row(norm_ffn_post[l])
        xp = _ffn(xp, g_fpre, g_fpost, mods_p, l, 512, 1024, w1, w2)
        xs = _ffn(xs, g_fpre, g_fpost, mods_s, l, r_s, 1024, w1, w2)

    st = lambda name: jnp.stack(outs[name])
    return (xp.reshape(n_p, s_len, D), xs.reshape(n_s, t_len, D), st("wkv_p"), st("shift_p"), st("k_p"),
            st("v_p"), st("wkv_s"), st("shift_s"), st("k_s"), st("v_s"), st("gv_s"))
```

```python
import functools
import math

import jax
import jax.numpy as jnp
from jax import lax
from jax.experimental import pallas as pl
from jax.experimental.pallas import tpu as pltpu

F32 = jnp.float32
BF16 = jnp.bfloat16

D = 1024
DEPTH = 4
D_FF = 4 * D
A_WIDTH = 512
A_GROUPS = 4
CHUNK = 128
B_WIDTH = 512
B_HEADS = 8
HEAD = 64
LORA = 128
B_COLS = 3 * B_WIDTH + LORA
C_HEADS = 16
ROT_DIM = 16
ROPE_THETA = 500000.0
DILATIONS = (1, 4, 16)
BAND = 128
PAST_LEN = 8192
NORM_EPS = 1e-6
LN_EPS = 1e-5
GN_EPS = 64e-5
NEG = -1e30

LANES = 128
DEN_EVEN, DEN_ODD = 64, 16
VMEM_LIMIT = 52 * 1024 * 1024


def _cparams(*sem):
    return pltpu.CompilerParams(dimension_semantics=sem, vmem_limit_bytes=VMEM_LIMIT)


def _split(x):
    hi = x.astype(BF16)
    return hi, (x - hi.astype(F32)).astype(BF16)


def _mm(a, b, dims=((1,), (0,)), passes=1):
    dn = (dims, ((), ()))
    dot = lambda p, q: lax.dot_general(p, q, dn, preferred_element_type=F32)
    if passes == 1:
        return dot(a.astype(BF16), b.astype(BF16))
    ah, al = _split(a)
    return dot(ah, b) + dot(al, b)


NT = ((1,), (1,))
TN = ((0,), (0,))


def _rms(x):
    return x * lax.rsqrt(jnp.mean(x * x, axis=-1, keepdims=True) + NORM_EPS)


def _mod_spec(mods, layer, chunk, tiles_per_group):
    return pl.BlockSpec((None, None, mods.shape[2], D), lambda i, *_: (layer, i // tiles_per_group, 0, chunk))


def _row_spec(tm, n):
    return pl.BlockSpec((tm, n), lambda i, *_: (i, 0))


def _layer_spec(w, layer, rows=None, row_block=0, **kw):
    rows = w.shape[1] if rows is None else rows
    return pl.BlockSpec((None, rows, w.shape[2]), lambda *_: (layer, row_block, 0), **kw)


def _const_spec(shape):
    nd = len(shape)
    return pl.BlockSpec(shape, lambda *_: (0,) * nd)


def _ada_body(c_ref, w_ref, b_ref, o_ref):
    c = c_ref[...]
    o_ref[...] = _mm(c * jax.nn.sigmoid(c), w_ref[...]) + b_ref[...]


def _ada(c_all, ada_w, ada_b):
    n = c_all.shape[0]
    tn = 1536
    return pl.pallas_call(
        _ada_body,
        grid=(DEPTH, 6 * D // tn),
        in_specs=[pl.BlockSpec((n, D), lambda l, j: (0, 0)),
                  pl.BlockSpec((None, D, tn), lambda l, j: (l, 0, j)),
                  pl.BlockSpec((None, 1, tn), lambda l, j: (l, 0, j))],
        out_specs=pl.BlockSpec((None, n, tn), lambda l, j: (l, 0, j)),
        out_shape=jax.ShapeDtypeStruct((DEPTH, n, 6 * D), F32),
        compiler_params=_cparams("parallel", "parallel"),
        name="ada",
    )(c_all, ada_w, ada_b.reshape(DEPTH, 1, 6 * D))


def _modnorm(x, g, sh, sc):
    return _rms(x) * g * (1.0 + sc) + sh


def _gelu(x):
    return 0.5 * x * (1.0 + lax.erf(x * (2.0 ** -0.5)))


def _inproj_body(x_ref, g_ref, sh_ref, sc_ref, w_ref, lng_ref, lnb_ref, ws_ref, bias_ref, oa_ref, pb_ref, *v_refs):
    h = _modnorm(x_ref[...], g_ref[...], sh_ref[...], sc_ref[...]).astype(BF16)
    pb_ref[...] = _mm(h, w_ref[:, 2 * A_WIDTH:])
    u = _gelu(_mm(h, w_ref[:, :A_WIDTH]))
    v = _gelu(_mm(h, w_ref[:, A_WIDTH:2 * A_WIDTH]))
    vc = v - jnp.mean(v, axis=-1, keepdims=True)
    var = jnp.mean(vc * vc, axis=-1, keepdims=True)
    vn = vc * lax.rsqrt(var + LN_EPS) * lng_ref[...] + lnb_ref[...]
    if v_refs:
        v_refs[0][...] = vn
    ii = lax.broadcasted_iota(jnp.int32, (CHUNK, CHUNK), 0)
    jj = lax.broadcasted_iota(jnp.int32, (CHUNK, CHUNK), 1)
    cg = A_WIDTH // A_GROUPS
    for grp in range(A_GROUPS):
        lanes = slice(grp * cg, (grp + 1) * cg)
        wc = jnp.where(jj <= ii, ws_ref[grp], 0.0).astype(BF16)
        for ci in range(x_ref.shape[0] // CHUNK):
            rows = slice(ci * CHUNK, (ci + 1) * CHUNK)
            z = _mm(wc, vn[rows, lanes]) + bias_ref[:, lanes]
            oa_ref[rows, lanes] = (u[rows, lanes] * z).astype(oa_ref.dtype)


def _inproj(x, g, mods, layer, tm, w, e, ln_g, ln_b, ws, bias, emit_v):
    r = x.shape[0]
    tpg = (r // mods.shape[1]) // tm
    half = lambda dtype: jax.ShapeDtypeStruct((r, A_WIDTH), dtype)
    return pl.pallas_call(
        _inproj_body,
        grid=(r // tm,),
        in_specs=[_row_spec(tm, D), _const_spec((1, D)), _mod_spec(mods, layer, 0, tpg),
                  _mod_spec(mods, layer, 1, tpg), _layer_spec(w, e), _const_spec((1, A_WIDTH)),
                  _const_spec((1, A_WIDTH)), _const_spec(ws.shape), _const_spec(bias.shape)],
        out_specs=[_row_spec(tm, A_WIDTH), _row_spec(tm, B_COLS)] + ([_row_spec(tm, A_WIDTH)] if emit_v else []),
        out_shape=[half(BF16), jax.ShapeDtypeStruct((r, B_COLS), F32)] + ([half(F32)] if emit_v else []),
        compiler_params=_cparams("parallel"),
        name="even_inproj_gmlp",
    )(x, g, mods, mods, w, ln_g, ln_b, ws, bias)


def _rope(seg, cos, s_lo, s_hi):
    return seg * cos + pltpu.roll(seg, LANES - ROT_DIM // 2, 1) * s_lo + pltpu.roll(seg, ROT_DIM // 2, 1) * s_hi


def _qkv_groups(x_ref, g_ref, sh_ref, sc_ref, w_ref, cos_ref, slo_ref, shi_ref):
    h = _modnorm(x_ref[...], g_ref[...], sh_ref[...], sc_ref[...]).astype(BF16)
    cos, s_lo, s_hi = cos_ref[...], slo_ref[...], shi_ref[...]
    wide = 2 * LANES
    for pair in range(D // wide):
        q2, k2, v2 = (_mm(h, w_ref[:, part * D + pair * wide:part * D + (pair + 1) * wide]) for part in range(3))
        for half in range(2):
            part = slice(half * LANES, (half + 1) * LANES)
            grp = 2 * pair + half
            yield (slice(grp * LANES, (grp + 1) * LANES), _rope(q2[:, part], cos, s_lo, s_hi) * HEAD ** -0.5,
                   _rope(k2[:, part], cos, s_lo, s_hi), v2[:, part])


def _qkv_body(*refs):
    q_ref, kf_ref, vf_ref = refs[8:]
    for lanes, q, k, v in _qkv_groups(*refs[:8]):
        q_ref[:, lanes] = q
        kf_ref[:, lanes] = k
        vf_ref[:, lanes] = v


def _qkv_classes_body(*refs, tps):
    q4, k4, v4, q16, k16, v16, kf_ref, vf_ref, acc_ref, st_ref, slab, slab4, q_scr, k_scr, v_scr = refs[8:]
    tm = q_scr.shape[0]
    first_tile = pl.program_id(0) % tps == 0

    @pl.when(first_tile)
    def _():
        k_scr[:BAND, :] = jnp.zeros((BAND, D), BF16)
        v_scr[:BAND, :] = jnp.zeros((BAND, D), BF16)

    @pl.when(jnp.logical_not(first_tile))
    def _():
        k_scr[:BAND, :] = k_scr[tm:, :]
        v_scr[:BAND, :] = v_scr[tm:, :]

    for lanes, q, k, v in _qkv_groups(*refs[:8]):
        kf_ref[lanes, :] = k.T
        vf_ref[lanes, :] = v.T
        q_scr[:, lanes] = q.astype(BF16)
        k_scr[BAND:, lanes] = k.astype(BF16)
        v_scr[BAND:, lanes] = v.astype(BF16)
        for i, (val, o4, o16) in enumerate(((q, q4, q16), (k, k4, k16), (v, v4, v16))):
            slab[i] = val
            for r4 in range(4):
                cls = slab[i, pl.ds(r4, tm // 4, stride=4), :]
                o4[r4, :, lanes] = cls.astype(BF16)
                slab4[i, r4 * (tm // 4):(r4 + 1) * (tm // 4), :] = cls
            for r16 in range(16):
                start = (r16 % 4) * (tm // 4) + r16 // 4
                o16[r16, :, lanes] = slab4[i, pl.ds(start, tm // 16, stride=4), :].astype(BF16)
    for j in range(tm // BAND):
        keys = lambda ref, lanes: ref[j * BAND:(j + 2) * BAND, lanes]
        mask = _band_mask(first_tile) if j == 0 else _band_mask(False)
        _band_block(q_scr, slice(j * BAND, (j + 1) * BAND), keys, k_scr, v_scr, mask, acc_ref, st_ref, 4)


def _rope_tables(pos):
    half = ROT_DIM // 2
    inv = ROPE_THETA ** (-jnp.arange(half, dtype=F32) * (2.0 / ROT_DIM))
    ang = pos.astype(F32)[:, None] * inv[None, :]
    cos, sin = jnp.cos(ang), jnp.sin(ang)
    t = pos.shape[0]
    one, zero, z8 = jnp.ones((t, HEAD - ROT_DIM), F32), jnp.zeros((t, HEAD - ROT_DIM), F32), jnp.zeros((t, half), F32)
    tab = lambda *parts: jnp.tile(jnp.concatenate(parts, axis=1), (1, LANES // HEAD))
    return tab(cos, cos, one), tab(-sin, z8, zero), tab(z8, sin, zero)


def _qkv_in_specs(mods, layer, tm, tpg, tables_per_seq, w, o):
    tab_spec = pl.BlockSpec((tm, LANES), lambda i: (i % tables_per_seq, 0))
    return [_row_spec(tm, D), _const_spec((1, D)), _mod_spec(mods, layer, 0, tpg), _mod_spec(mods, layer, 1, tpg),
            _layer_spec(w, o), tab_spec, tab_spec, tab_spec]


def _qkv(x, g, mods, layer, tm, w, o, tables):
    r = x.shape[0]
    tpg = (r // mods.shape[1]) // tm
    f32 = jax.ShapeDtypeStruct((r, D), F32)
    return pl.pallas_call(
        _qkv_body,
        grid=(r // tm,),
        in_specs=_qkv_in_specs(mods, layer, tm, tpg, 1, w, o),
        out_specs=[_row_spec(tm, D)] * 3,
        out_shape=[f32, f32, f32],
        compiler_params=_cparams("parallel"),
        name="odd_qkv_decode",
    )(x, g, mods, mods, w, *tables)


def _qkv_classes(x, g, mods, layer, tm, w, o, tables, nseq):
    r = x.shape[0]
    s_len = r // nseq
    tps = s_len // tm
    cls_spec = lambda dil: pl.BlockSpec((None, dil, tm // dil, D), lambda i: (i // tps, 0, i % tps, 0))
    cls_shape = lambda dil: jax.ShapeDtypeStruct((nseq, dil, s_len // dil, D), BF16)
    f32 = jax.ShapeDtypeStruct((nseq, D, s_len), F32)
    f32_spec = pl.BlockSpec((None, D, tm), lambda i: (i // tps, 0, i % tps))
    return pl.pallas_call(
        functools.partial(_qkv_classes_body, tps=tps),
        grid=(r // tm,),
        in_specs=_qkv_in_specs(mods, layer, tm, tps, tps, w, o),
        out_specs=[cls_spec(4)] * 3 + [cls_spec(16)] * 3 + [f32_spec] * 2 + [_row_spec(tm, D), _row_spec(tm, LANES)],
        out_shape=[cls_shape(4)] * 3 + [cls_shape(16)] * 3 + [f32, f32, jax.ShapeDtypeStruct((r, D), BF16),
                                                                jax.ShapeDtypeStruct((r, LANES), F32)],
        scratch_shapes=[pltpu.VMEM((3, tm, LANES), F32), pltpu.VMEM((3, tm, LANES), F32), pltpu.VMEM((tm, D), BF16),
                        pltpu.VMEM((BAND + tm, D), BF16), pltpu.VMEM((BAND + tm, D), BF16)],
        compiler_params=_cparams("arbitrary"),
        name="odd_qkv_attn1",
    )(x, g, mods, mods, w, *tables)


def _outproj_body(*refs, n_in):
    ins, ws = refs[:n_in], refs[n_in:2 * n_in]
    x_ref, gp_ref, gate_ref, o_ref = refs[2 * n_in:]
    y = _mm(ins[0][...], ws[0][...])
    for a, w in zip(ins[1:], ws[1:]):
        y = y + _mm(a[...], w[...])
    o_ref[...] = x_ref[...] + gate_ref[...] * (_rms(y) * gp_ref[...])


def _outproj(ins, ws, x, gp, mods, layer, tm):
    r = x.shape[0]
    tpg = (r // mods.shape[1]) // tm
    return pl.pallas_call(
        functools.partial(_outproj_body, n_in=len(ins)),
        grid=(r // tm,),
        in_specs=[_row_spec(tm, a.shape[1]) for a in ins]
        + [_layer_spec(w, l, a.shape[1], b) for a, (w, l, b) in zip(ins, ws)]
        + [_row_spec(tm, D), _const_spec((1, D)), _mod_spec(mods, layer, 2, tpg)],
        out_specs=_row_spec(tm, D),
        out_shape=jax.ShapeDtypeStruct((r, D), F32),
        compiler_params=_cparams("parallel"),
        name="outproj",
    )(*ins, *(w for w, _, _ in ws), x, gp, mods)


def _ffn_body(x_ref, g_ref, sh_ref, sc_ref, gate_ref, gp_ref, w1_ref, w2_ref, o_ref, t_scr, *, fc):
    x = x_ref[...]
    h = _modnorm(x, g_ref[...], sh_ref[...], sc_ref[...]).astype(BF16)
    for c in range(D_FF // fc):
        cols = slice(c * fc, (c + 1) * fc)
        t = jnp.maximum(_mm(h, w1_ref[:, cols]), 0.0)
        t_scr[:, cols] = (t * t).astype(BF16)
    o_ref[...] = x + gate_ref[...] * (_rms(_mm(t_scr[...], w2_ref[...])) * gp_ref[...])


def _ffn(x, g, gp, mods, layer, tm, fc, w1, w2):
    r = x.shape[0]
    tpg = (r // mods.shape[1]) // tm
    resident = lambda w: _layer_spec(w, layer, pipeline_mode=pl.Buffered(1))
    return pl.pallas_call(
        functools.partial(_ffn_body, fc=fc),
        grid=(r // tm,),
        in_specs=[_row_spec(tm, D), _const_spec((1, D)), _mod_spec(mods, layer, 3, tpg),
                  _mod_spec(mods, layer, 4, tpg), _mod_spec(mods, layer, 5, tpg), _const_spec((1, D)),
                  resident(w1), resident(w2)],
        out_specs=_row_spec(tm, D),
        out_shape=jax.ShapeDtypeStruct((r, D), F32),
        scratch_shapes=[pltpu.VMEM((tm, D_FF), BF16)],
        compiler_params=_cparams("parallel"),
        name="ffn",
    )(x, g, mods, mods, mods, gp, w1, w2)


def _seg_cumsum(x, seg):
    row = lax.broadcasted_iota(jnp.int32, (x.shape[0], 1), 0) % seg
    s = 1
    while s < seg:
        x = x + jnp.where(row >= s, pltpu.roll(x, s, 0), 0.0)
        s *= 2
    return x


def _rwkv_body(pb_ref, prev_ref, s0_ref, mu_ref, prm_ref, lora_ref, hsum_ref, out_ref, s_ref, last_scr,
               *, nb, tt, c):
    @pl.when(pl.program_id(1) == 0)
    def _():
        s_ref[...] = s0_ref[...]
        last_scr[...] = prev_ref[...]

    mu = mu_ref[...]
    w0, a0, k_k, k_a, r_k, ln_g, ln_b = (prm_ref[i:i + 1, :] for i in range(7))
    hsum = hsum_ref[...]

    def headsum(t):
        groups = [_mm(t[:, g * LANES:(g + 1) * LANES], hsum, passes=2) for g in range(B_WIDTH // LANES)]
        return jnp.concatenate(groups, axis=1)

    row = lax.broadcasted_iota(jnp.int32, (tt, 1), 0)
    ii = lax.broadcasted_iota(jnp.int32, (c, 2 * c), 0)
    jj = lax.broadcasted_iota(jnp.int32, (c, 2 * c), 1) % c
    strict, incl = jj < ii, jj <= ii
    eye = jnp.where(lax.broadcasted_iota(jnp.int32, (c, c), 0) == lax.broadcasted_iota(jnp.int32, (c, c), 1), 1.0, 0.0).astype(F32)
    zeros_c = jnp.zeros((c, HEAD), BF16)
    n_chunk = tt // c
    levels = int(math.log2(c)) - 1

    seqs = []
    for s in range(nb):
        x = pb_ref[s]
        shifted = jnp.where(row == 0, last_scr[s], pltpu.roll(x, 1, 0))
        last_scr[s] = x[tt - 1:tt, :]
        xm = x + mu * (shifted - x)
        r, k, v = xm[:, :B_WIDTH], xm[:, B_WIDTH:2 * B_WIDTH], xm[:, 2 * B_WIDTH:3 * B_WIDTH]
        lo = xm[:, 3 * B_WIDTH:]
        z = -(w0 + _mm(jnp.tanh(lo), lora_ref[0]))
        softplus = jnp.maximum(z, 0.0) + jnp.log1p(jnp.exp(-jnp.abs(z)))
        lw = -jnp.exp(-softplus - 0.5)
        a = jax.nn.sigmoid(a0 + _mm(lo, lora_ref[1]))
        g = _mm(jax.nn.sigmoid(lo), lora_ref[2])
        kk = k * k_k
        kk = kk * lax.rsqrt(headsum(kk * kk) + 1e-12)
        k2 = k * (1.0 + (a - 1.0) * k_a)
        lp = _seg_cumsum(lw, c)
        e_pos, e_neg = jnp.exp(lp), jnp.exp(-lp)
        seqs.append(dict(
            r=r, k2=k2, v=v, g=g, e_pos=e_pos,
            a_t=-kk * jnp.exp(lp - lw), b_t=kk * a * e_neg, k_t=k2 * e_neg, r_t=r * e_pos, v_b=v))

    def piece(s, name, ci, h):
        return seqs[s][name][ci * c:(ci + 1) * c, h * HEAD:(h + 1) * HEAD].astype(BF16)

    chains = [(s, h) for s in range(nb) for h in range(B_HEADS)]
    units = [(ci, s, h) for ci in range(n_chunk) for (s, h) in chains]

    a_u = {u: piece(u[1], "a_t", u[0], u[2]) for u in units}
    r_u = {u: piece(u[1], "r_t", u[0], u[2]) for u in units}
    v_u = {u: piece(u[1], "v_b", u[0], u[2]) for u in units}
    ar_u = {u: jnp.concatenate([a_u[u], r_u[u]], axis=0) for u in units}
    bk_u = {u: jnp.concatenate([piece(u[1], "b_t", u[0], u[2]), piece(u[1], "k_t", u[0], u[2])], axis=0) for u in units}
    zv_u = {u: jnp.concatenate([zeros_c, v_u[u]], axis=0) for u in units}
    g12 = {u: _mm(ar_u[u], bk_u[u], NT) for u in units}
    g1 = {u: jnp.where(strict, g12[u][:c], 0.0).astype(BF16) for u in units}
    g2 = {u: jnp.where(incl, g12[u][c:], 0.0).astype(BF16) for u in units}
    n_u = {u: g1[u][:, :c] for u in units}
    inv = {u: eye + n_u[u].astype(F32) for u in units}
    pw = {u: _mm(n_u[u], n_u[u]) for u in units}
    for level in range(levels):
        if level + 1 < levels:
            both = {u: _mm(jnp.concatenate([pw[u], inv[u]], axis=0), pw[u]) for u in units}
            pw = {u: both[u][:c] for u in units}
            inv = {u: inv[u] + both[u][c:] for u in units}
        else:
            inv = {u: inv[u] + _mm(inv[u], pw[u]) for u in units}
    xn = {u: _mm(jnp.concatenate(_split(inv[u]), axis=0), n_u[u]) for u in units}
    res = {u: eye - inv[u] + (xn[u][:c] + xn[u][c:]) for u in units}
    inv = {u: inv[u] + _mm(res[u], inv[u]) for u in units}

    state = {ch: s_ref[ch[0], ch[1]] for ch in chains}
    o_u = {}
    for ci in range(n_chunk):
        us = [(ci, s, h) for (s, h) in chains]
        ars = {u: _mm(ar_u[u], state[u[1:]], NT) for u in us}
        rhs = {u: ars[u][:c] + _mm(g1[u], zv_u[u]) for u in us}
        uu = {u: _mm(inv[u], rhs[u]) for u in us}
        uv = {u: jnp.concatenate([uu[u].astype(BF16), v_u[u]], axis=0) for u in us}
        for u in us:
            o_u[u] = ars[u][c:] + _mm(g2[u], uv[u])
        for u in us:
            s, h = u[1:]
            p_end = seqs[s]["e_pos"][(ci + 1) * c - 1:(ci + 1) * c, h * HEAD:(h + 1) * HEAD]
            state[(s, h)] = (state[(s, h)] + _mm(uv[u], bk_u[u], TN)) * p_end
    for (s, h) in chains:
        s_ref[s, h] = state[(s, h)]

    for s in range(nb):
        q = seqs[s]
        o_rows = [jnp.concatenate([o_u[(ci, s, h)] for h in range(B_HEADS)], axis=1) for ci in range(n_chunk)]
        o = o_rows[0] if n_chunk == 1 else jnp.concatenate(o_rows, axis=0)
        inv_n = 1.0 / HEAD
        oc = o - headsum(o) * inv_n
        on = oc * lax.rsqrt(headsum(oc * oc) * inv_n + GN_EPS) * ln_g + ln_b
        bonus = headsum(q["r"] * q["k2"] * r_k) * q["v"]
        out_ref[s] = ((on + bonus) * q["g"]).astype(out_ref.dtype)


def _rwkv(pb, prev, s0, mu, prm, lora, hsum, nb, tt, c):
    n, t, _ = pb.shape
    return pl.pallas_call(
        functools.partial(_rwkv_body, nb=nb, tt=tt, c=c),
        grid=(n // nb, t // tt),
        in_specs=[pl.BlockSpec((nb, tt, B_COLS), lambda i, j: (i, j, 0)),
                  pl.BlockSpec((nb, 1, B_COLS), lambda i, j: (i, 0, 0)),
                  pl.BlockSpec((nb, B_HEADS, HEAD, HEAD), lambda i, j: (i, 0, 0, 0)),
                  _const_spec(mu.shape), _const_spec(prm.shape), _const_spec(lora.shape), _const_spec(hsum.shape)],
        out_specs=[pl.BlockSpec((nb, tt, B_WIDTH), lambda i, j: (i, j, 0)),
                   pl.BlockSpec((nb, B_HEADS, HEAD, HEAD), lambda i, j: (i, 0, 0, 0))],
        out_shape=[jax.ShapeDtypeStruct((n, t, B_WIDTH), BF16), jax.ShapeDtypeStruct(s0.shape, F32)],
        scratch_shapes=[pltpu.VMEM((nb, 1, B_COLS), F32)],
        compiler_params=_cparams("parallel", "arbitrary"),
        name="rwkv7",
    )(pb, prev, s0, mu, prm, lora, hsum)


def _band_block(q_at, rows, keys, k_ref, v_ref, mask, acc_at, st_at, group):
    nk = mask.shape[1]
    lane = lax.broadcasted_iota(jnp.int32, (BAND, LANES), 1)
    lane_k = lax.broadcasted_iota(jnp.int32, (nk, LANES), 1)
    stats = jnp.zeros((BAND, LANES), F32)
    for g0 in range(0, C_HEADS // 2, group):
        pairs = range(g0, g0 + group)
        scores = []
        for pair in pairs:
            lanes = slice(pair * LANES, (pair + 1) * LANES)
            q2, k2 = q_at[rows, lanes], keys(k_ref, lanes)
            for half in range(2):
                in_head = (lane >= HEAD) if half else (lane < HEAD)
                scores.append(_mm(jnp.where(in_head, q2, jnp.zeros_like(q2)), k2, NT))
        probs = []
        for i, s in enumerate(scores):
            s = jnp.where(mask, s, NEG)
            m = jnp.max(s, axis=-1, keepdims=True)
            probs.append(jnp.exp(s - m).astype(BF16))
            stats = jnp.where(lane == (i % 2) * (C_HEADS // 2) + g0 + i // 2, m, stats)
        for i, pair in enumerate(pairs):
            lanes = slice(pair * LANES, (pair + 1) * LANES)
            v2 = keys(v_ref, lanes)
            lo = _mm(probs[2 * i], jnp.where(lane_k < HEAD, v2, jnp.ones_like(v2)))
            hi = _mm(probs[2 * i + 1], jnp.where(lane_k >= HEAD, v2, jnp.ones_like(v2)))
            acc_at[rows, lanes] = jnp.where(lane < HEAD, lo, hi).astype(acc_at.dtype)
            stats = jnp.where(lane == DEN_EVEN + pair, lo, stats)
            stats = jnp.where(lane == DEN_ODD + pair, hi, stats)
    st_at[rows, :] = stats


def _band_mask(first_block):
    qi = lax.broadcasted_iota(jnp.int32, (BAND, 2 * BAND), 0)
    kc = lax.broadcasted_iota(jnp.int32, (BAND, 2 * BAND), 1)
    return (kc >= qi) & (kc <= qi + BAND) & (kc >= jnp.where(first_block, BAND, 0))


def _band_attn_body(q_ref, k_ref, v_ref, acc_ref, st_ref, *, nblk, units):
    for j in range(units):
        if nblk == 1:
            qi = lax.broadcasted_iota(jnp.int32, (BAND, BAND), 0)
            kc = lax.broadcasted_iota(jnp.int32, (BAND, BAND), 1)
            _band_block(q_ref.at[j], slice(None), lambda ref, lanes: ref[j, :, lanes], k_ref, v_ref, kc <= qi,
                        acc_ref.at[j], st_ref.at[j], 8)
        else:
            blk = pl.program_id(2) * units + j
            cur = pl.multiple_of(blk * BAND, BAND)
            prev = pl.multiple_of(jnp.maximum(blk - 1, 0) * BAND, BAND)
            keys = lambda ref, lanes: jnp.concatenate([ref[pl.ds(prev, BAND), lanes], ref[pl.ds(cur, BAND), lanes]], axis=0)
            _band_block(q_ref, slice(j * BAND, (j + 1) * BAND), keys, k_ref, v_ref, _band_mask(blk == 0),
                        acc_ref, st_ref, 4)


def _band_attn(q, k, v, units=4):
    nseq, dil, sub, _ = q.shape
    nblk = sub // BAND
    if nblk == 1:
        grid = (nseq, dil // units, 1)
        blk = lambda width: pl.BlockSpec((None, units, sub, width), lambda b, r, n: (b, r, 0, 0))
        cls = blk(D)
    else:
        grid = (nseq, dil, nblk // units)
        blk = lambda width: pl.BlockSpec((None, None, units * BAND, width), lambda b, r, n: (b, r, n, 0))
        cls = pl.BlockSpec((None, None, sub, D), lambda b, r, n: (b, r, 0, 0))
    return pl.pallas_call(
        functools.partial(_band_attn_body, nblk=nblk, units=units),
        grid=grid,
        in_specs=[blk(D), cls, cls],
        out_specs=[blk(D), blk(LANES)],
        out_shape=[jax.ShapeDtypeStruct(q.shape, q.dtype), jax.ShapeDtypeStruct((nseq, dil, sub, LANES), F32)],
        compiler_params=_cparams("parallel", "parallel", "arbitrary"),
        name=f"band_attn_d{dil}",
    )(q, k, v)


def _mix_body(a1_ref, a4_ref, a16_ref, s1_ref, s4_ref, s16_ref, e_ref, w_ref, x_ref, gp_ref, gate_ref, o_ref,
              slab4, slab16, st4, st16, o_scr):
    tm = x_ref.shape[0]
    ngrp = D // LANES
    for dil, a_ref, s_ref, slab, st in ((4, a4_ref, s4_ref, slab4, st4), (16, a16_ref, s16_ref, slab16, st16)):
        for r in range(dil):
            rows = pl.ds(r, tm // dil, stride=dil)
            st[rows, :] = s_ref[r]
            for grp in range(ngrp):
                slab[grp, rows, :] = a_ref[r, :, grp * LANES:(grp + 1) * LANES].astype(F32)
    sts = [s1_ref[...], st4[...], st16[...]]
    lane = lax.broadcasted_iota(jnp.int32, sts[0].shape, 1)
    top = jnp.maximum(jnp.maximum(sts[0], sts[1]), sts[2])
    es = [jnp.exp(st - top) for st in sts]
    dens = [jnp.where(lane < C_HEADS // 2, pltpu.roll(st, LANES - DEN_EVEN, 1),
                      pltpu.roll(st, LANES - DEN_ODD + C_HEADS // 2, 1)) for st in sts]
    total = es[0] * dens[0] + es[1] * dens[1] + es[2] * dens[2]
    coefs = [_mm(jnp.where(lane < C_HEADS, e / total, 0.0), e_ref[...], passes=2) for e in es]
    for grp in range(ngrp):
        lanes = slice(grp * LANES, (grp + 1) * LANES)
        o = (coefs[0][:, lanes] * a1_ref[:, lanes].astype(F32) + coefs[1][:, lanes] * slab4[grp]
             + coefs[2][:, lanes] * slab16[grp])
        o_scr[:, lanes] = o.astype(BF16)
    y = _mm(o_scr[...], w_ref[...])
    o_ref[...] = x_ref[...] + gate_ref[...] * (_rms(y) * gp_ref[...])


def _mix_outproj(accs, sts, expand, w, o, x, gp, mods, layer, tm):
    r = x.shape[0]
    nseq = accs[1].shape[0]
    tps = (r // nseq) // tm
    cls_spec = lambda dil, width: pl.BlockSpec((None, dil, tm // dil, width), lambda i: (i // tps, 0, i % tps, 0))
    return pl.pallas_call(
        _mix_body,
        grid=(r // tm,),
        in_specs=[_row_spec(tm, D), cls_spec(4, D), cls_spec(16, D), _row_spec(tm, LANES), cls_spec(4, LANES),
                  cls_spec(16, LANES), _const_spec(expand.shape), _layer_spec(w, o), _row_spec(tm, D),
                  _const_spec((1, D)), _mod_spec(mods, layer, 2, tps)],
        out_specs=_row_spec(tm, D),
        out_shape=jax.ShapeDtypeStruct((r, D), F32),
        scratch_shapes=[pltpu.VMEM((D // LANES, tm, LANES), F32), pltpu.VMEM((D // LANES, tm, LANES), F32),
                        pltpu.VMEM((tm, LANES), F32), pltpu.VMEM((tm, LANES), F32), pltpu.VMEM((tm, D), BF16)],
        compiler_params=_cparams("parallel"),
        name="attn_mix_outproj",
    )(*accs, *sts, expand, w, x, gp, mods)


def _attn_sample_body(q_ref, kn_ref, vn_ref, kc_ref, vc_ref, o_ref, *, t_len, window):
    tq = lax.broadcasted_iota(jnp.int32, (t_len, window), 0)
    back = window + tq - lax.broadcasted_iota(jnp.int32, (t_len, window), 1)
    cnt = (jnp.where(back <= BAND, 1.0, 0.0) + jnp.where((back <= 4 * BAND) & ((back & 3) == 0), 1.0, 0.0)
           + jnp.where((back & 15) == 0, 1.0, 0.0))
    tn = lax.broadcasted_iota(jnp.int32, (t_len, t_len), 0)
    dn = tn - lax.broadcasted_iota(jnp.int32, (t_len, t_len), 1)
    mult = jnp.where(dn == 0, 3.0, 0.0) + jnp.where(dn > 0, 1.0, 0.0) + jnp.where(dn == 4, 1.0, 0.0)
    lane = lax.broadcasted_iota(jnp.int32, (t_len, LANES), 1)
    for pair in range(q_ref.shape[1] // LANES):
        lanes = slice(pair * LANES, (pair + 1) * LANES)
        q2, kn, vn = q_ref[:, lanes], kn_ref[:, lanes].astype(BF16), vn_ref[:, lanes].astype(BF16)
        kc, vc = kc_ref[lanes, :].astype(BF16), vc_ref[lanes, :].astype(BF16)
        outs = []
        for half in range(2):
            qh = jnp.where((lane >= HEAD) if half else (lane < HEAD), q2, 0.0).astype(BF16)
            s_c = jnp.where(cnt > 0.0, _mm(qh, kc), NEG)
            s_n = jnp.where(mult > 0.0, _mm(qh, kn, NT), NEG)
            top = jnp.maximum(jnp.max(s_c, axis=-1, keepdims=True), jnp.max(s_n, axis=-1, keepdims=True))
            p_c = jnp.exp(s_c - top) * cnt
            p_n = jnp.exp(s_n - top) * mult
            den = jnp.sum(p_c, axis=-1, keepdims=True) + jnp.sum(p_n, axis=-1, keepdims=True)
            outs.append((_mm(p_c, vc, NT) + _mm(p_n, vn)) / den)
        o_ref[:, lanes] = jnp.where(lane < HEAD, outs[0], outs[1])


def _attn_sample(q, kn, vn, cache_k, cache_v, layer, nseq):
    window = cache_k.shape[-1]
    t_len = q.shape[0] // nseq
    assert window == 16 * BAND and t_len == 8
    width = D
    new = pl.BlockSpec((t_len, width), lambda b, g: (b, g))
    cached = pl.BlockSpec((None, None, width, window), lambda b, g: (layer, b, g, 0))
    return pl.pallas_call(
        functools.partial(_attn_sample_body, t_len=t_len, window=window),
        grid=(nseq, D // width),
        in_specs=[new, new, new, cached, cached],
        out_specs=new,
        out_shape=jax.ShapeDtypeStruct(q.shape, F32),
        compiler_params=_cparams("parallel", "parallel"),
        name="attn_sample",
    )(q, kn, vn, cache_k, cache_v)


def kernel(x_prompt, x_sample, state_wkv, state_shift, cache_k, cache_v, c_prompt, c_sample, ada_w, ada_b, norm_mix_pre, norm_mix_post, norm_ffn_pre, norm_ffn_post, ffn_w1, ffn_w2, ev_w_in, ev_w_out, gm_ln_g, gm_ln_b, gm_ws, gm_bs, rw_mu, rw_w0, rw_w2, rw_a0, rw_a2, rw_g2, rw_kk, rw_ka, rw_rk, rw_lnx_g, rw_lnx_b, od_w_qkv, od_w_out):
    n_p, s_len, _ = x_prompt.shape
    n_s, t_len, _ = x_sample.shape
    r_p, r_s = n_p * s_len, n_s * t_len
    row = lambda t: t.reshape(1, -1).astype(F32)

    mods = _ada(jnp.concatenate([c_prompt, c_sample], axis=0).astype(F32), ada_w, ada_b)
    mods_p = mods[:, :n_p].reshape(DEPTH, n_p, 1, 6 * D)
    mods_s = jnp.repeat(mods[:, n_p:], t_len, axis=1).reshape(DEPTH, 1, r_s, 6 * D)

    xp = x_prompt.astype(F32).reshape(r_p, D)
    xs = x_sample.astype(F32).reshape(r_s, D)

    hsum = jnp.kron(jnp.eye(LANES // HEAD, dtype=F32), jnp.ones((HEAD, HEAD), F32)).astype(BF16)
    tile_head = 2 * (jnp.arange(LANES) % (C_HEADS // 2)) + jnp.arange(LANES) // (C_HEADS // 2)
    expand = ((tile_head[:, None] == jnp.arange(D)[None, :] // HEAD) & (jnp.arange(LANES)[:, None] < C_HEADS)).astype(BF16)
    tabs_p = _rope_tables(jnp.arange(s_len))
    tabs_s = _rope_tables(jnp.tile(PAST_LEN + jnp.arange(t_len), n_s))
    zeros_prev = jnp.zeros((n_p, 1, B_COLS), F32)
    zeros_state = jnp.zeros((n_p, B_HEADS, HEAD, HEAD), F32)
    reps = CHUNK // t_len
    feat_major = lambda c: jnp.transpose(c.astype(F32), (0, 1, 3, 4, 2)).reshape(c.shape[0], c.shape[1], D, c.shape[2])
    cache_kt, cache_vt = feat_major(cache_k), feat_major(cache_v)
    tok_major = lambda t: jnp.transpose(t.reshape(n_p, C_HEADS, HEAD, s_len), (0, 3, 1, 2))

    w_in, w_out = ev_w_in.astype(BF16), ev_w_out.astype(BF16)
    w_qkv, w_att = od_w_qkv.astype(BF16), od_w_out.astype(BF16)
    w1, w2 = ffn_w1.astype(BF16), ffn_w2.astype(BF16)
    tm_p = 1024
    outs = {k: [] for k in ("wkv_p", "shift_p", "k_p", "v_p", "wkv_s", "shift_s", "k_s", "v_s", "gv_s")}
    for l in range(DEPTH):
        g_pre, g_post = row(norm_mix_pre[l]), row(norm_mix_post[l])
        if l % 2 == 0:
            e = l // 2
            w_ab = [(w_out, e, 0), (w_out, e, 1)]
            ln_g, ln_b = row(gm_ln_g[e]), row(gm_ln_b[e])
            ws = gm_ws[e].astype(F32)
            bias_p = jnp.repeat(gm_bs[e].T, A_WIDTH // A_GROUPS, axis=1)
            ws_s = jax.vmap(lambda m: jnp.kron(jnp.eye(reps, dtype=F32), m))(ws[:, :t_len, :t_len])
            bias_s = jnp.tile(bias_p[:t_len], (reps, 1))
            mu = row(rw_mu[e])
            zpad = jnp.zeros((7 * B_WIDTH,), F32)
            prm = jnp.concatenate([rw_w0[e], rw_a0[e], rw_kk[e], rw_ka[e], rw_rk[e].reshape(-1), rw_lnx_g[e],
                                   rw_lnx_b[e], zpad[:B_WIDTH]]).reshape(8, B_WIDTH).astype(F32)
            zl = lambda a, b: jnp.zeros((a, b), F32)
            lora = jnp.stack([
                jnp.concatenate([rw_w2[e], zl(96, B_WIDTH)], axis=0),
                jnp.concatenate([zl(32, B_WIDTH), rw_a2[e], zl(64, B_WIDTH)], axis=0),
                jnp.concatenate([zl(64, B_WIDTH), rw_g2[e]], axis=0)]).astype(BF16)

            oa_p, pb_p = _inproj(xp, g_pre, mods_p, l, tm_p, w_in, e, ln_g, ln_b, ws, bias_p, False)
            pb_p3 = pb_p.reshape(n_p, s_len, B_COLS)
            ob_p, st_p = _rwkv(pb_p3, zeros_prev, zeros_state, mu, prm, lora, hsum, 2, 128, 64)
            xp = _outproj([oa_p, ob_p.reshape(r_p, B_WIDTH)], w_ab, xp, g_post, mods_p, l, tm_p)
            outs["wkv_p"].append(st_p)
            outs["shift_p"].append(pb_p3[:, -1])

            oa_s, pb_s, gv = _inproj(xs, g_pre, mods_s, l, r_s, w_in, e, ln_g, ln_b, ws_s, bias_s, True)
            pb_s3 = pb_s.reshape(n_s, t_len, B_COLS)
            ob_s, st_s = _rwkv(pb_s3, state_shift[e].astype(F32).reshape(n_s, 1, B_COLS), state_wkv[e].astype(F32),
                               mu, prm, lora, hsum, 8, t_len, t_len)
            xs = _outproj([oa_s, ob_s.reshape(r_s, B_WIDTH)], w_ab, xs, g_post, mods_s, l, r_s)
            outs["wkv_s"].append(st_s)
            outs["shift_s"].append(pb_s3[:, -1])
            outs["gv_s"].append(gv.reshape(n_s, t_len, A_WIDTH))
        else:
            o = l // 2
            q4, k4, v4, q16, k16, v16, kf, vf, acc1, st1 = _qkv_classes(xp, g_pre, mods_p, l, 512, w_qkv, o, tabs_p, n_p)
            res = [_band_attn(q4, k4, v4), _band_attn(q16, k16, v16)]
            accs = [acc1, res[0][0], res[1][0]]
            sts = [st1, res[0][1], res[1][1]]
            xp = _mix_outproj(accs, sts, expand, w_att, o, xp, g_post, mods_p, l, 1024)
            outs["k_p"].append(tok_major(kf))
            outs["v_p"].append(tok_major(vf))

            qs, kfs, vfs = _qkv(xs, g_pre, mods_s, l, r_s, w_qkv, o, tabs_s)
            heads = lambda t: t.reshape(n_s, t_len, C_HEADS, HEAD)
            att = _attn_sample(qs, kfs, vfs, cache_kt, cache_vt, o, n_s)
            xs = _outproj([att], [(w_att, o, 0)], xs, g_post, mods_s, l, r_s)
            outs["k_s"].append(heads(kfs))
            outs["v_s"].append(heads(vfs))
        g_fpre, g_fpost = row(norm_ffn_pre[l]), row(norm_ffn_post[l])
        xp = _ffn(xp, g_fpre, g_fpost, mods_p, l, 1024, 1024, w1, w2)
        xs = _ffn(xs, g_fpre, g_fpost, mods_s, l, r_s, 1024, w1, w2)

    st = lambda name: jnp.stack(outs[name])
    return (xp.reshape(n_p, s_len, D), xs.reshape(n_s, t_len, D), st("wkv_p"), st("shift_p"), st("k_p"),
            st("v_p"), st("wkv_s"), st("shift_s"), st("k_s"), st("v_s"), st("gv_s"))
```
